```python
import jax, jax.numpy as jnp
from jax import lax
import numpy as np

D_MODEL = 1024
BATCH = 4
SEQ = 8192
DEPTH = 2

MIX_WIDTH = D_MODEL
BRANCH = MIX_WIDTH // 2
GLA_HEADS = 4
GLA_DK = BRANCH // 2 // GLA_HEADS
GLA_DV = BRANCH // GLA_HEADS
GLA_RANK = 16
GLA_TAU = 16.0
GLA_CHUNK = 64
SGU_GROUPS = 4
SGU_CHUNK = 128
SGU_CH = BRANCH // SGU_GROUPS
RET_HEADS = 4
RET_DK = BRANCH // 2 // RET_HEADS
RET_DV = BRANCH // RET_HEADS
RET_CHUNK = 128
ROPE_BASE = 10000.0
POOL_WINDOWS = (2, 4, 8, 16)
POOL_CH = BRANCH // len(POOL_WINDOWS)
DN_ALPHA = (2 * DEPTH) ** 0.25
DN_BETA = (8 * DEPTH) ** -0.25
LN_EPS = 1e-5

EVEN_SIZES = (GLA_HEADS * GLA_DK, GLA_HEADS * GLA_DK, BRANCH, BRANCH, GLA_RANK,
              BRANCH, BRANCH, BRANCH)
ODD_SIZES = (RET_HEADS * RET_DK, RET_HEADS * RET_DK, BRANCH, BRANCH,
             BRANCH, BRANCH)

kernel_name = "hybrid_gla_sgu_retention_pool_deepnorm"


def _split(h, sizes):
    idx = [int(i) for i in np.cumsum(sizes)[:-1]]
    return jnp.split(h, idx, axis=-1)


def _layer_norm(x, g, b):
    xf = x.astype(jnp.float32)
    mu = jnp.mean(xf, axis=-1, keepdims=True)
    var = jnp.mean(jnp.square(xf - mu), axis=-1, keepdims=True)
    return ((xf - mu) * lax.rsqrt(var + LN_EPS) * g + b).astype(x.dtype)


def _head_rmsnorm(o, g):
    of = o.astype(jnp.float32)
    return of * lax.rsqrt(jnp.mean(jnp.square(of), axis=-1, keepdims=True) + LN_EPS) * g


def _head_groupnorm(o, g):
    of = o.astype(jnp.float32)
    mu = jnp.mean(of, axis=-1, keepdims=True)
    var = jnp.mean(jnp.square(of - mu), axis=-1, keepdims=True)
    return (of - mu) * lax.rsqrt(var + LN_EPS) * g


def _chunk_scan(decay, chunk_state):
    def step(s_prev, inp):
        dec, cs = inp
        return dec * s_prev + cs, s_prev
    init = jnp.zeros(chunk_state.shape[1:], jnp.float32)
    _, s_in = lax.scan(step, init, (decay, chunk_state))
    return s_in


def _gla(q, k, v, log_a):
    B, S, H, dk = q.shape
    dv = v.shape[-1]
    C = GLA_CHUNK
    n = S // C
    f32 = jnp.float32
    q = q.astype(f32).reshape(B, n, C, H, dk) * (dk ** -0.5)
    k = k.astype(f32).reshape(B, n, C, H, dk)
    v = v.astype(f32).reshape(B, n, C, H, dv)
    b = jnp.cumsum(log_a.astype(f32).reshape(B, n, C, H, dk), axis=2)
    q_dec = q * jnp.exp(b)
    k_dec = k * jnp.exp(-b)
    causal = jnp.tril(jnp.ones((C, C), dtype=bool))
    scores = jnp.einsum('bnihd,bnjhd->bnhij', q_dec, k_dec)
    scores = jnp.where(causal, scores, 0.0)
    o_intra = jnp.einsum('bnhij,bnjhe->bnihe', scores, v)
    b_last = b[:, :, -1]
    k_state = k * jnp.exp(b_last[:, :, None] - b)
    chunk_state = jnp.einsum('bnjhd,bnjhe->bnhde', k_state, v)
    decay = jnp.exp(b_last)[..., None]
    s_in = _chunk_scan(jnp.moveaxis(decay, 1, 0), jnp.moveaxis(chunk_state, 1, 0))
    s_in = jnp.moveaxis(s_in, 0, 1)
    o_inter = jnp.einsum('bnihd,bnhde->bnihe', q_dec, s_in)
    return (o_intra + o_inter).reshape(B, S, H, dv)


def _spatial_gating(u, sv, ln_g, ln_b, w_s, b_s):
    B, S, _ = u.shape
    n = S // SGU_CHUNK
    svr = sv.reshape(B, n, SGU_CHUNK, SGU_GROUPS, SGU_CH)
    svr = _layer_norm(svr, ln_g, ln_b)
    w = jnp.where(jnp.tril(jnp.ones((SGU_CHUNK, SGU_CHUNK), dtype=bool)), w_s, 0.0)
    s = jnp.einsum('gts,bnsgc->bntgc', w, svr) + jnp.transpose(b_s)[None, None, :, :, None]
    return u * s.reshape(B, S, BRANCH).astype(u.dtype)


def _rotary(x, positions):
    half = x.shape[-1] // 2
    inv = ROPE_BASE ** (-jnp.arange(half, dtype=jnp.float32) / half)
    ang = positions.astype(jnp.float32)[..., None] * inv
    cos = jnp.cos(ang)[:, :, None, :]
    sin = jnp.sin(ang)[:, :, None, :]
    xf = x.astype(jnp.float32)
    x1, x2 = xf[..., :half], xf[..., half:]
    return jnp.concatenate([x1 * cos - x2 * sin, x1 * sin + x2 * cos], axis=-1)


def _retention(q, k, v):
    B, S, H, dk = q.shape
    dv = v.shape[-1]
    C = RET_CHUNK
    n = S // C
    f32 = jnp.float32
    log_gamma = jnp.log(1.0 - 2.0 ** (-5.0 - jnp.arange(H, dtype=f32)))
    q = q.astype(f32).reshape(B, n, C, H, dk) * (dk ** -0.5)
    k = k.astype(f32).reshape(B, n, C, H, dk)
    v = v.astype(f32).reshape(B, n, C, H, dv)
    idx = jnp.arange(C, dtype=f32)
    rel = idx[:, None] - idx[None, :]
    dmat = jnp.where(rel >= 0, jnp.exp(jnp.maximum(rel, 0.0)[None] * log_gamma[:, None, None]), 0.0)
    scores = jnp.einsum('bnihd,bnjhd->bnhij', q, k) * dmat
    o_intra = jnp.einsum('bnhij,bnjhe->bnihe', scores, v)
    w_state = jnp.exp((C - 1.0 - idx)[None] * log_gamma[:, None])
    chunk_state = jnp.einsum('bnjhd,hj,bnjhe->bnhde', k, w_state, v)
    chunk_decay = jnp.exp(C * log_gamma)[None, None, :, None, None]
    chunk_decay = jnp.broadcast_to(chunk_decay, (B, n, H, 1, 1))
    s_in = _chunk_scan(jnp.moveaxis(chunk_decay, 1, 0), jnp.moveaxis(chunk_state, 1, 0))
    s_in = jnp.moveaxis(s_in, 0, 1)
    w_inter = jnp.exp((idx + 1.0)[None] * log_gamma[:, None])
    o_inter = jnp.einsum('bnihd,hi,bnhde->bnihe', q, w_inter, s_in)
    return (o_intra + o_inter).reshape(B, S, H, dv)


def _multiscale_pool(p, w_pool, scale):
    B, S, _ = p.shape
    pf = p.astype(jnp.float32)
    cs = jnp.concatenate([jnp.zeros((B, 1, BRANCH), jnp.float32), jnp.cumsum(pf, axis=1)], axis=1)
    t = jnp.arange(S)
    outs = []
    for g, w in enumerate(POOL_WINDOWS):
        lo, hi = g * POOL_CH, (g + 1) * POOL_CH
        start = jnp.maximum(t + 1 - w, 0)
        win_sum = cs[:, 1:, lo:hi] - cs[:, start, lo:hi]
        cnt = jnp.minimum(t + 1, w).astype(jnp.float32)[None, :, None]
        outs.append(win_sum / cnt - pf[..., lo:hi])
    pooled = jnp.concatenate(outs, axis=-1).reshape(B, S, len(POOL_WINDOWS), POOL_CH)
    y = jnp.einsum('bsgc,gcd->bsgd', pooled, w_pool).reshape(B, S, BRANCH) * scale
    return y.astype(p.dtype)


def _even_layer(x, w_in, w_a2, b_a, gla_norm_g, sgu_ln_g, sgu_ln_b, w_s, b_s, w_out, ln_g, ln_b):
    B, S, _ = x.shape
    h = x @ w_in
    q, k, v, g_a, a_lr, u, sv, g_b = _split(h, EVEN_SIZES)
    log_a = jax.nn.log_sigmoid((a_lr @ w_a2 + b_a).astype(jnp.float32)) / GLA_TAU
    o = _gla(q.reshape(B, S, GLA_HEADS, GLA_DK), k.reshape(B, S, GLA_HEADS, GLA_DK),
             v.reshape(B, S, GLA_HEADS, GLA_DV), log_a.reshape(B, S, GLA_HEADS, GLA_DK))
    o_a = _head_rmsnorm(o, gla_norm_g).reshape(B, S, BRANCH).astype(x.dtype) * jax.nn.silu(g_a)
    o_b = _spatial_gating(jax.nn.gelu(u), jax.nn.gelu(sv), sgu_ln_g, sgu_ln_b, w_s, b_s) * jax.nn.silu(g_b)
    y = jnp.concatenate([o_a, o_b], axis=-1) @ w_out
    return _layer_norm(DN_ALPHA * x + y, ln_g, ln_b)


def _odd_layer(x, positions, w_in, ret_norm_g, w_pool, pool_scale, w_out, ln_g, ln_b):
    B, S, _ = x.shape
    h = x @ w_in
    q, k, v, g_c, p, g_d = _split(h, ODD_SIZES)
    qr = _rotary(q.reshape(B, S, RET_HEADS, RET_DK), positions)
    kr = _rotary(k.reshape(B, S, RET_HEADS, RET_DK), positions)
    o = _retention(qr, kr, v.reshape(B, S, RET_HEADS, RET_DV))
    o_c = _head_groupnorm(o, ret_norm_g).reshape(B, S, BRANCH).astype(x.dtype) * jax.nn.silu(g_c)
    o_d = _multiscale_pool(p, w_pool, pool_scale) * jax.nn.silu(g_d)
    y = jnp.concatenate([o_c, o_d], axis=-1) @ w_out
    return _layer_norm(DN_ALPHA * x + y, ln_g, ln_b)


def setup_inputs(seed: int = 0) -> dict:
    key = jax.random.key(seed)
    ks = jax.random.split(key, 24)
    f32 = jnp.float32
    nrm = lambda k, shape, s: jax.random.normal(k, shape, f32) * s
    d_even = sum(EVEN_SIZES)
    d_odd = sum(ODD_SIZES)
    return {
        "x": jax.random.normal(ks[0], (BATCH, SEQ, D_MODEL), f32),
        "positions": jnp.broadcast_to(jnp.arange(SEQ, dtype=jnp.int32), (BATCH, SEQ)),
        "l0_w_in": nrm(ks[1], (D_MODEL, d_even), D_MODEL ** -0.5),
        "l0_w_a2": nrm(ks[2], (GLA_RANK, GLA_HEADS * GLA_DK), GLA_RANK ** -0.5),
        "l0_b_a": nrm(ks[3], (GLA_HEADS * GLA_DK,), 0.1),
        "l0_gla_norm_g": 1.0 + nrm(ks[4], (GLA_HEADS, GLA_DV), 0.1),
        "l0_sgu_ln_g": 1.0 + nrm(ks[5], (SGU_GROUPS, SGU_CH), 0.1),
        "l0_sgu_ln_b": nrm(ks[6], (SGU_GROUPS, SGU_CH), 0.02),
        "l0_w_s": nrm(ks[7], (SGU_GROUPS, SGU_CHUNK, SGU_CHUNK), 0.5 * SGU_CHUNK ** -0.5),
        "l0_b_s": 1.0 + nrm(ks[8], (SGU_GROUPS, SGU_CHUNK), 0.1),
        "l0_w_out": nrm(ks[9], (MIX_WIDTH, D_MODEL), DN_BETA * MIX_WIDTH ** -0.5),
        "l0_ln_g": 1.0 + nrm(ks[10], (D_MODEL,), 0.1),
        "l0_ln_b": nrm(ks[11], (D_MODEL,), 0.02),
        "l1_w_in": nrm(ks[12], (D_MODEL, d_odd), D_MODEL ** -0.5),
        "l1_ret_norm_g": 1.0 + nrm(ks[13], (RET_HEADS, RET_DV), 0.1),
        "l1_w_pool": nrm(ks[14], (len(POOL_WINDOWS), POOL_CH, POOL_CH), POOL_CH ** -0.5),
        "l1_pool_scale": 0.5 + nrm(ks[15], (BRANCH,), 0.05),
        "l1_w_out": nrm(ks[16], (MIX_WIDTH, D_MODEL), DN_BETA * MIX_WIDTH ** -0.5),
        "l1_ln_g": 1.0 + nrm(ks[17], (D_MODEL,), 0.1),
        "l1_ln_b": nrm(ks[18], (D_MODEL,), 0.02),
    }


def reference(x, positions, l0_w_in, l0_w_a2, l0_b_a, l0_gla_norm_g, l0_sgu_ln_g, l0_sgu_ln_b,
              l0_w_s, l0_b_s, l0_w_out, l0_ln_g, l0_ln_b, l1_w_in, l1_ret_norm_g, l1_w_pool,
              l1_pool_scale, l1_w_out, l1_ln_g, l1_ln_b):
    even_params = [(l0_w_in, l0_w_a2, l0_b_a, l0_gla_norm_g, l0_sgu_ln_g, l0_sgu_ln_b,
                    l0_w_s, l0_b_s, l0_w_out, l0_ln_g, l0_ln_b)]
    odd_params = [(l1_w_in, l1_ret_norm_g, l1_w_pool, l1_pool_scale, l1_w_out, l1_ln_g, l1_ln_b)]
    for layer in range(DEPTH):
        if layer % 2 == 0:
            x = _even_layer(x, *even_params[layer // 2])
        else:
            x = _odd_layer(x, positions, *odd_params[layer // 2])
    return x
```

```python
import functools

import jax
import jax.numpy as jnp
import numpy as np
from jax import lax
from jax.experimental import pallas as pl
from jax.experimental.pallas import tpu as pltpu

F32 = jnp.float32
BF16 = jnp.bfloat16

D_MODEL = 1024
BRANCH = 512
HEADS = 4
DK = 64
DV = 128
QK = HEADS * DK
GLA_RANK = 16
GLA_TAU = 16.0
GLA_CHUNK = 64
SGU_CHUNK = 128
RET_CHUNK = 128
ROPE_BASE = 10000.0
POOL_WINDOWS = (2, 4, 8, 16)
POOL_PAD = 16
DN_ALPHA = 4.0 ** 0.25
LN_EPS = 1e-5
LANES = 128
TOKEN_BLOCK = 512
VMEM_LIMIT_BYTES = 56 * 1024 * 1024


def _dot(a, b):
    return jnp.dot(a, b, preferred_element_type=F32)


def _dot_nt(a, b):
    return lax.dot_general(a, b, (((1,), (1,)), ((), ())), preferred_element_type=F32)


def _silu(x):
    return x / (1.0 + jnp.exp(-x))


def _gelu_tanh(x):
    c = np.float32(np.sqrt(2.0 / np.pi))
    return 0.5 * x * (1.0 + jnp.tanh(c * (x + 0.044715 * (x * x * x))))


def _log_sigmoid(z):
    return jnp.minimum(z, 0.0) - jnp.log(1.0 + jnp.exp(-jnp.abs(z)))


def _layer_norm_rows(r, g, b):
    mu = jnp.mean(r, axis=-1, keepdims=True)
    c = r - mu
    var = jnp.mean(c * c, axis=-1, keepdims=True)
    return c * lax.rsqrt(var + LN_EPS) * g + b


def _head_mask(h, natural):
    lane = lax.broadcasted_iota(jnp.int32, (1, QK), 1)
    head = lane // DK if natural else (lane % LANES) // (DK // 2)
    return (head == h).astype(F32)


def _even_kernel(x_ref, wa_ref, wlr_ref, wa2_ref, ba_ref, gng_ref, wb_ref, slg_ref, slb_ref, ws_ref,
                 bs_ref, wout_ref, lng_ref, lnb_ref, o_ref, q_s, k_s, v_s, la_s, oa_s, st_s):
    tb = x_ref.shape[1]

    @pl.when(pl.program_id(1) == 0)
    def _():
        st_s[...] = jnp.zeros_like(st_s)

    x = x_ref[0]
    xb = x.astype(BF16)

    q_s[...] = _dot(xb, wa_ref[:, 0:QK])
    k_s[...] = _dot(xb, wa_ref[:, QK:2 * QK])
    v_s[...] = _dot(xb, wa_ref[:, 2 * QK:2 * QK + BRANCH]).astype(BF16)
    a_lr = _dot(xb, wlr_ref[...])
    z = _dot(a_lr.astype(BF16), wa2_ref[...]) + ba_ref[...]
    la_s[...] = _log_sigmoid(z) * (1.0 / GLA_TAU)

    c_len = GLA_CHUNK
    row = lax.broadcasted_iota(jnp.int32, (c_len, c_len), 0)
    col = lax.broadcasted_iota(jnp.int32, (c_len, c_len), 1)
    causal = row >= col
    tril = causal.astype(BF16)
    masks = [_head_mask(h, True) for h in range(HEADS)]
    gng = gng_ref[...]

    def gla_chunk(c, carry):
        r = pl.ds(pl.multiple_of(c * c_len, c_len), c_len)
        q = q_s[r, :]
        k = k_s[r, :]
        v = v_s[r, :]
        la = la_s[r, :]
        la_hi = la.astype(BF16)
        la_lo = (la - la_hi.astype(F32)).astype(BF16)
        b = _dot(tril, la_hi) + _dot(tril, la_lo)
        b_last = b[c_len - 1:c_len, :]
        q_dec = q * jnp.exp(b) * (DK ** -0.5)
        k_dec = (k * jnp.exp(-b))
        k_dec_b = k_dec.astype(BF16)
        s_b = st_s[...].astype(BF16)
        for h in range(HEADS):
            qm = (q_dec * masks[h]).astype(BF16)
            sc = jnp.where(causal, _dot_nt(qm, k_dec_b), 0.0)
            o = _dot(sc.astype(BF16), v[:, h * DV:(h + 1) * DV]) + _dot(qm, s_b)
            o = o * lax.rsqrt(jnp.mean(o * o, axis=-1, keepdims=True) + LN_EPS)
            oa_s[r, h * DV:(h + 1) * DV] = o * gng[:, h * DV:(h + 1) * DV]
        kt = k_dec.T.astype(BF16)
        dcol = jnp.exp(jnp.broadcast_to(b_last, (LANES, QK)).T)
        for h in range(HEADS):
            rows = slice(h * DK, (h + 1) * DK)
            upd = _dot(kt[rows, :], v[:, h * DV:(h + 1) * DV])
            st_s[rows, :] = dcol[rows, :] * (st_s[rows, :] + upd)
        return carry

    lax.fori_loop(0, tb // c_len, gla_chunk, 0)

    g_a = _dot(xb, wa_ref[:, 2 * QK + BRANCH:2 * QK + 2 * BRANCH])
    o_a = oa_s[...] * _silu(g_a)
    y = _dot(o_a.astype(BF16), wout_ref[0:BRANCH, :])

    u = _gelu_tanh(_dot(xb, wb_ref[:, 0:BRANCH]))
    sv = _gelu_tanh(_dot(xb, wb_ref[:, BRANCH:2 * BRANCH]))
    g_b = _dot(xb, wb_ref[:, 2 * BRANCH:3 * BRANCH])
    t_len = SGU_CHUNK
    trow = lax.broadcasted_iota(jnp.int32, (t_len, t_len), 0)
    tcol = lax.broadcasted_iota(jnp.int32, (t_len, t_len), 1)
    slg = slg_ref[...]
    slb = slb_ref[...]
    s_groups = []
    for g in range(BRANCH // LANES):
        cols = slice(g * LANES, (g + 1) * LANES)
        w = jnp.where(trow >= tcol, ws_ref[g], 0.0).astype(BF16)
        svn = _layer_norm_rows(sv[:, cols], slg[:, cols], slb[:, cols]).astype(BF16)
        s_chunks = [_dot(w, svn[n * t_len:(n + 1) * t_len, :]) for n in range(tb // t_len)]
        s_groups.append(jnp.concatenate(s_chunks, axis=0))
    s = jnp.concatenate(s_groups, axis=1) + jnp.concatenate([bs_ref[...]] * (tb // t_len), axis=0)
    o_b = u * s * _silu(g_b)
    y = y + _dot(o_b.astype(BF16), wout_ref[BRANCH:2 * BRANCH, :])

    o_ref[0] = _layer_norm_rows(DN_ALPHA * x + y, lng_ref[...], lnb_ref[...])


def _odd_kernel(x_ref, pos_ref, w_ref, inv_ref, dmat_ref, wint_ref, wst_ref, dec_ref, rng_ref, wp_ref,
                psc_ref, wout_ref, lng_ref, lnb_ref, o_ref, q_s, k_s, v_s, oc_s, st_s, pad_s):
    tb = x_ref.shape[1]
    i = pl.program_id(1)

    @pl.when(i == 0)
    def _():
        st_s[...] = jnp.zeros_like(st_s)
        pad_s[0:POOL_PAD, :] = jnp.zeros((POOL_PAD, BRANCH), F32)

    x = x_ref[0]
    xb = x.astype(BF16)

    ang = pos_ref[0].astype(F32) * inv_ref[...]
    cos = jnp.cos(ang)
    sin = jnp.sin(ang)
    q = _dot(xb, w_ref[:, 0:QK]) * (DK ** -0.5)
    k = _dot(xb, w_ref[:, QK:2 * QK])
    q1, q2 = q[:, :LANES], q[:, LANES:]
    k1, k2 = k[:, :LANES], k[:, LANES:]
    q_s[...] = jnp.concatenate([q1 * cos - q2 * sin, q1 * sin + q2 * cos], axis=1)
    k_s[...] = jnp.concatenate([k1 * cos - k2 * sin, k1 * sin + k2 * cos], axis=1)
    v_s[...] = _dot(xb, w_ref[:, 2 * QK:2 * QK + BRANCH]).astype(BF16)

    c_len = RET_CHUNK
    masks = [_head_mask(h, False) for h in range(HEADS)]
    half = DK // 2
    rng = rng_ref[...]

    def ret_chunk(c, carry):
        r = pl.ds(pl.multiple_of(c * c_len, c_len), c_len)
        qr = q_s[r, :]
        kr = k_s[r, :]
        v = v_s[r, :]
        kr_b = kr.astype(BF16)
        s_b = st_s[...].astype(BF16)
        for h in range(HEADS):
            qm = (qr * masks[h]).astype(BF16)
            sc = _dot_nt(qm, kr_b) * dmat_ref[h]
            o = _dot(sc.astype(BF16), v[:, h * DV:(h + 1) * DV]) + wint_ref[h] * _dot(qm, s_b)
            mu = jnp.mean(o, axis=-1, keepdims=True)
            oc = o - mu
            var = jnp.mean(oc * oc, axis=-1, keepdims=True)
            oc_s[r, h * DV:(h + 1) * DV] = oc * lax.rsqrt(var + LN_EPS) * rng[:, h * DV:(h + 1) * DV]
        kt = (kr * wst_ref[...]).T.astype(BF16)
        for h in range(HEADS):
            lo = slice(h * half, (h + 1) * half)
            hi = slice(LANES + h * half, LANES + (h + 1) * half)
            vh = v[:, h * DV:(h + 1) * DV]
            st_s[lo, :] = dec_ref[lo, :] * st_s[lo, :] + _dot(kt[lo, :], vh)
            st_s[hi, :] = dec_ref[hi, :] * st_s[hi, :] + _dot(kt[hi, :], vh)
        return carry

    lax.fori_loop(0, tb // c_len, ret_chunk, 0)

    g_c = _dot(xb, w_ref[:, 2 * QK + BRANCH:2 * QK + 2 * BRANCH])
    o_c = oc_s[...] * _silu(g_c)
    y = _dot(o_c.astype(BF16), wout_ref[0:BRANCH, :])

    p = _dot(xb, w_ref[:, 2 * QK + 2 * BRANCH:2 * QK + 3 * BRANCH])
    g_d = _dot(xb, w_ref[:, 2 * QK + 3 * BRANCH:2 * QK + 4 * BRANCH])
    pad_s[POOL_PAD:POOL_PAD + tb, :] = p
    t_seq = i * tb + lax.broadcasted_iota(jnp.int32, (tb, LANES), 0)
    pooled = []
    for g, win in enumerate(POOL_WINDOWS):
        cols = slice(g * LANES, (g + 1) * LANES)
        acc = p[:, cols]
        for d in range(1, win):
            acc = acc + pad_s[POOL_PAD - d:POOL_PAD - d + tb, cols]
        cnt = jnp.minimum(t_seq + 1, win).astype(F32)
        pooled.append((acc / cnt - p[:, cols]).astype(BF16))
    tail = pad_s[tb:tb + POOL_PAD, :]
    pad_s[0:POOL_PAD, :] = tail
    yd = jnp.concatenate([_dot(pooled[g], wp_ref[g]) for g in range(len(POOL_WINDOWS))], axis=1)
    o_d = yd * psc_ref[...] * _silu(g_d)
    y = y + _dot(o_d.astype(BF16), wout_ref[BRANCH:2 * BRANCH, :])

    o_ref[0] = _layer_norm_rows(DN_ALPHA * x + y, lng_ref[...], lnb_ref[...])


def _full_spec(a):
    nd = a.ndim
    return pl.BlockSpec(a.shape, lambda b, i, _nd=nd: (0,) * _nd)


def _compiler_params():
    return pltpu.CompilerParams(dimension_semantics=("arbitrary", "arbitrary"),
                                vmem_limit_bytes=VMEM_LIMIT_BYTES)


def _row(a):
    return a.reshape(1, -1).astype(F32)


def _even_layer(x, w_in, w_a2, b_a, gla_norm_g, sgu_ln_g, sgu_ln_b, w_s, b_s, w_out, ln_g, ln_b):
    B, S, D = x.shape
    tb = TOKEN_BLOCK
    a_cols = 2 * QK + 2 * BRANCH
    wa = w_in[:, :a_cols].astype(BF16)
    wlr = jnp.pad(w_in[:, a_cols:a_cols + GLA_RANK], ((0, 0), (0, LANES - GLA_RANK))).astype(BF16)
    wa2 = jnp.pad(w_a2, ((0, LANES - GLA_RANK), (0, 0))).astype(BF16)
    wb = w_in[:, a_cols + GLA_RANK:].astype(BF16)
    bs = jnp.repeat(jnp.transpose(b_s), LANES, axis=1).astype(F32)
    params = (wa, wlr, wa2, _row(b_a), _row(gla_norm_g), wb, _row(sgu_ln_g), _row(sgu_ln_b),
              w_s.astype(F32), bs, w_out.astype(BF16), _row(ln_g), _row(ln_b))
    tok_spec = pl.BlockSpec((1, tb, D), lambda b, i: (b, i, 0))
    return pl.pallas_call(
        _even_kernel,
        out_shape=jax.ShapeDtypeStruct((B, S, D), F32),
        grid=(B, S // tb),
        in_specs=[tok_spec] + [_full_spec(p) for p in params],
        out_specs=tok_spec,
        scratch_shapes=[
            pltpu.VMEM((tb, QK), F32),
            pltpu.VMEM((tb, QK), F32),
            pltpu.VMEM((tb, BRANCH), BF16),
            pltpu.VMEM((tb, QK), F32),
            pltpu.VMEM((tb, BRANCH), F32),
            pltpu.VMEM((QK, DV), F32),
        ],
        compiler_params=_compiler_params(),
        name="even_layer_gla_sgu",
    )(x, *params)


def _retention_tables():
    c = RET_CHUNK
    log_gamma = np.log(1.0 - 2.0 ** (-5.0 - np.arange(HEADS, dtype=np.float64)))
    idx = np.arange(c, dtype=np.float64)
    rel = idx[:, None] - idx[None, :]
    dmat = np.where(rel >= 0, np.exp(np.maximum(rel, 0.0)[None] * log_gamma[:, None, None]), 0.0)
    w_inter = np.exp((idx + 1.0)[None] * log_gamma[:, None])
    w_state = np.exp((c - 1.0 - idx)[None] * log_gamma[:, None])
    lane_head = (np.arange(QK) % LANES) // (DK // 2)
    wint = np.broadcast_to(w_inter[:, :, None], (HEADS, c, DV))
    wst = w_state[lane_head, :].T
    dec = np.broadcast_to(np.exp(c * log_gamma)[lane_head][:, None], (QK, DV))
    inv = ROPE_BASE ** (-np.arange(DK // 2, dtype=np.float64) / (DK // 2))
    inv = np.tile(inv, HEADS)[None, :]
    f = lambda a: jnp.asarray(np.ascontiguousarray(a), F32)
    return f(inv), f(dmat), f(wint), f(wst), f(dec)


def _odd_layer(x, positions, w_in, ret_norm_g, w_pool, pool_scale, w_out, ln_g, ln_b):
    B, S, D = x.shape
    tb = TOKEN_BLOCK
    half = DK // 2
    j = np.arange(QK)
    perm = ((j % LANES) // half) * DK + (j // LANES) * half + (j % half)
    wq = w_in[:, 0:QK][:, perm]
    wk = w_in[:, QK:2 * QK][:, perm]
    w = jnp.concatenate([wq, wk, w_in[:, 2 * QK:]], axis=1).astype(BF16)
    inv, dmat, wint, wst, dec = _retention_tables()
    params = (w, inv, dmat, wint, wst, dec, _row(ret_norm_g), w_pool.astype(BF16), _row(pool_scale),
              w_out.astype(BF16), _row(ln_g), _row(ln_b))
    tok_spec = pl.BlockSpec((1, tb, D), lambda b, i: (b, i, 0))
    pos_spec = pl.BlockSpec((1, tb, 1), lambda b, i: (b, i, 0))
    return pl.pallas_call(
        _odd_kernel,
        out_shape=jax.ShapeDtypeStruct((B, S, D), F32),
        grid=(B, S // tb),
        in_specs=[tok_spec, pos_spec] + [_full_spec(p) for p in params],
        out_specs=tok_spec,
        scratch_shapes=[
            pltpu.VMEM((tb, QK), F32),
            pltpu.VMEM((tb, QK), F32),
            pltpu.VMEM((tb, BRANCH), BF16),
            pltpu.VMEM((tb, BRANCH), F32),
            pltpu.VMEM((QK, DV), F32),
            pltpu.VMEM((POOL_PAD + tb, BRANCH), F32),
        ],
        compiler_params=_compiler_params(),
        name="odd_layer_retention_pool",
    )(x, positions.reshape(B, S, 1), *params)


def kernel(x, positions, l0_w_in, l0_w_a2, l0_b_a, l0_gla_norm_g, l0_sgu_ln_g, l0_sgu_ln_b, l0_w_s, l0_b_s, l0_w_out, l0_ln_g, l0_ln_b, l1_w_in, l1_ret_norm_g, l1_w_pool, l1_pool_scale, l1_w_out, l1_ln_g, l1_ln_b):
    x = _even_layer(x, l0_w_in, l0_w_a2, l0_b_a, l0_gla_norm_g, l0_sgu_ln_g, l0_sgu_ln_b, l0_w_s, l0_b_s,
                    l0_w_out, l0_ln_g, l0_ln_b)
    return _odd_layer(x, positions, l1_w_in, l1_ret_norm_g, l1_w_pool, l1_pool_scale, l1_w_out, l1_ln_g, l1_ln_b)
```

```python
import functools

import jax
import jax.numpy as jnp
import numpy as np
from jax import lax
from jax.experimental import pallas as pl
from jax.experimental.pallas import tpu as pltpu

F32 = jnp.float32
BF16 = jnp.bfloat16

D_MODEL = 1024
BRANCH = 512
HEADS = 4
DK = 64
DV = 128
QK = HEADS * DK
GLA_RANK = 16
GLA_TAU = 16.0
GLA_CHUNK = 64
SGU_CHUNK = 128
RET_CHUNK = 128
ROPE_BASE = 10000.0
POOL_WINDOWS = (2, 4, 8, 16)
POOL_PAD = 16
DN_ALPHA = 4.0 ** 0.25
LN_EPS = 1e-5
LANES = 128
TOKEN_BLOCK = 512
VMEM_LIMIT_BYTES = 56 * 1024 * 1024


def _dot(a, b):
    return jnp.dot(a, b, preferred_element_type=F32)


def _dot_nt(a, b):
    return lax.dot_general(a, b, (((1,), (1,)), ((), ())), preferred_element_type=F32)


def _silu(x):
    return x / (1.0 + jnp.exp(-x))


def _gelu_tanh(x):
    c = np.float32(np.sqrt(2.0 / np.pi))
    return 0.5 * x * (1.0 + jnp.tanh(c * (x + 0.044715 * (x * x * x))))


def _log_sigmoid(z):
    return jnp.minimum(z, 0.0) - jnp.log(1.0 + jnp.exp(-jnp.abs(z)))


def _layer_norm_rows(r, g, b):
    mu = jnp.mean(r, axis=-1, keepdims=True)
    c = r - mu
    var = jnp.mean(c * c, axis=-1, keepdims=True)
    return c * lax.rsqrt(var + LN_EPS) * g + b


def _head_mask(h, natural):
    lane = lax.broadcasted_iota(jnp.int32, (1, QK), 1)
    head = lane // DK if natural else (lane % LANES) // (DK // 2)
    return (head == h).astype(F32)


def _even_kernel(x_ref, wa_ref, wlr_ref, wa2_ref, ba_ref, gng_ref, wb_ref, slg_ref, slb_ref, ws_ref,
                 bs_ref, wout_ref, lng_ref, lnb_ref, o_ref, st_s):
    tb = x_ref.shape[1]

    @pl.when(pl.program_id(1) == 0)
    def _():
        st_s[...] = jnp.zeros_like(st_s)

    x = x_ref[0]
    xb = x.astype(BF16)

    q = _dot(xb, wa_ref[:, 0:QK])
    k = _dot(xb, wa_ref[:, QK:2 * QK])
    v = _dot(xb, wa_ref[:, 2 * QK:2 * QK + BRANCH]).astype(BF16)
    a_lr = _dot(xb, wlr_ref[...])
    z = _dot(a_lr.astype(BF16), wa2_ref[...]) + ba_ref[...]
    la = _log_sigmoid(z) * (1.0 / GLA_TAU)

    c_len = GLA_CHUNK
    row = lax.broadcasted_iota(jnp.int32, (LANES, LANES), 0)
    col = lax.broadcasted_iota(jnp.int32, (LANES, LANES), 1)
    tril2 = ((row >= col) & (row // c_len == col // c_len)).astype(BF16)
    la_hi = la.astype(BF16)
    la_lo = (la - la_hi.astype(F32)).astype(BF16)
    b = jnp.concatenate(
        [_dot(tril2, la_hi[n * LANES:(n + 1) * LANES]) + _dot(tril2, la_lo[n * LANES:(n + 1) * LANES])
         for n in range(tb // LANES)], axis=0)
    q_dec = q * jnp.exp(b) * (DK ** -0.5)
    k_dec = k * jnp.exp(-b)
    k_dec_b = k_dec.astype(BF16)

    crow = lax.broadcasted_iota(jnp.int32, (HEADS * c_len, c_len), 0) % c_len
    ccol = lax.broadcasted_iota(jnp.int32, (HEADS * c_len, c_len), 1)
    causal = crow >= ccol
    masks = [_head_mask(h, True) for h in range(HEADS)]
    gng = gng_ref[...]
    state = st_s[...]
    o_chunks = []
    for c in range(tb // c_len):
        sl = slice(c * c_len, (c + 1) * c_len)
        qd = q_dec[sl]
        vc = v[sl]
        qm = jnp.concatenate([qd * masks[h] for h in range(HEADS)], axis=0).astype(BF16)
        sc = jnp.where(causal, _dot_nt(qm, k_dec_b[sl]), 0.0).astype(BF16)
        o_inter = _dot(qm, state.astype(BF16))
        o_heads = []
        for h in range(HEADS):
            rows = slice(h * c_len, (h + 1) * c_len)
            o = _dot(sc[rows], vc[:, h * DV:(h + 1) * DV]) + o_inter[rows]
            o = o * lax.rsqrt(jnp.mean(o * o, axis=-1, keepdims=True) + LN_EPS)
            o_heads.append(o * gng[:, h * DV:(h + 1) * DV])
        o_chunks.append(jnp.concatenate(o_heads, axis=1))
        kt = k_dec[sl].T.astype(BF16)
        b_last = b[(c + 1) * c_len - 1:(c + 1) * c_len, :]
        dcol = jnp.exp(jnp.broadcast_to(b_last, (LANES, QK)).T)
        upd = jnp.concatenate([_dot(kt[h * DK:(h + 1) * DK], vc[:, h * DV:(h + 1) * DV])
                               for h in range(HEADS)], axis=0)
        state = dcol * (state + upd)
    st_s[...] = state

    g_a = _dot(xb, wa_ref[:, 2 * QK + BRANCH:2 * QK + 2 * BRANCH])
    o_a = jnp.concatenate(o_chunks, axis=0) * _silu(g_a)
    y = _dot(o_a.astype(BF16), wout_ref[0:BRANCH, :])

    u = _gelu_tanh(_dot(xb, wb_ref[:, 0:BRANCH]))
    sv = _gelu_tanh(_dot(xb, wb_ref[:, BRANCH:2 * BRANCH]))
    g_b = _dot(xb, wb_ref[:, 2 * BRANCH:3 * BRANCH])
    t_len = SGU_CHUNK
    trow = lax.broadcasted_iota(jnp.int32, (t_len, t_len), 0)
    tcol = lax.broadcasted_iota(jnp.int32, (t_len, t_len), 1)
    slg = slg_ref[...]
    slb = slb_ref[...]
    s_groups = []
    for g in range(BRANCH // LANES):
        cols = slice(g * LANES, (g + 1) * LANES)
        w = jnp.where(trow >= tcol, ws_ref[g], 0.0).astype(BF16)
        svn = _layer_norm_rows(sv[:, cols], slg[:, cols], slb[:, cols]).astype(BF16)
        s_chunks = [_dot(w, svn[n * t_len:(n + 1) * t_len, :]) for n in range(tb // t_len)]
        s_groups.append(jnp.concatenate(s_chunks, axis=0))
    s = jnp.concatenate(s_groups, axis=1) + jnp.concatenate([bs_ref[...]] * (tb // t_len), axis=0)
    o_b = u * s * _silu(g_b)
    y = y + _dot(o_b.astype(BF16), wout_ref[BRANCH:2 * BRANCH, :])

    o_ref[0] = _layer_norm_rows(DN_ALPHA * x + y, lng_ref[...], lnb_ref[...])


def _odd_kernel(x_ref, pos_ref, w_ref, inv_ref, dmat_ref, wint_ref, wst_ref, dec_ref, rng_ref, wp_ref,
                psc_ref, wout_ref, lng_ref, lnb_ref, o_ref, st_s, pad_s):
    tb = x_ref.shape[1]
    i = pl.program_id(1)

    @pl.when(i == 0)
    def _():
        st_s[...] = jnp.zeros_like(st_s)
        pad_s[0:POOL_PAD, :] = jnp.zeros((POOL_PAD, BRANCH), F32)

    x = x_ref[0]
    xb = x.astype(BF16)

    ang = pos_ref[0].astype(F32) * inv_ref[...]
    cos = jnp.cos(ang)
    sin = jnp.sin(ang)
    q = _dot(xb, w_ref[:, 0:QK]) * (DK ** -0.5)
    k = _dot(xb, w_ref[:, QK:2 * QK])
    q1, q2 = q[:, :LANES], q[:, LANES:]
    k1, k2 = k[:, :LANES], k[:, LANES:]
    qr = jnp.concatenate([q1 * cos - q2 * sin, q1 * sin + q2 * cos], axis=1)
    kr = jnp.concatenate([k1 * cos - k2 * sin, k1 * sin + k2 * cos], axis=1)
    kr_b = kr.astype(BF16)
    v = _dot(xb, w_ref[:, 2 * QK:2 * QK + BRANCH]).astype(BF16)

    c_len = RET_CHUNK
    masks = [_head_mask(h, False) for h in range(HEADS)]
    half = DK // 2
    rng = rng_ref[...]
    state = st_s[...]
    o_chunks = []
    for c in range(tb // c_len):
        sl = slice(c * c_len, (c + 1) * c_len)
        vc = v[sl]
        qm = jnp.concatenate([qr[sl] * masks[h] for h in range(HEADS)], axis=0).astype(BF16)
        sc = (_dot_nt(qm, kr_b[sl]) * dmat_ref[...]).astype(BF16)
        o_inter = _dot(qm, state.astype(BF16)) * wint_ref[...]
        o_heads = []
        for h in range(HEADS):
            rows = slice(h * c_len, (h + 1) * c_len)
            o = _dot(sc[rows], vc[:, h * DV:(h + 1) * DV]) + o_inter[rows]
            mu = jnp.mean(o, axis=-1, keepdims=True)
            oc = o - mu
            var = jnp.mean(oc * oc, axis=-1, keepdims=True)
            o_heads.append(oc * lax.rsqrt(var + LN_EPS) * rng[:, h * DV:(h + 1) * DV])
        o_chunks.append(jnp.concatenate(o_heads, axis=1))
        kt = (kr[sl] * wst_ref[...]).T.astype(BF16)
        upd = [_dot(jnp.concatenate([kt[h * half:(h + 1) * half],
                                     kt[LANES + h * half:LANES + (h + 1) * half]], axis=0),
                    vc[:, h * DV:(h + 1) * DV]) for h in range(HEADS)]
        upd = jnp.concatenate([u[:half] for u in upd] + [u[half:] for u in upd], axis=0)
        state = dec_ref[...] * state + upd
    st_s[...] = state

    g_c = _dot(xb, w_ref[:, 2 * QK + BRANCH:2 * QK + 2 * BRANCH])
    o_c = jnp.concatenate(o_chunks, axis=0) * _silu(g_c)
    y = _dot(o_c.astype(BF16), wout_ref[0:BRANCH, :])

    p = _dot(xb, w_ref[:, 2 * QK + 2 * BRANCH:2 * QK + 3 * BRANCH])
    g_d = _dot(xb, w_ref[:, 2 * QK + 3 * BRANCH:2 * QK + 4 * BRANCH])
    pad_s[POOL_PAD:POOL_PAD + tb, :] = p
    t_seq = i * tb + lax.broadcasted_iota(jnp.int32, (tb, LANES), 0)
    pooled = []
    for g, win in enumerate(POOL_WINDOWS):
        cols = slice(g * LANES, (g + 1) * LANES)
        acc = p[:, cols]
        for d in range(1, win):
            acc = acc + pad_s[POOL_PAD - d:POOL_PAD - d + tb, cols]
        cnt = jnp.minimum(t_seq + 1, win).astype(F32)
        pooled.append((acc / cnt - p[:, cols]).astype(BF16))
    tail = pad_s[tb:tb + POOL_PAD, :]
    pad_s[0:POOL_PAD, :] = tail
    yd = jnp.concatenate([_dot(pooled[g], wp_ref[g]) for g in range(len(POOL_WINDOWS))], axis=1)
    o_d = yd * psc_ref[...] * _silu(g_d)
    y = y + _dot(o_d.astype(BF16), wout_ref[BRANCH:2 * BRANCH, :])

    o_ref[0] = _layer_norm_rows(DN_ALPHA * x + y, lng_ref[...], lnb_ref[...])


def _full_spec(a):
    nd = a.ndim
    return pl.BlockSpec(a.shape, lambda b, i, _nd=nd: (0,) * _nd)


def _compiler_params():
    return pltpu.CompilerParams(dimension_semantics=("arbitrary", "arbitrary"),
                                vmem_limit_bytes=VMEM_LIMIT_BYTES)


def _row(a):
    return a.reshape(1, -1).astype(F32)


def _even_layer(x, w_in, w_a2, b_a, gla_norm_g, sgu_ln_g, sgu_ln_b, w_s, b_s, w_out, ln_g, ln_b):
    B, S, D = x.shape
    tb = TOKEN_BLOCK
    a_cols = 2 * QK + 2 * BRANCH
    wa = w_in[:, :a_cols].astype(BF16)
    wlr = jnp.pad(w_in[:, a_cols:a_cols + GLA_RANK], ((0, 0), (0, LANES - GLA_RANK))).astype(BF16)
    wa2 = jnp.pad(w_a2, ((0, LANES - GLA_RANK), (0, 0))).astype(BF16)
    wb = w_in[:, a_cols + GLA_RANK:].astype(BF16)
    bs = jnp.repeat(jnp.transpose(b_s), LANES, axis=1).astype(F32)
    params = (wa, wlr, wa2, _row(b_a), _row(gla_norm_g), wb, _row(sgu_ln_g), _row(sgu_ln_b),
              w_s.astype(F32), bs, w_out.astype(BF16), _row(ln_g), _row(ln_b))
    tok_spec = pl.BlockSpec((1, tb, D), lambda b, i: (b, i, 0))
    return pl.pallas_call(
        _even_kernel,
        out_shape=jax.ShapeDtypeStruct((B, S, D), F32),
        grid=(B, S // tb),
        in_specs=[tok_spec] + [_full_spec(p) for p in params],
        out_specs=tok_spec,
        scratch_shapes=[pltpu.VMEM((QK, DV), F32)],
        compiler_params=_compiler_params(),
        name="even_layer_gla_sgu",
    )(x, *params)


def _retention_tables():
    c = RET_CHUNK
    log_gamma = np.log(1.0 - 2.0 ** (-5.0 - np.arange(HEADS, dtype=np.float64)))
    idx = np.arange(c, dtype=np.float64)
    rel = idx[:, None] - idx[None, :]
    dmat = np.where(rel >= 0, np.exp(np.maximum(rel, 0.0)[None] * log_gamma[:, None, None]), 0.0)
    w_inter = np.exp((idx + 1.0)[None] * log_gamma[:, None])
    w_state = np.exp((c - 1.0 - idx)[None] * log_gamma[:, None])
    lane_head = (np.arange(QK) % LANES) // (DK // 2)
    dmat = dmat.reshape(HEADS * c, c)
    wint = np.broadcast_to(w_inter[:, :, None], (HEADS, c, DV)).reshape(HEADS * c, DV)
    wst = w_state[lane_head, :].T
    dec = np.broadcast_to(np.exp(c * log_gamma)[lane_head][:, None], (QK, DV))
    inv = ROPE_BASE ** (-np.arange(DK // 2, dtype=np.float64) / (DK // 2))
    inv = np.tile(inv, HEADS)[None, :]
    f = lambda a: jnp.asarray(np.ascontiguousarray(a), F32)
    return f(inv), f(dmat), f(wint), f(wst), f(dec)


def _odd_layer(x, positions, w_in, ret_norm_g, w_pool, pool_scale, w_out, ln_g, ln_b):
    B, S, D = x.shape
    tb = TOKEN_BLOCK
    half = DK // 2
    j = np.arange(QK)
    perm = ((j % LANES) // half) * DK + (j // LANES) * half + (j % half)
    wq = w_in[:, 0:QK][:, perm]
    wk = w_in[:, QK:2 * QK][:, perm]
    w = jnp.concatenate([wq, wk, w_in[:, 2 * QK:]], axis=1).astype(BF16)
    inv, dmat, wint, wst, dec = _retention_tables()
    params = (w, inv, dmat, wint, wst, dec, _row(ret_norm_g), w_pool.astype(BF16), _row(pool_scale),
              w_out.astype(BF16), _row(ln_g), _row(ln_b))
    tok_spec = pl.BlockSpec((1, tb, D), lambda b, i: (b, i, 0))
    pos_spec = pl.BlockSpec((1, tb, 1), lambda b, i: (b, i, 0))
    return pl.pallas_call(
        _odd_kernel,
        out_shape=jax.ShapeDtypeStruct((B, S, D), F32),
        grid=(B, S // tb),
        in_specs=[tok_spec, pos_spec] + [_full_spec(p) for p in params],
        out_specs=tok_spec,
        scratch_shapes=[
            pltpu.VMEM((QK, DV), F32),
            pltpu.VMEM((POOL_PAD + tb, BRANCH), F32),
        ],
        compiler_params=_compiler_params(),
        name="odd_layer_retention_pool",
    )(x, positions.reshape(B, S, 1), *params)


def kernel(x, positions, l0_w_in, l0_w_a2, l0_b_a, l0_gla_norm_g, l0_sgu_ln_g, l0_sgu_ln_b, l0_w_s, l0_b_s, l0_w_out, l0_ln_g, l0_ln_b, l1_w_in, l1_ret_norm_g, l1_w_pool, l1_pool_scale, l1_w_out, l1_ln_g, l1_ln_b):
    x = _even_layer(x, l0_w_in, l0_w_a2, l0_b_a, l0_gla_norm_g, l0_sgu_ln_g, l0_sgu_ln_b, l0_w_s, l0_b_s,
                    l0_w_out, l0_ln_g, l0_ln_b)
    return _odd_layer(x, positions, l1_w_in, l1_ret_norm_g, l1_w_pool, l1_pool_scale, l1_w_out, l1_ln_g, l1_ln_b)
```

```python
import jax
import jax.numpy as jnp
import numpy as np
from jax import lax
from jax.experimental import pallas as pl
from jax.experimental.pallas import tpu as pltpu

F32 = jnp.float32
BF16 = jnp.bfloat16

D_MODEL = 1024
BRANCH = 512
HEADS = 4
DK = 64
DV = 128
QK = HEADS * DK
GLA_RANK = 16
GLA_TAU = 16.0
GLA_CHUNK = 64
SGU_CHUNK = 128
RET_CHUNK = 128
ROPE_BASE = 10000.0
POOL_WINDOWS = (2, 4, 8, 16)
POOL_PAD = 16
DN_ALPHA = 4.0 ** 0.25
LN_EPS = 1e-5
LANES = 128
TOKEN_BLOCK = 512
VMEM_LIMIT_BYTES = 56 * 1024 * 1024


def _dot(a, b):
    return jnp.dot(a, b, preferred_element_type=F32)


def _dot_nt(a, b):
    return lax.dot_general(a, b, (((1,), (1,)), ((), ())), preferred_element_type=F32)


def _silu(x):
    return x / (1.0 + jnp.exp(-x))


def _gelu_tanh(x):
    c = np.sqrt(2.0 / np.pi)
    hx = 0.5 * x
    return hx + hx * jnp.tanh(x * (np.float32(c) + np.float32(c * 0.044715) * (x * x)))


def _log_sigmoid(z):
    return jnp.minimum(z, 0.0) - jnp.log(1.0 + jnp.exp(-jnp.abs(z)))


def _layer_norm_rows(r, g, b):
    mu = jnp.mean(r, axis=-1, keepdims=True)
    c = r - mu
    var = jnp.mean(c * c, axis=-1, keepdims=True)
    return c * lax.rsqrt(var + LN_EPS) * g + b


def _head_mask(h, natural):
    lane = lax.broadcasted_iota(jnp.int32, (1, QK), 1)
    head = lane // DK if natural else (lane % LANES) // (DK // 2)
    return (head == h).astype(F32)


def _even_kernel(x_ref, wa_ref, wlr_ref, wa2_ref, ba_ref, gng_ref, wb_ref, slg_ref, slb_ref, ws_ref,
                 bs_ref, wout_ref, lng_ref, lnb_ref, o_ref, st_s):
    tb = x_ref.shape[1]

    @pl.when(pl.program_id(1) == 0)
    def _():
        st_s[...] = jnp.zeros_like(st_s)

    x = x_ref[0]
    xb = x.astype(BF16)
    c_len = GLA_CHUNK
    t_len = SGU_CHUNK
    chunks = [slice(c * c_len, (c + 1) * c_len) for c in range(tb // c_len)]

    a_lr = _dot(xb, wlr_ref[...])
    z = _dot(a_lr.astype(BF16), wa2_ref[...]) + ba_ref[...]
    q = _dot(xb, wa_ref[:, 0:QK])
    k = _dot(xb, wa_ref[:, QK:2 * QK])
    la = _log_sigmoid(z) * (1.0 / GLA_TAU)
    v = _dot(xb, wa_ref[:, 2 * QK:2 * QK + BRANCH]).astype(BF16)

    row = lax.broadcasted_iota(jnp.int32, (LANES, LANES), 0)
    col = lax.broadcasted_iota(jnp.int32, (LANES, LANES), 1)
    tril2 = ((row >= col) & (row // c_len == col // c_len)).astype(BF16)
    la_hi = la.astype(BF16)
    la_lo = (la - la_hi.astype(F32)).astype(BF16)
    b = jnp.concatenate(
        [_dot(tril2, la_hi[n * LANES:(n + 1) * LANES]) + _dot(tril2, la_lo[n * LANES:(n + 1) * LANES])
         for n in range(tb // LANES)], axis=0)

    u = _dot(xb, wb_ref[:, 0:BRANCH])
    q_dec = q * jnp.exp(b) * (DK ** -0.5)
    k_dec = k * jnp.exp(-b)
    k_dec_b = k_dec.astype(BF16)
    masks = [_head_mask(h, True) for h in range(HEADS)]
    qms = [jnp.concatenate([q_dec[sl] * masks[h] for h in range(HEADS)], axis=0).astype(BF16)
           for sl in chunks]
    sv = _dot(xb, wb_ref[:, BRANCH:2 * BRANCH])

    crow = lax.broadcasted_iota(jnp.int32, (HEADS * c_len, c_len), 0) % c_len
    ccol = lax.broadcasted_iota(jnp.int32, (HEADS * c_len, c_len), 1)
    causal = crow >= ccol
    scs = [jnp.where(causal, _dot_nt(qm, k_dec_b[sl]), 0.0).astype(BF16)
           for qm, sl in zip(qms, chunks)]
    u = _gelu_tanh(u)
    upds, dcols = [], []
    for c, sl in enumerate(chunks):
        kt = k_dec[sl].T.astype(BF16)
        upds.append(jnp.concatenate([_dot(kt[h * DK:(h + 1) * DK], v[sl, h * DV:(h + 1) * DV])
                                     for h in range(HEADS)], axis=0))
        b_last = b[(c + 1) * c_len - 1:(c + 1) * c_len, :]
        dcols.append(jnp.exp(jnp.broadcast_to(b_last, (LANES, QK)).T))
    g_a = _dot(xb, wa_ref[:, 2 * QK + BRANCH:2 * QK + 2 * BRANCH])
    sv = _gelu_tanh(sv)
    slg = slg_ref[...]
    slb = slb_ref[...]
    svn = jnp.concatenate(
        [_layer_norm_rows(sv[:, g * LANES:(g + 1) * LANES], slg[:, g * LANES:(g + 1) * LANES],
                          slb[:, g * LANES:(g + 1) * LANES]) for g in range(BRANCH // LANES)],
        axis=1).astype(BF16)
    g_b = _dot(xb, wb_ref[:, 2 * BRANCH:3 * BRANCH])

    state = st_s[...]
    states = []
    for upd, dcol in zip(upds, dcols):
        states.append(state.astype(BF16))
        state = dcol * (state + upd)
    st_s[...] = state

    trow = lax.broadcasted_iota(jnp.int32, (t_len, t_len), 0)
    tcol = lax.broadcasted_iota(jnp.int32, (t_len, t_len), 1)
    s_groups = []
    for g in range(BRANCH // LANES):
        w = jnp.where(trow >= tcol, ws_ref[g], 0.0).astype(BF16)
        s_groups.append(jnp.concatenate(
            [_dot(w, svn[n * t_len:(n + 1) * t_len, g * LANES:(g + 1) * LANES])
             for n in range(tb // t_len)], axis=0))
    s = jnp.concatenate(s_groups, axis=1) + jnp.concatenate([bs_ref[...]] * (tb // t_len), axis=0)

    gng = gng_ref[...]
    o_inters = [_dot(qm, s_b) for qm, s_b in zip(qms, states)]
    o_b = u * s * _silu(g_b)
    o_chunks = []
    for sl, sc, o_inter in zip(chunks, scs, o_inters):
        o_heads = []
        for h in range(HEADS):
            rows = slice(h * c_len, (h + 1) * c_len)
            o = _dot(sc[rows], v[sl, h * DV:(h + 1) * DV]) + o_inter[rows]
            o = o * lax.rsqrt(jnp.mean(o * o, axis=-1, keepdims=True) + LN_EPS)
            o_heads.append(o * gng[:, h * DV:(h + 1) * DV])
        o_chunks.append(jnp.concatenate(o_heads, axis=1))
    y = _dot(o_b.astype(BF16), wout_ref[BRANCH:2 * BRANCH, :])
    o_a = jnp.concatenate(o_chunks, axis=0) * _silu(g_a)
    y = y + _dot(o_a.astype(BF16), wout_ref[0:BRANCH, :])

    o_ref[0] = _layer_norm_rows(DN_ALPHA * x + y, lng_ref[...], lnb_ref[...])


def _odd_kernel(run_ref, x_ref, pos_ref, w_ref, inv_ref, ctab_ref, stab_ref, dmat_ref, wint_ref, wst_ref,
                dec_ref, rng_ref, wp_ref, psc_ref, wout_ref, lng_ref, lnb_ref, o_ref, st_s, pad_s, cos_s,
                sin_s):
    tb = x_ref.shape[1]
    i = pl.program_id(1)

    @pl.when(i == 0)
    def _():
        st_s[...] = jnp.zeros_like(st_s)
        pad_s[0:POOL_PAD, :] = jnp.zeros((POOL_PAD, BRANCH), F32)

    x = x_ref[0]
    xb = x.astype(BF16)
    c_len = RET_CHUNK
    half = DK // 2
    chunks = [slice(c * c_len, (c + 1) * c_len) for c in range(tb // c_len)]
    c_q, c_k, c_v, c_gc, c_p, c_gd = [(o, o + n) for o, n in zip(
        (0, QK, 2 * QK, 2 * QK + BRANCH, 2 * QK + 2 * BRANCH, 2 * QK + 3 * BRANCH),
        (QK, QK, BRANCH, BRANCH, BRANCH, BRANCH))]

    consecutive = run_ref[pl.program_id(0), i] == 1

    @pl.when(consecutive)
    def _():
        a0 = pos_ref[0, 0:1, :].astype(F32) * inv_ref[...]
        c0 = jnp.cos(a0)
        s0 = jnp.sin(a0)
        cos_s[...] = c0 * ctab_ref[...] - s0 * stab_ref[...]
        sin_s[...] = s0 * ctab_ref[...] + c0 * stab_ref[...]

    @pl.when(jnp.logical_not(consecutive))
    def _():
        ang = pos_ref[0].astype(F32) * inv_ref[...]
        cos_s[...] = jnp.cos(ang)
        sin_s[...] = jnp.sin(ang)

    q = _dot(xb, w_ref[:, c_q[0]:c_q[1]]) * (DK ** -0.5)
    k = _dot(xb, w_ref[:, c_k[0]:c_k[1]])
    cos = cos_s[...]
    sin = sin_s[...]
    v = _dot(xb, w_ref[:, c_v[0]:c_v[1]]).astype(BF16)
    p = _dot(xb, w_ref[:, c_p[0]:c_p[1]])
    q1, q2 = q[:, :LANES], q[:, LANES:]
    k1, k2 = k[:, :LANES], k[:, LANES:]
    qr = jnp.concatenate([q1 * cos - q2 * sin, q1 * sin + q2 * cos], axis=1)
    kr = jnp.concatenate([k1 * cos - k2 * sin, k1 * sin + k2 * cos], axis=1)
    kr_b = kr.astype(BF16)
    masks = [_head_mask(h, False) for h in range(HEADS)]
    qms = [jnp.concatenate([qr[sl] * masks[h] for h in range(HEADS)], axis=0).astype(BF16)
           for sl in chunks]
    g_c = _dot(xb, w_ref[:, c_gc[0]:c_gc[1]])

    scs = [(_dot_nt(qm, kr_b[sl]) * dmat_ref[...]).astype(BF16) for qm, sl in zip(qms, chunks)]
    pad_s[POOL_PAD:POOL_PAD + tb, :] = p
    t_top = i * tb + lax.broadcasted_iota(jnp.int32, (POOL_PAD, LANES), 0)
    pooled = []
    for g, win in enumerate(POOL_WINDOWS):
        cols = slice(g * LANES, (g + 1) * LANES)
        acc = pad_s[:, cols]
        shift = 1
        while shift < win:
            acc = acc + pltpu.roll(acc, shift, axis=0)
            shift *= 2
        cnt_top = jnp.minimum(t_top + 1, win).astype(F32)
        mean = jnp.concatenate([acc[POOL_PAD:2 * POOL_PAD] / cnt_top, acc[2 * POOL_PAD:] * (1.0 / win)],
                               axis=0)
        pooled.append((mean - p[:, cols]).astype(BF16))
    tail = pad_s[tb:tb + POOL_PAD, :]
    pad_s[0:POOL_PAD, :] = tail
    upds = []
    for sl in chunks:
        kt = (kr[sl] * wst_ref[...]).T.astype(BF16)
        upd = [_dot(jnp.concatenate([kt[h * half:(h + 1) * half],
                                     kt[LANES + h * half:LANES + (h + 1) * half]], axis=0),
                    v[sl, h * DV:(h + 1) * DV]) for h in range(HEADS)]
        upds.append(jnp.concatenate([u[:half] for u in upd] + [u[half:] for u in upd], axis=0))
    g_d = _dot(xb, w_ref[:, c_gd[0]:c_gd[1]])

    state = st_s[...]
    states = []
    for upd in upds:
        states.append(state.astype(BF16))
        state = dec_ref[...] * state + upd
    st_s[...] = state
    yd = jnp.concatenate([_dot(pooled[g], wp_ref[g]) for g in range(len(POOL_WINDOWS))], axis=1)
    o_inters = [_dot(qm, s_b) * wint_ref[...] for qm, s_b in zip(qms, states)]
    o_d = yd * psc_ref[...] * _silu(g_d)
    rng = rng_ref[...]
    o_chunks = []
    for sl, sc, o_inter in zip(chunks, scs, o_inters):
        o_heads = []
        for h in range(HEADS):
            rows = slice(h * c_len, (h + 1) * c_len)
            o = _dot(sc[rows], v[sl, h * DV:(h + 1) * DV]) + o_inter[rows]
            mu = jnp.mean(o, axis=-1, keepdims=True)
            oc = o - mu
            var = jnp.mean(oc * oc, axis=-1, keepdims=True)
            o_heads.append(oc * lax.rsqrt(var + LN_EPS) * rng[:, h * DV:(h + 1) * DV])
        o_chunks.append(jnp.concatenate(o_heads, axis=1))
    y = _dot(o_d.astype(BF16), wout_ref[BRANCH:2 * BRANCH, :])
    o_c = jnp.concatenate(o_chunks, axis=0) * _silu(g_c)
    y = y + _dot(o_c.astype(BF16), wout_ref[0:BRANCH, :])

    o_ref[0] = _layer_norm_rows(DN_ALPHA * x + y, lng_ref[...], lnb_ref[...])


def _full_spec(a):
    nd = a.ndim
    return pl.BlockSpec(a.shape, lambda b, i, _nd=nd: (0,) * _nd)


def _compiler_params():
    return pltpu.CompilerParams(dimension_semantics=("arbitrary", "arbitrary"),
                                vmem_limit_bytes=VMEM_LIMIT_BYTES)


def _row(a):
    return a.reshape(1, -1).astype(F32)


def _even_layer(x, w_in, w_a2, b_a, gla_norm_g, sgu_ln_g, sgu_ln_b, w_s, b_s, w_out, ln_g, ln_b):
    B, S, D = x.shape
    tb = TOKEN_BLOCK
    a_cols = 2 * QK + 2 * BRANCH
    wa = w_in[:, :a_cols].astype(BF16)
    wlr = jnp.pad(w_in[:, a_cols:a_cols + GLA_RANK], ((0, 0), (0, LANES - GLA_RANK))).astype(BF16)
    wa2 = jnp.pad(w_a2, ((0, LANES - GLA_RANK), (0, 0))).astype(BF16)
    wb = w_in[:, a_cols + GLA_RANK:].astype(BF16)
    bs = jnp.repeat(jnp.transpose(b_s), LANES, axis=1).astype(F32)
    params = (wa, wlr, wa2, _row(b_a), _row(gla_norm_g), wb, _row(sgu_ln_g), _row(sgu_ln_b),
              w_s.astype(F32), bs, w_out.astype(BF16), _row(ln_g), _row(ln_b))
    tok_spec = pl.BlockSpec((1, tb, D), lambda b, i: (b, i, 0))
    return pl.pallas_call(
        _even_kernel,
        out_shape=jax.ShapeDtypeStruct((B, S, D), F32),
        grid=(B, S // tb),
        in_specs=[tok_spec] + [_full_spec(p) for p in params],
        out_specs=tok_spec,
        scratch_shapes=[pltpu.VMEM((QK, DV), F32)],
        compiler_params=_compiler_params(),
        name="even_layer_gla_sgu",
    )(x, *params)


def _retention_tables():
    c = RET_CHUNK
    log_gamma = np.log(1.0 - 2.0 ** (-5.0 - np.arange(HEADS, dtype=np.float64)))
    idx = np.arange(c, dtype=np.float64)
    rel = idx[:, None] - idx[None, :]
    dmat = np.where(rel >= 0, np.exp(np.maximum(rel, 0.0)[None] * log_gamma[:, None, None]), 0.0)
    w_inter = np.exp((idx + 1.0)[None] * log_gamma[:, None])
    w_state = np.exp((c - 1.0 - idx)[None] * log_gamma[:, None])
    lane_head = (np.arange(QK) % LANES) // (DK // 2)
    dmat = dmat.reshape(HEADS * c, c)
    wint = np.broadcast_to(w_inter[:, :, None], (HEADS, c, DV)).reshape(HEADS * c, DV)
    wst = w_state[lane_head, :].T
    dec = np.broadcast_to(np.exp(c * log_gamma)[lane_head][:, None], (QK, DV))
    inv = ROPE_BASE ** (-np.arange(DK // 2, dtype=np.float64) / (DK // 2))
    inv = np.tile(inv, HEADS)[None, :].astype(np.float32)
    off = np.arange(TOKEN_BLOCK, dtype=np.float64)[:, None] * inv.astype(np.float64)
    f = lambda a: jnp.asarray(np.ascontiguousarray(a), F32)
    return f(inv), f(np.cos(off)), f(np.sin(off)), f(dmat), f(wint), f(wst), f(dec)


def _odd_layer(x, positions, w_in, ret_norm_g, w_pool, pool_scale, w_out, ln_g, ln_b):
    B, S, D = x.shape
    tb = TOKEN_BLOCK
    half = DK // 2
    j = np.arange(QK)
    perm = ((j % LANES) // half) * DK + (j // LANES) * half + (j % half)
    wq = w_in[:, 0:QK][:, perm]
    wk = w_in[:, QK:2 * QK][:, perm]
    w = jnp.concatenate([wq, wk, w_in[:, 2 * QK:]], axis=1).astype(BF16)
    params = (w, *_retention_tables(), _row(ret_norm_g), w_pool.astype(BF16), _row(pool_scale),
              w_out.astype(BF16), _row(ln_g), _row(ln_b))
    pos_blocks = positions.reshape(B, S // tb, tb)
    runs = jnp.all(pos_blocks == pos_blocks[:, :, :1] + jnp.arange(tb, dtype=positions.dtype), axis=-1)
    tok_spec = pl.BlockSpec((1, tb, D), lambda b, i: (b, i, 0))
    pos_spec = pl.BlockSpec((1, tb, 1), lambda b, i: (b, i, 0))
    return pl.pallas_call(
        _odd_kernel,
        out_shape=jax.ShapeDtypeStruct((B, S, D), F32),
        grid=(B, S // tb),
        in_specs=[pl.BlockSpec(memory_space=pltpu.SMEM), tok_spec, pos_spec] + [_full_spec(p) for p in params],
        out_specs=tok_spec,
        scratch_shapes=[
            pltpu.VMEM((QK, DV), F32),
            pltpu.VMEM((POOL_PAD + tb, BRANCH), F32),
            pltpu.VMEM((tb, LANES), F32),
            pltpu.VMEM((tb, LANES), F32),
        ],
        compiler_params=_compiler_params(),
        name="odd_layer_retention_pool",
    )(runs.astype(jnp.int32), x, positions.reshape(B, S, 1), *params)


def kernel(x, positions, l0_w_in, l0_w_a2, l0_b_a, l0_gla_norm_g, l0_sgu_ln_g, l0_sgu_ln_b, l0_w_s, l0_b_s, l0_w_out, l0_ln_g, l0_ln_b, l1_w_in, l1_ret_norm_g, l1_w_pool, l1_pool_scale, l1_w_out, l1_ln_g, l1_ln_b):
    x = _even_layer(x, l0_w_in, l0_w_a2, l0_b_a, l0_gla_norm_g, l0_sgu_ln_g, l0_sgu_ln_b, l0_w_s, l0_b_s,
                    l0_w_out, l0_ln_g, l0_ln_b)
    return _odd_layer(x, positions, l1_w_in, l1_ret_norm_g, l1_w_pool, l1_pool_scale, l1_w_out, l1_ln_g, l1_ln_b)
```

```python
import jax
import jax.numpy as jnp
import numpy as np
from jax import lax
from jax.experimental import pallas as pl
from jax.experimental.pallas import tpu as pltpu

F32 = jnp.float32
BF16 = jnp.bfloat16

D_MODEL = 1024
BRANCH = 512
HEADS = 4
DK = 64
DV = 128
QK = HEADS * DK
GLA_RANK = 16
GLA_TAU = 16.0
GLA_CHUNK = 64
SGU_CHUNK = 128
RET_CHUNK = 128
ROPE_BASE = 10000.0
POOL_WINDOWS = (2, 4, 8, 16)
POOL_PAD = 16
DN_ALPHA = 4.0 ** 0.25
LN_EPS = 1e-5
LANES = 128
SUB_BLOCK = 1024
TOKEN_BLOCK = 1024
VMEM_LIMIT_BYTES = 56 * 1024 * 1024


def _dot(a, b):
    return jnp.dot(a, b, preferred_element_type=F32)


def _dot_nt(a, b):
    return lax.dot_general(a, b, (((1,), (1,)), ((), ())), preferred_element_type=F32)


def _silu(x):
    return x / (1.0 + jnp.exp(-x))


def _gelu_tanh(x):
    c = np.sqrt(2.0 / np.pi)
    hx = 0.5 * x
    return hx + hx * jnp.tanh(x * (np.float32(c) + np.float32(c * 0.044715) * (x * x)))


def _log_sigmoid(z):
    return jnp.minimum(z, 0.0) - jnp.log(1.0 + jnp.exp(-jnp.abs(z)))


def _layer_norm_rows(r, g, b):
    mu = jnp.mean(r, axis=-1, keepdims=True)
    c = r - mu
    var = jnp.mean(c * c, axis=-1, keepdims=True)
    return c * lax.rsqrt(var + LN_EPS) * g + b


def _head_mask(h, natural):
    lane = lax.broadcasted_iota(jnp.int32, (1, QK), 1)
    head = lane // DK if natural else (lane % LANES) // (DK // 2)
    return (head == h).astype(F32)


def _even_kernel(x_ref, wa_ref, wlr_ref, wa2_ref, ba_ref, gng_ref, wb_ref, slg_ref, slb_ref, ws_ref,
                 bs_ref, wout_ref, lng_ref, lnb_ref, o_ref, st_s):
    @pl.when(pl.program_id(1) == 0)
    def _():
        st_s[...] = jnp.zeros_like(st_s)

    refs = (wa_ref, wlr_ref, wa2_ref, ba_ref, gng_ref, wb_ref, slg_ref, slb_ref, ws_ref, bs_ref, wout_ref,
            lng_ref, lnb_ref)
    state = st_s[...]
    for n in range(x_ref.shape[1] // SUB_BLOCK):
        rows = slice(n * SUB_BLOCK, (n + 1) * SUB_BLOCK)
        o_ref[0, rows, :], state = _even_block(x_ref[0, rows, :], state, *refs)
    st_s[...] = state


def _even_block(x, state, wa_ref, wlr_ref, wa2_ref, ba_ref, gng_ref, wb_ref, slg_ref, slb_ref, ws_ref,
                bs_ref, wout_ref, lng_ref, lnb_ref):
    tb = x.shape[0]
    xb = x.astype(BF16)
    c_len = GLA_CHUNK
    t_len = SGU_CHUNK
    chunks = [slice(c * c_len, (c + 1) * c_len) for c in range(tb // c_len)]

    a_lr = _dot(xb, wlr_ref[...])
    z = _dot(a_lr.astype(BF16), wa2_ref[...]) + ba_ref[...]
    q = _dot(xb, wa_ref[:, 0:QK])
    k = _dot(xb, wa_ref[:, QK:2 * QK])
    la = _log_sigmoid(z) * (1.0 / GLA_TAU)
    v = _dot(xb, wa_ref[:, 2 * QK:2 * QK + BRANCH]).astype(BF16)

    row = lax.broadcasted_iota(jnp.int32, (LANES, LANES), 0)
    col = lax.broadcasted_iota(jnp.int32, (LANES, LANES), 1)
    tril2 = ((row >= col) & (row // c_len == col // c_len)).astype(BF16)
    la_hi = la.astype(BF16)
    la_lo = (la - la_hi.astype(F32)).astype(BF16)
    b = jnp.concatenate(
        [_dot(tril2, la_hi[n * LANES:(n + 1) * LANES]) + _dot(tril2, la_lo[n * LANES:(n + 1) * LANES])
         for n in range(tb // LANES)], axis=0)

    u = _dot(xb, wb_ref[:, 0:BRANCH])
    q_dec = q * jnp.exp(b) * (DK ** -0.5)
    k_dec = k * jnp.exp(-b)
    k_dec_b = k_dec.astype(BF16)
    masks = [_head_mask(h, True) for h in range(HEADS)]
    qms = [jnp.concatenate([q_dec[sl] * masks[h] for h in range(HEADS)], axis=0).astype(BF16)
           for sl in chunks]
    sv = _dot(xb, wb_ref[:, BRANCH:2 * BRANCH])

    crow = lax.broadcasted_iota(jnp.int32, (HEADS * c_len, c_len), 0) % c_len
    ccol = lax.broadcasted_iota(jnp.int32, (HEADS * c_len, c_len), 1)
    causal = crow >= ccol
    scs = [jnp.where(causal, _dot_nt(qm, k_dec_b[sl]), 0.0).astype(BF16)
           for qm, sl in zip(qms, chunks)]
    u = _gelu_tanh(u)
    upds, dcols = [], []
    for c, sl in enumerate(chunks):
        kt = k_dec[sl].T.astype(BF16)
        upds.append(jnp.concatenate([_dot(kt[h * DK:(h + 1) * DK], v[sl, h * DV:(h + 1) * DV])
                                     for h in range(HEADS)], axis=0))
        b_last = b[(c + 1) * c_len - 1:(c + 1) * c_len, :]
        dcols.append(jnp.exp(jnp.broadcast_to(b_last, (LANES, QK)).T))
    g_a = _dot(xb, wa_ref[:, 2 * QK + BRANCH:2 * QK + 2 * BRANCH])
    sv = _gelu_tanh(sv)
    slg = slg_ref[...]
    slb = slb_ref[...]
    svn = jnp.concatenate(
        [_layer_norm_rows(sv[:, g * LANES:(g + 1) * LANES], slg[:, g * LANES:(g + 1) * LANES],
                          slb[:, g * LANES:(g + 1) * LANES]) for g in range(BRANCH // LANES)],
        axis=1).astype(BF16)
    g_b = _dot(xb, wb_ref[:, 2 * BRANCH:3 * BRANCH])

    states = []
    for upd, dcol in zip(upds, dcols):
        states.append(state.astype(BF16))
        state = dcol * (state + upd)

    trow = lax.broadcasted_iota(jnp.int32, (t_len, t_len), 0)
    tcol = lax.broadcasted_iota(jnp.int32, (t_len, t_len), 1)
    s_groups = []
    for g in range(BRANCH // LANES):
        w = jnp.where(trow >= tcol, ws_ref[g], 0.0).astype(BF16)
        s_groups.append(jnp.concatenate(
            [_dot(w, svn[n * t_len:(n + 1) * t_len, g * LANES:(g + 1) * LANES])
             for n in range(tb // t_len)], axis=0))
    s = jnp.concatenate(s_groups, axis=1) + jnp.concatenate([bs_ref[...]] * (tb // t_len), axis=0)

    gng = gng_ref[...]
    o_inters = [_dot(qm, s_b) for qm, s_b in zip(qms, states)]
    o_b = u * s * _silu(g_b)
    o_chunks = []
    for sl, sc, o_inter in zip(chunks, scs, o_inters):
        o_heads = []
        for h in range(HEADS):
            rows = slice(h * c_len, (h + 1) * c_len)
            o = _dot(sc[rows], v[sl, h * DV:(h + 1) * DV]) + o_inter[rows]
            o = o * lax.rsqrt(jnp.mean(o * o, axis=-1, keepdims=True) + LN_EPS)
            o_heads.append(o * gng[:, h * DV:(h + 1) * DV])
        o_chunks.append(jnp.concatenate(o_heads, axis=1))
    y = _dot(o_b.astype(BF16), wout_ref[BRANCH:2 * BRANCH, :])
    o_a = jnp.concatenate(o_chunks, axis=0) * _silu(g_a)
    y = y + _dot(o_a.astype(BF16), wout_ref[0:BRANCH, :])

    return _layer_norm_rows(DN_ALPHA * x + y, lng_ref[...], lnb_ref[...]), state


def _odd_kernel(run_ref, x_ref, pos_ref, w_ref, inv_ref, ctab_ref, stab_ref, dmat_ref, wint_ref, wst_ref,
                dec_ref, rng_ref, wp_ref, psc_ref, wout_ref, lng_ref, lnb_ref, o_ref, st_s, pad_s, cos_s,
                sin_s):
    tb = x_ref.shape[1]
    i = pl.program_id(1)

    @pl.when(i == 0)
    def _():
        st_s[...] = jnp.zeros_like(st_s)
        pad_s[0:POOL_PAD, :] = jnp.zeros((POOL_PAD, BRANCH), F32)

    x = x_ref[0]
    xb = x.astype(BF16)
    c_len = RET_CHUNK
    half = DK // 2
    chunks = [slice(c * c_len, (c + 1) * c_len) for c in range(tb // c_len)]
    c_q, c_k, c_v, c_gc, c_p, c_gd = [(o, o + n) for o, n in zip(
        (0, QK, 2 * QK, 2 * QK + BRANCH, 2 * QK + 2 * BRANCH, 2 * QK + 3 * BRANCH),
        (QK, QK, BRANCH, BRANCH, BRANCH, BRANCH))]

    consecutive = run_ref[pl.program_id(0), i] == 1

    @pl.when(consecutive)
    def _():
        a0 = pos_ref[0, 0:1, :].astype(F32) * inv_ref[...]
        c0 = jnp.cos(a0)
        s0 = jnp.sin(a0)
        cos_s[...] = c0 * ctab_ref[...] - s0 * stab_ref[...]
        sin_s[...] = s0 * ctab_ref[...] + c0 * stab_ref[...]

    @pl.when(jnp.logical_not(consecutive))
    def _():
        ang = pos_ref[0].astype(F32) * inv_ref[...]
        cos_s[...] = jnp.cos(ang)
        sin_s[...] = jnp.sin(ang)

    q = _dot(xb, w_ref[:, c_q[0]:c_q[1]]) * (DK ** -0.5)
    k = _dot(xb, w_ref[:, c_k[0]:c_k[1]])
    cos = cos_s[...]
    sin = sin_s[...]
    v = _dot(xb, w_ref[:, c_v[0]:c_v[1]]).astype(BF16)
    p = _dot(xb, w_ref[:, c_p[0]:c_p[1]])
    q1, q2 = q[:, :LANES], q[:, LANES:]
    k1, k2 = k[:, :LANES], k[:, LANES:]
    qr = jnp.concatenate([q1 * cos - q2 * sin, q1 * sin + q2 * cos], axis=1)
    kr = jnp.concatenate([k1 * cos - k2 * sin, k1 * sin + k2 * cos], axis=1)
    kr_b = kr.astype(BF16)
    masks = [_head_mask(h, False) for h in range(HEADS)]
    qms = [jnp.concatenate([qr[sl] * masks[h] for h in range(HEADS)], axis=0).astype(BF16)
           for sl in chunks]
    g_c = _dot(xb, w_ref[:, c_gc[0]:c_gc[1]])

    scs = [(_dot_nt(qm, kr_b[sl]) * dmat_ref[...]).astype(BF16) for qm, sl in zip(qms, chunks)]
    pad_s[POOL_PAD:POOL_PAD + tb, :] = p
    t_top = i * tb + lax.broadcasted_iota(jnp.int32, (POOL_PAD, LANES), 0)
    pooled = []
    for g, win in enumerate(POOL_WINDOWS):
        cols = slice(g * LANES, (g + 1) * LANES)
        acc = pad_s[:, cols]
        shift = 1
        while shift < win:
            acc = acc + pltpu.roll(acc, shift, axis=0)
            shift *= 2
        cnt_top = jnp.minimum(t_top + 1, win).astype(F32)
        mean = jnp.concatenate([acc[POOL_PAD:2 * POOL_PAD] / cnt_top, acc[2 * POOL_PAD:] * (1.0 / win)],
                               axis=0)
        pooled.append((mean - p[:, cols]).astype(BF16))
    tail = pad_s[tb:tb + POOL_PAD, :]
    pad_s[0:POOL_PAD, :] = tail
    upds = []
    for sl in chunks:
        kt = (kr[sl] * wst_ref[...]).T.astype(BF16)
        upd = [_dot(jnp.concatenate([kt[h * half:(h + 1) * half],
                                     kt[LANES + h * half:LANES + (h + 1) * half]], axis=0),
                    v[sl, h * DV:(h + 1) * DV]) for h in range(HEADS)]
        upds.append(jnp.concatenate([u[:half] for u in upd] + [u[half:] for u in upd], axis=0))
    g_d = _dot(xb, w_ref[:, c_gd[0]:c_gd[1]])

    state = st_s[...]
    states = []
    for upd in upds:
        states.append(state.astype(BF16))
        state = dec_ref[...] * state + upd
    st_s[...] = state
    yd = jnp.concatenate([_dot(pooled[g], wp_ref[g]) for g in range(len(POOL_WINDOWS))], axis=1)
    o_inters = [_dot(qm, s_b) * wint_ref[...] for qm, s_b in zip(qms, states)]
    o_d = yd * psc_ref[...] * _silu(g_d)
    rng = rng_ref[...]
    o_chunks = []
    for sl, sc, o_inter in zip(chunks, scs, o_inters):
        o_heads = []
        for h in range(HEADS):
            rows = slice(h * c_len, (h + 1) * c_len)
            o = _dot(sc[rows], v[sl, h * DV:(h + 1) * DV]) + o_inter[rows]
            mu = jnp.mean(o, axis=-1, keepdims=True)
            oc = o - mu
            var = jnp.mean(oc * oc, axis=-1, keepdims=True)
            o_heads.append(oc * lax.rsqrt(var + LN_EPS) * rng[:, h * DV:(h + 1) * DV])
        o_chunks.append(jnp.concatenate(o_heads, axis=1))
    y = _dot(o_d.astype(BF16), wout_ref[BRANCH:2 * BRANCH, :])
    o_c = jnp.concatenate(o_chunks, axis=0) * _silu(g_c)
    y = y + _dot(o_c.astype(BF16), wout_ref[0:BRANCH, :])

    o_ref[0] = _layer_norm_rows(DN_ALPHA * x + y, lng_ref[...], lnb_ref[...])


def _full_spec(a):
    nd = a.ndim
    return pl.BlockSpec(a.shape, lambda b, i, _nd=nd: (0,) * _nd)


def _compiler_params():
    return pltpu.CompilerParams(dimension_semantics=("arbitrary", "arbitrary"),
                                vmem_limit_bytes=VMEM_LIMIT_BYTES)


def _row(a):
    return a.reshape(1, -1).astype(F32)


def _even_layer(x, w_in, w_a2, b_a, gla_norm_g, sgu_ln_g, sgu_ln_b, w_s, b_s, w_out, ln_g, ln_b):
    B, S, D = x.shape
    tb = TOKEN_BLOCK
    a_cols = 2 * QK + 2 * BRANCH
    wa = w_in[:, :a_cols].astype(BF16)
    wlr = jnp.pad(w_in[:, a_cols:a_cols + GLA_RANK], ((0, 0), (0, LANES - GLA_RANK))).astype(BF16)
    wa2 = jnp.pad(w_a2, ((0, LANES - GLA_RANK), (0, 0))).astype(BF16)
    wb = w_in[:, a_cols + GLA_RANK:].astype(BF16)
    bs = jnp.repeat(jnp.transpose(b_s), LANES, axis=1).astype(F32)
    params = (wa, wlr, wa2, _row(b_a), _row(gla_norm_g), wb, _row(sgu_ln_g), _row(sgu_ln_b),
              w_s.astype(F32), bs, w_out.astype(BF16), _row(ln_g), _row(ln_b))
    tok_spec = pl.BlockSpec((1, tb, D), lambda b, i: (b, i, 0))
    return pl.pallas_call(
        _even_kernel,
        out_shape=jax.ShapeDtypeStruct((B, S, D), F32),
        grid=(B, S // tb),
        in_specs=[tok_spec] + [_full_spec(p) for p in params],
        out_specs=tok_spec,
        scratch_shapes=[pltpu.VMEM((QK, DV), F32)],
        compiler_params=_compiler_params(),
        name="even_layer_gla_sgu",
    )(x, *params)


def _retention_tables():
    c = RET_CHUNK
    log_gamma = np.log(1.0 - 2.0 ** (-5.0 - np.arange(HEADS, dtype=np.float64)))
    idx = np.arange(c, dtype=np.float64)
    rel = idx[:, None] - idx[None, :]
    dmat = np.where(rel >= 0, np.exp(np.maximum(rel, 0.0)[None] * log_gamma[:, None, None]), 0.0)
    w_inter = np.exp((idx + 1.0)[None] * log_gamma[:, None])
    w_state = np.exp((c - 1.0 - idx)[None] * log_gamma[:, None])
    lane_head = (np.arange(QK) % LANES) // (DK // 2)
    dmat = dmat.reshape(HEADS * c, c)
    wint = np.broadcast_to(w_inter[:, :, None], (HEADS, c, DV)).reshape(HEADS * c, DV)
    wst = w_state[lane_head, :].T
    dec = np.broadcast_to(np.exp(c * log_gamma)[lane_head][:, None], (QK, DV))
    inv = ROPE_BASE ** (-np.arange(DK // 2, dtype=np.float64) / (DK // 2))
    inv = np.tile(inv, HEADS)[None, :].astype(np.float32)
    off = np.arange(TOKEN_BLOCK, dtype=np.float64)[:, None] * inv.astype(np.float64)
    f = lambda a: jnp.asarray(np.ascontiguousarray(a), F32)
    return f(inv), f(np.cos(off)), f(np.sin(off)), f(dmat), f(wint), f(wst), f(dec)


def _odd_layer(x, positions, w_in, ret_norm_g, w_pool, pool_scale, w_out, ln_g, ln_b):
    B, S, D = x.shape
    tb = TOKEN_BLOCK
    half = DK // 2
    j = np.arange(QK)
    perm = ((j % LANES) // half) * DK + (j // LANES) * half + (j % half)
    wq = w_in[:, 0:QK][:, perm]
    wk = w_in[:, QK:2 * QK][:, perm]
    w = jnp.concatenate([wq, wk, w_in[:, 2 * QK:]], axis=1).astype(BF16)
    params = (w, *_retention_tables(), _row(ret_norm_g), w_pool.astype(BF16), _row(pool_scale),
              w_out.astype(BF16), _row(ln_g), _row(ln_b))
    pos_blocks = positions.reshape(B, S // tb, tb)
    runs = jnp.all(pos_blocks == pos_blocks[:, :, :1] + jnp.arange(tb, dtype=positions.dtype), axis=-1)
    tok_spec = pl.BlockSpec((1, tb, D), lambda b, i: (b, i, 0))
    pos_spec = pl.BlockSpec((1, tb, 1), lambda b, i: (b, i, 0))
    return pl.pallas_call(
        _odd_kernel,
        out_shape=jax.ShapeDtypeStruct((B, S, D), F32),
        grid=(B, S // tb),
        in_specs=[pl.BlockSpec(memory_space=pltpu.SMEM), tok_spec, pos_spec] + [_full_spec(p) for p in params],
        out_specs=tok_spec,
        scratch_shapes=[
            pltpu.VMEM((QK, DV), F32),
            pltpu.VMEM((POOL_PAD + tb, BRANCH), F32),
            pltpu.VMEM((tb, LANES), F32),
            pltpu.VMEM((tb, LANES), F32),
        ],
        compiler_params=_compiler_params(),
        name="odd_layer_retention_pool",
    )(runs.astype(jnp.int32), x, positions.reshape(B, S, 1), *params)


def kernel(x, positions, l0_w_in, l0_w_a2, l0_b_a, l0_gla_norm_g, l0_sgu_ln_g, l0_sgu_ln_b, l0_w_s, l0_b_s, l0_w_out, l0_ln_g, l0_ln_b, l1_w_in, l1_ret_norm_g, l1_w_pool, l1_pool_scale, l1_w_out, l1_ln_g, l1_ln_b):
    x = _even_layer(x, l0_w_in, l0_w_a2, l0_b_a, l0_gla_norm_g, l0_sgu_ln_g, l0_sgu_ln_b, l0_w_s, l0_b_s,
                    l0_w_out, l0_ln_g, l0_ln_b)
    return _odd_layer(x, positions, l1_w_in, l1_ret_norm_g, l1_w_pool, l1_pool_scale, l1_w_out, l1_ln_g, l1_ln_b)
```

```python
import functools

import jax
import jax.numpy as jnp
import numpy as np
from jax import lax
from jax.experimental import pallas as pl
from jax.experimental.pallas import tpu as pltpu

F32 = jnp.float32
BF16 = jnp.bfloat16

D_MODEL = 1024
BRANCH = 512
HEADS = 4
DK = 64
DV = 128
QK = HEADS * DK
GLA_RANK = 16
GLA_TAU = 16.0
GLA_CHUNK = 64
SGU_CHUNK = 128
RET_CHUNK = 128
ROPE_BASE = 10000.0
POOL_WINDOWS = (2, 4, 8, 16)
POOL_PAD = 16
DN_ALPHA = 4.0 ** 0.25
LN_EPS = 1e-5
LANES = 128
TOKEN_BLOCK = 1024
VMEM_LIMIT_BYTES = 56 * 1024 * 1024


def _dot(a, b):
    return jnp.dot(a, b, preferred_element_type=F32)


def _dot_nt(a, b):
    return lax.dot_general(a, b, (((1,), (1,)), ((), ())), preferred_element_type=F32)


def _silu(x):
    return x / (1.0 + jnp.exp(-x))


def _gelu_tanh(x):
    c = np.sqrt(2.0 / np.pi)
    hx = 0.5 * x
    return hx + hx * jnp.tanh(x * (np.float32(c) + np.float32(c * 0.044715) * (x * x)))


def _log_sigmoid(z):
    return jnp.minimum(z, 0.0) - jnp.log(1.0 + jnp.exp(-jnp.abs(z)))


def _layer_norm_rows(r, g, b):
    mu = jnp.mean(r, axis=-1, keepdims=True)
    c = r - mu
    var = jnp.mean(c * c, axis=-1, keepdims=True)
    return c * lax.rsqrt(var + LN_EPS) * g + b


def _head_mask(h, natural):
    lane = lax.broadcasted_iota(jnp.int32, (1, QK), 1)
    head = lane // DK if natural else (lane % LANES) // (DK // 2)
    return (head == h).astype(F32)


def _even_kernel(x_ref, wa_ref, wlr_ref, wa2_ref, ba_ref, gng_ref, wb_ref, slg_ref, slb_ref, ws_ref,
                 bs_ref, wout_ref, lng_ref, lnb_ref, o_ref, st_s, wz_s):
    @pl.when((pl.program_id(0) == 0) & (pl.program_id(1) == 0))
    def _():
        wz_s[...] = _dot(wlr_ref[...], wa2_ref[...]).astype(BF16)

    @pl.when(pl.program_id(1) == 0)
    def _():
        st_s[...] = jnp.zeros_like(st_s)

    refs = (wa_ref, wz_s, ba_ref, gng_ref, wb_ref, slg_ref, slb_ref, ws_ref, bs_ref, wout_ref, lng_ref, lnb_ref)
    o_ref[0], st_s[...] = _even_block(x_ref[0], st_s[...], *refs)


def _even_block(x, state, wa_ref, wz_ref, ba_ref, gng_ref, wb_ref, slg_ref, slb_ref, ws_ref, bs_ref, wout_ref,
                lng_ref, lnb_ref):
    tb = x.shape[0]
    xb = x.astype(BF16)
    c_len = GLA_CHUNK
    t_len = SGU_CHUNK
    chunks = [slice(c * c_len, (c + 1) * c_len) for c in range(tb // c_len)]

    z = _dot(xb, wz_ref[...]) + ba_ref[...]
    q = _dot(xb, wa_ref[:, 0:QK])
    k = _dot(xb, wa_ref[:, QK:2 * QK])
    la = _log_sigmoid(z) * (1.0 / GLA_TAU)
    v = _dot(xb, wa_ref[:, 2 * QK:2 * QK + BRANCH]).astype(BF16)

    row = lax.broadcasted_iota(jnp.int32, (LANES, LANES), 0)
    col = lax.broadcasted_iota(jnp.int32, (LANES, LANES), 1)
    tril2 = ((row >= col) & (row // c_len == col // c_len)).astype(BF16)
    la_hi = la.astype(BF16)
    la_lo = (la - la_hi.astype(F32)).astype(BF16)
    b = jnp.concatenate(
        [_dot(tril2, la_hi[n * LANES:(n + 1) * LANES]) + _dot(tril2, la_lo[n * LANES:(n + 1) * LANES])
         for n in range(tb // LANES)], axis=0)

    u = _dot(xb, wb_ref[:, 0:BRANCH])
    q_dec = q * jnp.exp(b) * (DK ** -0.5)
    k_dec = k * jnp.exp(-b)
    k_dec_b = k_dec.astype(BF16)
    masks = [_head_mask(h, True) for h in range(HEADS)]
    qms = [jnp.concatenate([q_dec[sl] * masks[h] for h in range(HEADS)], axis=0).astype(BF16)
           for sl in chunks]
    sv = _dot(xb, wb_ref[:, BRANCH:2 * BRANCH])

    crow = lax.broadcasted_iota(jnp.int32, (HEADS * c_len, c_len), 0) % c_len
    ccol = lax.broadcasted_iota(jnp.int32, (HEADS * c_len, c_len), 1)
    causal = crow >= ccol
    scs = [jnp.where(causal, _dot_nt(qm, k_dec_b[sl]), 0.0).astype(BF16)
           for qm, sl in zip(qms, chunks)]
    u = _gelu_tanh(u)
    upds, dcols = [], []
    for c, sl in enumerate(chunks):
        kt = k_dec[sl].T.astype(BF16)
        upds.append(jnp.concatenate([_dot(kt[h * DK:(h + 1) * DK], v[sl, h * DV:(h + 1) * DV])
                                     for h in range(HEADS)], axis=0))
        b_last = b[(c + 1) * c_len - 1:(c + 1) * c_len, :]
        dcols.append(jnp.exp(jnp.broadcast_to(b_last, (LANES, QK)).T))
    g_a = _dot(xb, wa_ref[:, 2 * QK + BRANCH:2 * QK + 2 * BRANCH])
    sv = _gelu_tanh(sv)
    slg = slg_ref[...]
    slb = slb_ref[...]
    svn = jnp.concatenate(
        [_layer_norm_rows(sv[:, g * LANES:(g + 1) * LANES], slg[:, g * LANES:(g + 1) * LANES],
                          slb[:, g * LANES:(g + 1) * LANES]) for g in range(BRANCH // LANES)],
        axis=1).astype(BF16)
    g_b = _dot(xb, wb_ref[:, 2 * BRANCH:3 * BRANCH])

    states = []
    for upd, dcol in zip(upds, dcols):
        states.append(state.astype(BF16))
        state = dcol * (state + upd)

    trow = lax.broadcasted_iota(jnp.int32, (t_len, t_len), 0)
    tcol = lax.broadcasted_iota(jnp.int32, (t_len, t_len), 1)
    s_groups = []
    for g in range(BRANCH // LANES):
        w = jnp.where(trow >= tcol, ws_ref[g], 0.0).astype(BF16)
        s_groups.append(jnp.concatenate(
            [_dot(w, svn[n * t_len:(n + 1) * t_len, g * LANES:(g + 1) * LANES])
             for n in range(tb // t_len)], axis=0))
    s = jnp.concatenate(s_groups, axis=1) + jnp.concatenate([bs_ref[...]] * (tb // t_len), axis=0)

    gng = gng_ref[...]
    o_inters = [_dot(qm, s_b) for qm, s_b in zip(qms, states)]
    o_b = u * s * _silu(g_b)
    o_chunks = []
    for sl, sc, o_inter in zip(chunks, scs, o_inters):
        o_heads = []
        for h in range(HEADS):
            rows = slice(h * c_len, (h + 1) * c_len)
            o = _dot(sc[rows], v[sl, h * DV:(h + 1) * DV]) + o_inter[rows]
            o = o * lax.rsqrt(jnp.mean(o * o, axis=-1, keepdims=True) + LN_EPS)
            o_heads.append(o * gng[:, h * DV:(h + 1) * DV])
        o_chunks.append(jnp.concatenate(o_heads, axis=1))
    y = _dot(o_b.astype(BF16), wout_ref[BRANCH:2 * BRANCH, :])
    o_a = jnp.concatenate(o_chunks, axis=0) * _silu(g_a)
    y = y + _dot(o_a.astype(BF16), wout_ref[0:BRANCH, :])

    return _layer_norm_rows(DN_ALPHA * x + y, lng_ref[...], lnb_ref[...]), state


def _odd_kernel(run_ref, x_ref, pos_ref, w_ref, inv_ref, ctab_ref, stab_ref, dmat_ref, wint_ref, wst_ref,
                dec_ref, rng_ref, wp_ref, psc_ref, wout_ref, lng_ref, lnb_ref, o_ref, st_s, pad_s, cos_s,
                sin_s):
    tb = x_ref.shape[1]
    seq = pl.program_id(0)
    i = pl.program_id(1)

    @pl.when(i == 0)
    def _():
        st_s[...] = jnp.zeros_like(st_s)
        pad_s[0:POOL_PAD, :] = jnp.zeros((POOL_PAD, BRANCH), F32)

    c_len = RET_CHUNK
    half = DK // 2
    chunks = [slice(c * c_len, (c + 1) * c_len) for c in range(tb // c_len)]
    c_q, c_k, c_v, c_gc, c_p, c_gd = [(o, o + n) for o, n in zip(
        (0, QK, 2 * QK, 2 * QK + BRANCH, 2 * QK + 2 * BRANCH, 2 * QK + 3 * BRANCH),
        (QK, QK, BRANCH, BRANCH, BRANCH, BRANCH))]

    consecutive = run_ref[seq, i] == 1

    @pl.when(consecutive)
    def _():
        a0 = pos_ref[0, 0:1, :].astype(F32) * inv_ref[...]
        c0 = jnp.cos(a0)
        s0 = jnp.sin(a0)
        cos_s[...] = c0 * ctab_ref[...] - s0 * stab_ref[...]
        sin_s[...] = s0 * ctab_ref[...] + c0 * stab_ref[...]

    @pl.when(jnp.logical_not(consecutive))
    def _():
        ang = pos_ref[0].astype(F32) * inv_ref[...]
        cos_s[...] = jnp.cos(ang)
        sin_s[...] = jnp.sin(ang)

    x = x_ref[0]
    xb = x.astype(BF16)

    q = _dot(xb, w_ref[:, c_q[0]:c_q[1]]) * (DK ** -0.5)
    k = _dot(xb, w_ref[:, c_k[0]:c_k[1]])
    cos = cos_s[...]
    sin = sin_s[...]
    v = _dot(xb, w_ref[:, c_v[0]:c_v[1]]).astype(BF16)
    p = _dot(xb, w_ref[:, c_p[0]:c_p[1]])
    q1, q2 = q[:, :LANES], q[:, LANES:]
    k1, k2 = k[:, :LANES], k[:, LANES:]
    qr = jnp.concatenate([q1 * cos - q2 * sin, q1 * sin + q2 * cos], axis=1)
    kr = jnp.concatenate([k1 * cos - k2 * sin, k1 * sin + k2 * cos], axis=1)
    kr_b = kr.astype(BF16)
    masks = [_head_mask(h, False) for h in range(HEADS)]
    qms = [jnp.concatenate([qr[sl] * masks[h] for h in range(HEADS)], axis=0).astype(BF16)
           for sl in chunks]
    g_c = _dot(xb, w_ref[:, c_gc[0]:c_gc[1]])

    scs = [(_dot_nt(qm, kr_b[sl]) * dmat_ref[...]).astype(BF16) for qm, sl in zip(qms, chunks)]
    pad_s[POOL_PAD:POOL_PAD + tb, :] = p
    t_top = i * tb + lax.broadcasted_iota(jnp.int32, (POOL_PAD, LANES), 0)
    pooled = []
    for g, win in enumerate(POOL_WINDOWS):
        cols = slice(g * LANES, (g + 1) * LANES)
        acc = pad_s[:, cols]
        shift = 1
        while shift < win:
            acc = acc + pltpu.roll(acc, shift, axis=0)
            shift *= 2
        cnt_top = jnp.minimum(t_top + 1, win).astype(F32)
        mean = jnp.concatenate([acc[POOL_PAD:2 * POOL_PAD] / cnt_top, acc[2 * POOL_PAD:] * (1.0 / win)],
                               axis=0)
        pooled.append((mean - p[:, cols]).astype(BF16))
    tail = pad_s[tb:tb + POOL_PAD, :]
    pad_s[0:POOL_PAD, :] = tail
    upds = []
    for sl in chunks:
        kt = (kr[sl] * wst_ref[...]).T.astype(BF16)
        upd = [_dot(jnp.concatenate([kt[h * half:(h + 1) * half],
                                     kt[LANES + h * half:LANES + (h + 1) * half]], axis=0),
                    v[sl, h * DV:(h + 1) * DV]) for h in range(HEADS)]
        upds.append(jnp.concatenate([u[:half] for u in upd] + [u[half:] for u in upd], axis=0))
    g_d = _dot(xb, w_ref[:, c_gd[0]:c_gd[1]])

    state = st_s[...]
    states = []
    for upd in upds:
        states.append(state.astype(BF16))
        state = dec_ref[...] * state + upd
    st_s[...] = state
    yd = jnp.concatenate([_dot(pooled[g], wp_ref[g]) for g in range(len(POOL_WINDOWS))], axis=1)
    o_inters = [_dot(qm, s_b) * wint_ref[...] for qm, s_b in zip(qms, states)]
    o_d = yd * psc_ref[...] * _silu(g_d)
    rng = rng_ref[...]
    o_chunks = []
    for sl, sc, o_inter in zip(chunks, scs, o_inters):
        o_heads = []
        for h in range(HEADS):
            rows = slice(h * c_len, (h + 1) * c_len)
            o = _dot(sc[rows], v[sl, h * DV:(h + 1) * DV]) + o_inter[rows]
            mu = jnp.mean(o, axis=-1, keepdims=True)
            oc = o - mu
            var = jnp.mean(oc * oc, axis=-1, keepdims=True)
            o_heads.append(oc * lax.rsqrt(var + LN_EPS) * rng[:, h * DV:(h + 1) * DV])
        o_chunks.append(jnp.concatenate(o_heads, axis=1))
    y = _dot(o_d.astype(BF16), wout_ref[BRANCH:2 * BRANCH, :])
    o_c = jnp.concatenate(o_chunks, axis=0) * _silu(g_c)
    y = y + _dot(o_c.astype(BF16), wout_ref[0:BRANCH, :])

    o_ref[0] = _layer_norm_rows(DN_ALPHA * x + y, lng_ref[...], lnb_ref[...])


def _full_spec(a):
    nd = a.ndim
    return pl.BlockSpec(a.shape, lambda *_, _nd=nd: (0,) * _nd)


def _compiler_params():
    return pltpu.CompilerParams(dimension_semantics=("arbitrary", "arbitrary"),
                                vmem_limit_bytes=VMEM_LIMIT_BYTES)


def _row(a):
    return a.reshape(1, -1).astype(F32)


def _even_layer(x, w_in, w_a2, b_a, gla_norm_g, sgu_ln_g, sgu_ln_b, w_s, b_s, w_out, ln_g, ln_b):
    B, S, D = x.shape
    tb = TOKEN_BLOCK
    a_cols = 2 * QK + 2 * BRANCH
    wa = w_in[:, :a_cols].astype(BF16)
    wlr = jnp.pad(w_in[:, a_cols:a_cols + GLA_RANK], ((0, 0), (0, LANES - GLA_RANK))).astype(BF16)
    wa2 = jnp.pad(w_a2, ((0, LANES - GLA_RANK), (0, 0))).astype(BF16)
    wb = w_in[:, a_cols + GLA_RANK:].astype(BF16)
    bs = jnp.repeat(jnp.transpose(b_s), LANES, axis=1).astype(F32)
    params = (wa, wlr, wa2, _row(b_a), _row(gla_norm_g), wb, _row(sgu_ln_g), _row(sgu_ln_b),
              w_s.astype(F32), bs, w_out.astype(BF16), _row(ln_g), _row(ln_b))
    tok_spec = pl.BlockSpec((1, tb, D), lambda b, i: (b, i, 0))
    return pl.pallas_call(
        _even_kernel,
        out_shape=jax.ShapeDtypeStruct((B, S, D), F32),
        grid=(B, S // tb),
        in_specs=[tok_spec] + [_full_spec(p) for p in params],
        out_specs=tok_spec,
        scratch_shapes=[
            pltpu.VMEM((QK, DV), F32),
            pltpu.VMEM((D, QK), BF16),
        ],
        compiler_params=_compiler_params(),
        name="even_layer_gla_sgu",
    )(x, *params)


def _retention_tables():
    c = RET_CHUNK
    log_gamma = np.log(1.0 - 2.0 ** (-5.0 - np.arange(HEADS, dtype=np.float64)))
    idx = np.arange(c, dtype=np.float64)
    rel = idx[:, None] - idx[None, :]
    dmat = np.where(rel >= 0, np.exp(np.maximum(rel, 0.0)[None] * log_gamma[:, None, None]), 0.0)
    w_inter = np.exp((idx + 1.0)[None] * log_gamma[:, None])
    w_state = np.exp((c - 1.0 - idx)[None] * log_gamma[:, None])
    lane_head = (np.arange(QK) % LANES) // (DK // 2)
    dmat = dmat.reshape(HEADS * c, c)
    wint = np.broadcast_to(w_inter[:, :, None], (HEADS, c, DV)).reshape(HEADS * c, DV)
    wst = w_state[lane_head, :].T
    dec = np.broadcast_to(np.exp(c * log_gamma)[lane_head][:, None], (QK, DV))
    inv = ROPE_BASE ** (-np.arange(DK // 2, dtype=np.float64) / (DK // 2))
    inv = np.tile(inv, HEADS)[None, :].astype(np.float32)
    off = np.arange(TOKEN_BLOCK, dtype=np.float64)[:, None] * inv.astype(np.float64)
    f = lambda a: jnp.asarray(np.ascontiguousarray(a), F32)
    return f(inv), f(np.cos(off)), f(np.sin(off)), f(dmat), f(wint), f(wst), f(dec)


def _odd_layer(x, positions, w_in, ret_norm_g, w_pool, pool_scale, w_out, ln_g, ln_b):
    B, S, D = x.shape
    tb = TOKEN_BLOCK
    half = DK // 2
    j = np.arange(QK)
    perm = ((j % LANES) // half) * DK + (j // LANES) * half + (j % half)
    wq = w_in[:, 0:QK][:, perm]
    wk = w_in[:, QK:2 * QK][:, perm]
    w = jnp.concatenate([wq, wk, w_in[:, 2 * QK:]], axis=1).astype(BF16)
    params = (w, *_retention_tables(), _row(ret_norm_g), w_pool.astype(BF16), _row(pool_scale),
              w_out.astype(BF16), _row(ln_g), _row(ln_b))
    pos_blocks = positions.reshape(B, S // tb, tb)
    runs = jnp.all(pos_blocks == pos_blocks[:, :, :1] + jnp.arange(tb, dtype=positions.dtype), axis=-1)
    tok_spec = pl.BlockSpec((1, tb, D), lambda b, i: (b, i, 0))
    pos_spec = pl.BlockSpec((1, tb, 1), lambda b, i: (b, i, 0))
    return pl.pallas_call(
        _odd_kernel,
        out_shape=jax.ShapeDtypeStruct((B, S, D), F32),
        grid=(B, S // tb),
        in_specs=[pl.BlockSpec(memory_space=pltpu.SMEM), tok_spec, pos_spec] + [_full_spec(p) for p in params],
        out_specs=tok_spec,
        scratch_shapes=[
            pltpu.VMEM((QK, DV), F32),
            pltpu.VMEM((POOL_PAD + tb, BRANCH), F32),
            pltpu.VMEM((tb, LANES), F32),
            pltpu.VMEM((tb, LANES), F32),
        ],
        compiler_params=_compiler_params(),
        name="odd_layer_retention_pool",
    )(runs.astype(jnp.int32), x, positions.reshape(B, S, 1), *params)


def kernel(x, positions, l0_w_in, l0_w_a2, l0_b_a, l0_gla_norm_g, l0_sgu_ln_g, l0_sgu_ln_b, l0_w_s, l0_b_s, l0_w_out, l0_ln_g, l0_ln_b, l1_w_in, l1_ret_norm_g, l1_w_pool, l1_pool_scale, l1_w_out, l1_ln_g, l1_ln_b):
    x = _even_layer(x, l0_w_in, l0_w_a2, l0_b_a, l0_gla_norm_g, l0_sgu_ln_g, l0_sgu_ln_b, l0_w_s, l0_b_s,
                    l0_w_out, l0_ln_g, l0_ln_b)
    return _odd_layer(x, positions, l1_w_in, l1_ret_norm_g, l1_w_pool, l1_pool_scale, l1_w_out, l1_ln_g, l1_ln_b)
```

```python
import functools

import jax
import jax.numpy as jnp
import numpy as np
from jax import lax
from jax.experimental import pallas as pl
from jax.experimental.pallas import tpu as pltpu

F32 = jnp.float32
BF16 = jnp.bfloat16

D_MODEL = 1024
BRANCH = 512
HEADS = 4
DK = 64
DV = 128
QK = HEADS * DK
GLA_RANK = 16
GLA_TAU = 16.0
GLA_CHUNK = 64
SGU_CHUNK = 128
RET_CHUNK = 128
ROPE_BASE = 10000.0
POOL_WINDOWS = (2, 4, 8, 16)
POOL_PAD = 16
DN_ALPHA = 4.0 ** 0.25
LN_EPS = 1e-5
LANES = 128
TOKEN_BLOCK = 1024
VMEM_LIMIT_BYTES = 56 * 1024 * 1024
EVEN_Q, EVEN_K, EVEN_V, EVEN_GA = 0, QK, 2 * QK, 2 * QK + BRANCH
EVEN_U, EVEN_SV, EVEN_GB = EVEN_GA + BRANCH, EVEN_GA + 2 * BRANCH, EVEN_GA + 3 * BRANCH
EVEN_LR = EVEN_GA + 4 * BRANCH


def _dot(a, b):
    return jnp.dot(a, b, preferred_element_type=F32)


def _dot_nt(a, b):
    return lax.dot_general(a, b, (((1,), (1,)), ((), ())), preferred_element_type=F32)


def _silu(x):
    return x / (1.0 + jnp.exp(-x))


def _gelu_tanh(x):
    c = np.sqrt(2.0 / np.pi)
    hx = 0.5 * x
    return hx + hx * jnp.tanh(x * (np.float32(c) + np.float32(c * 0.044715) * (x * x)))


def _log_sigmoid(z):
    return jnp.minimum(z, 0.0) - jnp.log(1.0 + jnp.exp(-jnp.abs(z)))


def _layer_norm_rows(r, g, b):
    mu = jnp.mean(r, axis=-1, keepdims=True)
    c = r - mu
    var = jnp.mean(c * c, axis=-1, keepdims=True)
    return c * lax.rsqrt(var + LN_EPS) * g + b


def _head_mask(h, natural):
    lane = lax.broadcasted_iota(jnp.int32, (1, QK), 1)
    head = lane // DK if natural else (lane % LANES) // (DK // 2)
    return (head == h).astype(F32)


def _even_kernel(x_ref, w_ref, wa2_ref, ba_ref, gng_ref, slg_ref, slb_ref, ws_ref, bs_ref, wout_ref, lng_ref,
                 lnb_ref, o_ref, st_s, wz_s):
    @pl.when((pl.program_id(0) == 0) & (pl.program_id(1) == 0))
    def _():
        wz_s[...] = _dot(w_ref[:, EVEN_LR:EVEN_LR + LANES], wa2_ref[...]).astype(BF16)

    @pl.when(pl.program_id(1) == 0)
    def _():
        st_s[...] = jnp.zeros_like(st_s)

    refs = (w_ref, wz_s, ba_ref, gng_ref, slg_ref, slb_ref, ws_ref, bs_ref, wout_ref, lng_ref, lnb_ref)
    o_ref[0], st_s[...] = _even_block(x_ref[0], st_s[...], *refs)


def _even_block(x, state, w_ref, wz_ref, ba_ref, gng_ref, slg_ref, slb_ref, ws_ref, bs_ref, wout_ref, lng_ref,
                lnb_ref):
    tb = x.shape[0]
    xb = x.astype(BF16)
    c_len = GLA_CHUNK
    t_len = SGU_CHUNK
    chunks = [slice(c * c_len, (c + 1) * c_len) for c in range(tb // c_len)]

    z = _dot(xb, wz_ref[...]) + ba_ref[...]
    q = _dot(xb, w_ref[:, EVEN_Q:EVEN_Q + QK])
    k = _dot(xb, w_ref[:, EVEN_K:EVEN_K + QK])
    la = _log_sigmoid(z) * (1.0 / GLA_TAU)
    v = _dot(xb, w_ref[:, EVEN_V:EVEN_V + BRANCH]).astype(BF16)

    row = lax.broadcasted_iota(jnp.int32, (LANES, LANES), 0)
    col = lax.broadcasted_iota(jnp.int32, (LANES, LANES), 1)
    tril2 = ((row >= col) & (row // c_len == col // c_len)).astype(BF16)
    la_hi = la.astype(BF16)
    la_lo = (la - la_hi.astype(F32)).astype(BF16)
    b = jnp.concatenate(
        [_dot(tril2, la_hi[n * LANES:(n + 1) * LANES]) + _dot(tril2, la_lo[n * LANES:(n + 1) * LANES])
         for n in range(tb // LANES)], axis=0)

    u = _dot(xb, w_ref[:, EVEN_U:EVEN_U + BRANCH])
    q_dec = q * jnp.exp(b) * (DK ** -0.5)
    k_dec = k * jnp.exp(-b)
    k_dec_b = k_dec.astype(BF16)
    masks = [_head_mask(h, True) for h in range(HEADS)]
    qms = [jnp.concatenate([q_dec[sl] * masks[h] for h in range(HEADS)], axis=0).astype(BF16)
           for sl in chunks]
    sv = _dot(xb, w_ref[:, EVEN_SV:EVEN_SV + BRANCH])

    crow = lax.broadcasted_iota(jnp.int32, (HEADS * c_len, c_len), 0) % c_len
    ccol = lax.broadcasted_iota(jnp.int32, (HEADS * c_len, c_len), 1)
    causal = crow >= ccol
    scs = [jnp.where(causal, _dot_nt(qm, k_dec_b[sl]), 0.0).astype(BF16)
           for qm, sl in zip(qms, chunks)]
    u = _gelu_tanh(u)
    upds, dcols = [], []
    for c, sl in enumerate(chunks):
        kt = k_dec[sl].T.astype(BF16)
        upds.append(jnp.concatenate([_dot(kt[h * DK:(h + 1) * DK], v[sl, h * DV:(h + 1) * DV])
                                     for h in range(HEADS)], axis=0))
        b_last = b[(c + 1) * c_len - 1:(c + 1) * c_len, :]
        dcols.append(jnp.exp(jnp.broadcast_to(b_last, (LANES, QK)).T))
    g_a = _dot(xb, w_ref[:, EVEN_GA:EVEN_GA + BRANCH])
    sv = _gelu_tanh(sv)
    slg = slg_ref[...]
    slb = slb_ref[...]
    svn = jnp.concatenate(
        [_layer_norm_rows(sv[:, g * LANES:(g + 1) * LANES], slg[:, g * LANES:(g + 1) * LANES],
                          slb[:, g * LANES:(g + 1) * LANES]) for g in range(BRANCH // LANES)],
        axis=1).astype(BF16)
    g_b = _dot(xb, w_ref[:, EVEN_GB:EVEN_GB + BRANCH])

    states = []
    for upd, dcol in zip(upds, dcols):
        states.append(state.astype(BF16))
        state = dcol * (state + upd)

    trow = lax.broadcasted_iota(jnp.int32, (t_len, t_len), 0)
    tcol = lax.broadcasted_iota(jnp.int32, (t_len, t_len), 1)
    s_groups = []
    for g in range(BRANCH // LANES):
        w = jnp.where(trow >= tcol, ws_ref[g], 0.0).astype(BF16)
        s_groups.append(jnp.concatenate(
            [_dot(w, svn[n * t_len:(n + 1) * t_len, g * LANES:(g + 1) * LANES])
             for n in range(tb // t_len)], axis=0))
    s = jnp.concatenate(s_groups, axis=1) + jnp.concatenate([bs_ref[...]] * (tb // t_len), axis=0)

    gng = gng_ref[...]
    o_inters = [_dot(qm, s_b) for qm, s_b in zip(qms, states)]
    o_b = u * s * _silu(g_b)
    o_chunks = []
    for sl, sc, o_inter in zip(chunks, scs, o_inters):
        o_heads = []
        for h in range(HEADS):
            rows = slice(h * c_len, (h + 1) * c_len)
            o = _dot(sc[rows], v[sl, h * DV:(h + 1) * DV]) + o_inter[rows]
            o = o * lax.rsqrt(jnp.mean(o * o, axis=-1, keepdims=True) + LN_EPS)
            o_heads.append(o * gng[:, h * DV:(h + 1) * DV])
        o_chunks.append(jnp.concatenate(o_heads, axis=1))
    y = _dot(o_b.astype(BF16), wout_ref[BRANCH:2 * BRANCH, :])
    o_a = jnp.concatenate(o_chunks, axis=0) * _silu(g_a)
    y = y + _dot(o_a.astype(BF16), wout_ref[0:BRANCH, :])

    return _layer_norm_rows(DN_ALPHA * x + y, lng_ref[...], lnb_ref[...]), state


def _odd_kernel(run_ref, x_ref, pos_ref, w_ref, inv_ref, ctab_ref, stab_ref, dmat_ref, wint_ref, wst_ref,
                dec_ref, rng_ref, wp_ref, psc_ref, wout_ref, lng_ref, lnb_ref, o_ref, st_s, pad_s, cos_s,
                sin_s):
    tb = x_ref.shape[1]
    seq = pl.program_id(0)
    i = pl.program_id(1)

    @pl.when(i == 0)
    def _():
        st_s[...] = jnp.zeros_like(st_s)
        pad_s[0:POOL_PAD, :] = jnp.zeros((POOL_PAD, BRANCH), F32)

    c_len = RET_CHUNK
    half = DK // 2
    chunks = [slice(c * c_len, (c + 1) * c_len) for c in range(tb // c_len)]
    c_q, c_k, c_v, c_gc, c_p, c_gd = [(o, o + n) for o, n in zip(
        (0, QK, 2 * QK, 2 * QK + BRANCH, 2 * QK + 2 * BRANCH, 2 * QK + 3 * BRANCH),
        (QK, QK, BRANCH, BRANCH, BRANCH, BRANCH))]

    consecutive = run_ref[seq, i] == 1

    @pl.when(consecutive)
    def _():
        a0 = pos_ref[0, 0:1, :].astype(F32) * inv_ref[...]
        c0 = jnp.cos(a0)
        s0 = jnp.sin(a0)
        cos_s[...] = c0 * ctab_ref[...] - s0 * stab_ref[...]
        sin_s[...] = s0 * ctab_ref[...] + c0 * stab_ref[...]

    @pl.when(jnp.logical_not(consecutive))
    def _():
        ang = pos_ref[0].astype(F32) * inv_ref[...]
        cos_s[...] = jnp.cos(ang)
        sin_s[...] = jnp.sin(ang)

    x = x_ref[0]
    xb = x.astype(BF16)

    q = _dot(xb, w_ref[:, c_q[0]:c_q[1]]) * (DK ** -0.5)
    k = _dot(xb, w_ref[:, c_k[0]:c_k[1]])
    cos = cos_s[...]
    sin = sin_s[...]
    v = _dot(xb, w_ref[:, c_v[0]:c_v[1]]).astype(BF16)
    p = _dot(xb, w_ref[:, c_p[0]:c_p[1]])
    q1, q2 = q[:, :LANES], q[:, LANES:]
    k1, k2 = k[:, :LANES], k[:, LANES:]
    qr = jnp.concatenate([q1 * cos - q2 * sin, q1 * sin + q2 * cos], axis=1)
    kr = jnp.concatenate([k1 * cos - k2 * sin, k1 * sin + k2 * cos], axis=1)
    kr_b = kr.astype(BF16)
    masks = [_head_mask(h, False) for h in range(HEADS)]
    qms = [jnp.concatenate([qr[sl] * masks[h] for h in range(HEADS)], axis=0).astype(BF16)
           for sl in chunks]
    g_c = _dot(xb, w_ref[:, c_gc[0]:c_gc[1]])

    scs = [(_dot_nt(qm, kr_b[sl]) * dmat_ref[...]).astype(BF16) for qm, sl in zip(qms, chunks)]
    pad_s[POOL_PAD:POOL_PAD + tb, :] = p
    t_top = i * tb + lax.broadcasted_iota(jnp.int32, (POOL_PAD, LANES), 0)
    pooled = []
    for g, win in enumerate(POOL_WINDOWS):
        cols = slice(g * LANES, (g + 1) * LANES)
        acc = pad_s[:, cols]
        shift = 1
        while shift < win:
            acc = acc + pltpu.roll(acc, shift, axis=0)
            shift *= 2
        cnt_top = jnp.minimum(t_top + 1, win).astype(F32)
        mean = jnp.concatenate([acc[POOL_PAD:2 * POOL_PAD] / cnt_top, acc[2 * POOL_PAD:] * (1.0 / win)],
                               axis=0)
        pooled.append((mean - p[:, cols]).astype(BF16))
    tail = pad_s[tb:tb + POOL_PAD, :]
    pad_s[0:POOL_PAD, :] = tail
    upds = []
    for sl in chunks:
        kt = (kr[sl] * wst_ref[...]).T.astype(BF16)
        upd = [_dot(jnp.concatenate([kt[h * half:(h + 1) * half],
                                     kt[LANES + h * half:LANES + (h + 1) * half]], axis=0),
                    v[sl, h * DV:(h + 1) * DV]) for h in range(HEADS)]
        upds.append(jnp.concatenate([u[:half] for u in upd] + [u[half:] for u in upd], axis=0))
    g_d = _dot(xb, w_ref[:, c_gd[0]:c_gd[1]])

    state = st_s[...]
    states = []
    for upd in upds:
        states.append(state.astype(BF16))
        state = dec_ref[...] * state + upd
    st_s[...] = state
    yd = jnp.concatenate([_dot(pooled[g], wp_ref[g]) for g in range(len(POOL_WINDOWS))], axis=1)
    o_inters = [_dot(qm, s_b) * wint_ref[...] for qm, s_b in zip(qms, states)]
    o_d = yd * psc_ref[...] * _silu(g_d)
    rng = rng_ref[...]
    o_chunks = []
    for sl, sc, o_inter in zip(chunks, scs, o_inters):
        o_heads = []
        for h in range(HEADS):
            rows = slice(h * c_len, (h + 1) * c_len)
            o = _dot(sc[rows], v[sl, h * DV:(h + 1) * DV]) + o_inter[rows]
            mu = jnp.mean(o, axis=-1, keepdims=True)
            oc = o - mu
            var = jnp.mean(oc * oc, axis=-1, keepdims=True)
            o_heads.append(oc * lax.rsqrt(var + LN_EPS) * rng[:, h * DV:(h + 1) * DV])
        o_chunks.append(jnp.concatenate(o_heads, axis=1))
    y = _dot(o_d.astype(BF16), wout_ref[BRANCH:2 * BRANCH, :])
    o_c = jnp.concatenate(o_chunks, axis=0) * _silu(g_c)
    y = y + _dot(o_c.astype(BF16), wout_ref[0:BRANCH, :])

    o_ref[0] = _layer_norm_rows(DN_ALPHA * x + y, lng_ref[...], lnb_ref[...])


def _full_spec(a):
    nd = a.ndim
    return pl.BlockSpec(a.shape, lambda *_, _nd=nd: (0,) * _nd)


def _compiler_params():
    return pltpu.CompilerParams(dimension_semantics=("arbitrary", "arbitrary"),
                                vmem_limit_bytes=VMEM_LIMIT_BYTES)


def _row(a):
    return a.reshape(1, -1).astype(F32)


def _even_layer(x, w_in, w_a2, b_a, gla_norm_g, sgu_ln_g, sgu_ln_b, w_s, b_s, w_out, ln_g, ln_b):
    B, S, D = x.shape
    tb = TOKEN_BLOCK
    a_cols = 2 * QK + 2 * BRANCH
    w = jnp.concatenate([w_in[:, :a_cols], w_in[:, a_cols + GLA_RANK:], w_in[:, a_cols:a_cols + GLA_RANK],
                         jnp.zeros((D, LANES - GLA_RANK), w_in.dtype)], axis=1).astype(BF16)
    wa2 = jnp.pad(w_a2, ((0, LANES - GLA_RANK), (0, 0))).astype(BF16)
    bs = jnp.repeat(jnp.transpose(b_s), LANES, axis=1).astype(F32)
    params = (w, wa2, _row(b_a), _row(gla_norm_g), _row(sgu_ln_g), _row(sgu_ln_b), w_s.astype(F32), bs,
              w_out.astype(BF16), _row(ln_g), _row(ln_b))
    tok_spec = pl.BlockSpec((1, tb, D), lambda b, i: (b, i, 0))
    return pl.pallas_call(
        _even_kernel,
        out_shape=jax.ShapeDtypeStruct((B, S, D), F32),
        grid=(B, S // tb),
        in_specs=[tok_spec] + [_full_spec(p) for p in params],
        out_specs=tok_spec,
        scratch_shapes=[
            pltpu.VMEM((QK, DV), F32),
            pltpu.VMEM((D, QK), BF16),
        ],
        compiler_params=_compiler_params(),
        name="even_layer_gla_sgu",
    )(x, *params)


def _retention_tables():
    c = RET_CHUNK
    log_gamma = np.log(1.0 - 2.0 ** (-5.0 - np.arange(HEADS, dtype=np.float64)))
    idx = np.arange(c, dtype=np.float64)
    rel = idx[:, None] - idx[None, :]
    dmat = np.where(rel >= 0, np.exp(np.maximum(rel, 0.0)[None] * log_gamma[:, None, None]), 0.0)
    w_inter = np.exp((idx + 1.0)[None] * log_gamma[:, None])
    w_state = np.exp((c - 1.0 - idx)[None] * log_gamma[:, None])
    lane_head = (np.arange(QK) % LANES) // (DK // 2)
    dmat = dmat.reshape(HEADS * c, c)
    wint = np.broadcast_to(w_inter[:, :, None], (HEADS, c, DV)).reshape(HEADS * c, DV)
    wst = w_state[lane_head, :].T
    dec = np.broadcast_to(np.exp(c * log_gamma)[lane_head][:, None], (QK, DV))
    inv = ROPE_BASE ** (-np.arange(DK // 2, dtype=np.float64) / (DK // 2))
    inv = np.tile(inv, HEADS)[None, :].astype(np.float32)
    off = np.arange(TOKEN_BLOCK, dtype=np.float64)[:, None] * inv.astype(np.float64)
    f = lambda a: jnp.asarray(np.ascontiguousarray(a), F32)
    return f(inv), f(np.cos(off)), f(np.sin(off)), f(dmat), f(wint), f(wst), f(dec)


def _odd_layer(x, positions, w_in, ret_norm_g, w_pool, pool_scale, w_out, ln_g, ln_b):
    B, S, D = x.shape
    tb = TOKEN_BLOCK
    qk = w_in[:, :2 * QK].reshape(D, 2, HEADS, 2, DK // 2).transpose(0, 1, 3, 2, 4).reshape(D, 2 * QK)
    w = jnp.concatenate([qk, w_in[:, 2 * QK:]], axis=1).astype(BF16)
    params = (w, *_retention_tables(), _row(ret_norm_g), w_pool.astype(BF16), _row(pool_scale),
              w_out.astype(BF16), _row(ln_g), _row(ln_b))
    pos_blocks = positions.reshape(B, S // tb, tb)
    runs = jnp.all(pos_blocks == pos_blocks[:, :, :1] + jnp.arange(tb, dtype=positions.dtype), axis=-1)
    tok_spec = pl.BlockSpec((1, tb, D), lambda b, i: (b, i, 0))
    pos_spec = pl.BlockSpec((1, tb, 1), lambda b, i: (b, i, 0))
    return pl.pallas_call(
        _odd_kernel,
        out_shape=jax.ShapeDtypeStruct((B, S, D), F32),
        grid=(B, S // tb),
        in_specs=[pl.BlockSpec(memory_space=pltpu.SMEM), tok_spec, pos_spec] + [_full_spec(p) for p in params],
        out_specs=tok_spec,
        scratch_shapes=[
            pltpu.VMEM((QK, DV), F32),
            pltpu.VMEM((POOL_PAD + tb, BRANCH), F32),
            pltpu.VMEM((tb, LANES), F32),
            pltpu.VMEM((tb, LANES), F32),
        ],
        compiler_params=_compiler_params(),
        name="odd_layer_retention_pool",
    )(runs.astype(jnp.int32), x, positions.reshape(B, S, 1), *params)


def kernel(x, positions, l0_w_in, l0_w_a2, l0_b_a, l0_gla_norm_g, l0_sgu_ln_g, l0_sgu_ln_b, l0_w_s, l0_b_s, l0_w_out, l0_ln_g, l0_ln_b, l1_w_in, l1_ret_norm_g, l1_w_pool, l1_pool_scale, l1_w_out, l1_ln_g, l1_ln_b):
    x = _even_layer(x, l0_w_in, l0_w_a2, l0_b_a, l0_gla_norm_g, l0_sgu_ln_g, l0_sgu_ln_b, l0_w_s, l0_b_s,
                    l0_w_out, l0_ln_g, l0_ln_b)
    return _odd_layer(x, positions, l1_w_in, l1_ret_norm_g, l1_w_pool, l1_pool_scale, l1_w_out, l1_ln_g, l1_ln_b)
```

```python
import functools

import jax
import jax.numpy as jnp
import numpy as np
from jax import lax
from jax.experimental import pallas as pl
from jax.experimental.pallas import tpu as pltpu

F32 = jnp.float32
BF16 = jnp.bfloat16

D_MODEL = 1024
BRANCH = 512
HEADS = 4
DK = 64
DV = 128
QK = HEADS * DK
GLA_RANK = 16
GLA_TAU = 16.0
GLA_CHUNK = 64
SGU_CHUNK = 128
RET_CHUNK = 128
ROPE_BASE = 10000.0
POOL_WINDOWS = (2, 4, 8, 16)
POOL_PAD = 16
DN_ALPHA = 4.0 ** 0.25
LN_EPS = 1e-5
LANES = 128
TOKEN_BLOCK = 1024
VMEM_LIMIT_BYTES = 56 * 1024 * 1024
EVEN_Q, EVEN_K, EVEN_V, EVEN_GA, EVEN_LR = 0, QK, 2 * QK, 2 * QK + BRANCH, 2 * QK + 2 * BRANCH
EVEN_U, EVEN_SV, EVEN_GB = 0, BRANCH, 2 * BRANCH
EVEN_COLS = EVEN_LR + GLA_RANK + 3 * BRANCH
MXU_TILE = 256


def _dot(a, b):
    return jnp.dot(a, b, preferred_element_type=F32)


def _dot_nt(a, b):
    return lax.dot_general(a, b, (((1,), (1,)), ((), ())), preferred_element_type=F32)


def _silu(x):
    return x / (1.0 + jnp.exp(-x))


def _gelu_tanh(x):
    c = np.sqrt(2.0 / np.pi)
    hx = 0.5 * x
    return hx + hx * jnp.tanh(x * (np.float32(c) + np.float32(c * 0.044715) * (x * x)))


def _log_sigmoid(z):
    return jnp.minimum(z, 0.0) - jnp.log(1.0 + jnp.exp(-jnp.abs(z)))


def _layer_norm_rows(r, g, b):
    mu = jnp.mean(r, axis=-1, keepdims=True)
    c = r - mu
    var = jnp.mean(c * c, axis=-1, keepdims=True)
    return c * lax.rsqrt(var + LN_EPS) * g + b


def _head_mask(h, natural):
    lane = lax.broadcasted_iota(jnp.int32, (1, QK), 1)
    head = lane // DK if natural else (lane % LANES) // (DK // 2)
    return (head == h).astype(F32)


def _even_kernel(x_ref, w_ref, wa2_ref, shift_ref, ba_ref, gng_ref, slg_ref, slb_ref, ws_ref, bs_ref, wout_ref,
                 lng_ref, lnb_ref, o_ref, st_s, wz_s, wb_s):
    @pl.when((pl.program_id(0) == 0) & (pl.program_id(1) == 0))
    def _():
        wz_s[...] = _dot(w_ref[:, EVEN_LR:EVEN_LR + LANES], wa2_ref[...]).astype(BF16)
        window = shift_ref.shape[0]
        for t in range(3 * BRANCH // MXU_TILE):
            lo = EVEN_LR + t * MXU_TILE
            wb_s[:, t * MXU_TILE:(t + 1) * MXU_TILE] = _dot(w_ref[:, lo:lo + window], shift_ref[...]).astype(BF16)

    @pl.when(pl.program_id(1) == 0)
    def _():
        st_s[...] = jnp.zeros_like(st_s)

    refs = (w_ref, wb_s, wz_s, ba_ref, gng_ref, slg_ref, slb_ref, ws_ref, bs_ref, wout_ref, lng_ref, lnb_ref)
    o_ref[0], st_s[...] = _even_block(x_ref[0], st_s[...], *refs)


def _even_block(x, state, w_ref, wb_ref, wz_ref, ba_ref, gng_ref, slg_ref, slb_ref, ws_ref, bs_ref, wout_ref,
                lng_ref, lnb_ref):
    tb = x.shape[0]
    xb = x.astype(BF16)
    c_len = GLA_CHUNK
    t_len = SGU_CHUNK
    chunks = [slice(c * c_len, (c + 1) * c_len) for c in range(tb // c_len)]

    z = _dot(xb, wz_ref[...]) + ba_ref[...]
    q = _dot(xb, w_ref[:, EVEN_Q:EVEN_Q + QK])
    k = _dot(xb, w_ref[:, EVEN_K:EVEN_K + QK])
    la = _log_sigmoid(z) * (1.0 / GLA_TAU)
    v = _dot(xb, w_ref[:, EVEN_V:EVEN_V + BRANCH]).astype(BF16)

    row = lax.broadcasted_iota(jnp.int32, (LANES, LANES), 0)
    col = lax.broadcasted_iota(jnp.int32, (LANES, LANES), 1)
    tril2 = ((row >= col) & (row // c_len == col // c_len)).astype(BF16)
    la_hi = la.astype(BF16)
    la_lo = (la - la_hi.astype(F32)).astype(BF16)
    b = jnp.concatenate(
        [_dot(tril2, la_hi[n * LANES:(n + 1) * LANES]) + _dot(tril2, la_lo[n * LANES:(n + 1) * LANES])
         for n in range(tb // LANES)], axis=0)

    u = _dot(xb, wb_ref[:, EVEN_U:EVEN_U + BRANCH])
    q_dec = q * jnp.exp(b) * (DK ** -0.5)
    k_dec = k * jnp.exp(-b)
    k_dec_b = k_dec.astype(BF16)
    masks = [_head_mask(h, True) for h in range(HEADS)]
    qms = [jnp.concatenate([q_dec[sl] * masks[h] for h in range(HEADS)], axis=0).astype(BF16)
           for sl in chunks]
    sv = _dot(xb, wb_ref[:, EVEN_SV:EVEN_SV + BRANCH])

    crow = lax.broadcasted_iota(jnp.int32, (HEADS * c_len, c_len), 0) % c_len
    ccol = lax.broadcasted_iota(jnp.int32, (HEADS * c_len, c_len), 1)
    causal = crow >= ccol
    scs = [jnp.where(causal, _dot_nt(qm, k_dec_b[sl]), 0.0).astype(BF16)
           for qm, sl in zip(qms, chunks)]
    u = _gelu_tanh(u)
    upds, dcols = [], []
    for c, sl in enumerate(chunks):
        kt = k_dec[sl].T.astype(BF16)
        upds.append(jnp.concatenate([_dot(kt[h * DK:(h + 1) * DK], v[sl, h * DV:(h + 1) * DV])
                                     for h in range(HEADS)], axis=0))
        b_last = b[(c + 1) * c_len - 1:(c + 1) * c_len, :]
        dcols.append(jnp.exp(jnp.broadcast_to(b_last, (LANES, QK)).T))
    g_a = _dot(xb, w_ref[:, EVEN_GA:EVEN_GA + BRANCH])
    sv = _gelu_tanh(sv)
    slg = slg_ref[...]
    slb = slb_ref[...]
    svn = jnp.concatenate(
        [_layer_norm_rows(sv[:, g * LANES:(g + 1) * LANES], slg[:, g * LANES:(g + 1) * LANES],
                          slb[:, g * LANES:(g + 1) * LANES]) for g in range(BRANCH // LANES)],
        axis=1).astype(BF16)
    g_b = _dot(xb, wb_ref[:, EVEN_GB:EVEN_GB + BRANCH])

    states = []
    for upd, dcol in zip(upds, dcols):
        states.append(state.astype(BF16))
        state = dcol * (state + upd)

    trow = lax.broadcasted_iota(jnp.int32, (t_len, t_len), 0)
    tcol = lax.broadcasted_iota(jnp.int32, (t_len, t_len), 1)
    s_groups = []
    for g in range(BRANCH // LANES):
        w = jnp.where(trow >= tcol, ws_ref[g], 0.0).astype(BF16)
        s_groups.append(jnp.concatenate(
            [_dot(w, svn[n * t_len:(n + 1) * t_len, g * LANES:(g + 1) * LANES])
             for n in range(tb // t_len)], axis=0))
    s = jnp.concatenate(s_groups, axis=1) + jnp.concatenate([bs_ref[...]] * (tb // t_len), axis=0)

    gng = gng_ref[...]
    o_inters = [_dot(qm, s_b) for qm, s_b in zip(qms, states)]
    o_b = u * s * _silu(g_b)
    o_chunks = []
    for sl, sc, o_inter in zip(chunks, scs, o_inters):
        o_heads = []
        for h in range(HEADS):
            rows = slice(h * c_len, (h + 1) * c_len)
            o = _dot(sc[rows], v[sl, h * DV:(h + 1) * DV]) + o_inter[rows]
            o = o * lax.rsqrt(jnp.mean(o * o, axis=-1, keepdims=True) + LN_EPS)
            o_heads.append(o * gng[:, h * DV:(h + 1) * DV])
        o_chunks.append(jnp.concatenate(o_heads, axis=1))
    y = _dot(o_b.astype(BF16), wout_ref[BRANCH:2 * BRANCH, :])
    o_a = jnp.concatenate(o_chunks, axis=0) * _silu(g_a)
    y = y + _dot(o_a.astype(BF16), wout_ref[0:BRANCH, :])

    return _layer_norm_rows(DN_ALPHA * x + y, lng_ref[...], lnb_ref[...]), state


def _odd_kernel(run_ref, x_ref, pos_ref, w_ref, perm_ref, inv_ref, ctab_ref, stab_ref, dmat_ref, wint_ref,
                wst_ref, dec_ref, rng_ref, wp_ref, psc_ref, wout_ref, lng_ref, lnb_ref, o_ref, st_s, pad_s,
                cos_s, sin_s, wqk_s):
    tb = x_ref.shape[1]
    seq = pl.program_id(0)
    i = pl.program_id(1)

    @pl.when((seq == 0) & (i == 0))
    def _():
        wqk_s[...] = _dot(w_ref[:, 0:2 * QK], perm_ref[...]).astype(BF16)

    @pl.when(i == 0)
    def _():
        st_s[...] = jnp.zeros_like(st_s)
        pad_s[0:POOL_PAD, :] = jnp.zeros((POOL_PAD, BRANCH), F32)

    c_len = RET_CHUNK
    half = DK // 2
    chunks = [slice(c * c_len, (c + 1) * c_len) for c in range(tb // c_len)]
    c_q, c_k, c_v, c_gc, c_p, c_gd = [(o, o + n) for o, n in zip(
        (0, QK, 2 * QK, 2 * QK + BRANCH, 2 * QK + 2 * BRANCH, 2 * QK + 3 * BRANCH),
        (QK, QK, BRANCH, BRANCH, BRANCH, BRANCH))]

    consecutive = run_ref[seq, i] == 1

    @pl.when(consecutive)
    def _():
        a0 = pos_ref[0, :, 0:1].astype(F32) * inv_ref[...]
        c0 = jnp.cos(a0)
        s0 = jnp.sin(a0)
        cos_s[...] = c0 * ctab_ref[...] - s0 * stab_ref[...]
        sin_s[...] = s0 * ctab_ref[...] + c0 * stab_ref[...]

    @pl.when(jnp.logical_not(consecutive))
    def _():
        pos_rows = jnp.broadcast_to(pos_ref[0].astype(F32), (LANES, tb)).T
        ang = pos_rows * inv_ref[...]
        cos_s[...] = jnp.cos(ang)
        sin_s[...] = jnp.sin(ang)

    x = x_ref[0]
    xb = x.astype(BF16)

    q = _dot(xb, wqk_s[:, c_q[0]:c_q[1]]) * (DK ** -0.5)
    k = _dot(xb, wqk_s[:, c_k[0]:c_k[1]])
    cos = cos_s[...]
    sin = sin_s[...]
    v = _dot(xb, w_ref[:, c_v[0]:c_v[1]]).astype(BF16)
    p = _dot(xb, w_ref[:, c_p[0]:c_p[1]])
    q1, q2 = q[:, :LANES], q[:, LANES:]
    k1, k2 = k[:, :LANES], k[:, LANES:]
    qr = jnp.concatenate([q1 * cos - q2 * sin, q1 * sin + q2 * cos], axis=1)
    kr = jnp.concatenate([k1 * cos - k2 * sin, k1 * sin + k2 * cos], axis=1)
    kr_b = kr.astype(BF16)
    masks = [_head_mask(h, False) for h in range(HEADS)]
    qms = [jnp.concatenate([qr[sl] * masks[h] for h in range(HEADS)], axis=0).astype(BF16)
           for sl in chunks]
    g_c = _dot(xb, w_ref[:, c_gc[0]:c_gc[1]])

    scs = [(_dot_nt(qm, kr_b[sl]) * dmat_ref[...]).astype(BF16) for qm, sl in zip(qms, chunks)]
    pad_s[POOL_PAD:POOL_PAD + tb, :] = p
    t_top = i * tb + lax.broadcasted_iota(jnp.int32, (POOL_PAD, LANES), 0)
    pooled = []
    for g, win in enumerate(POOL_WINDOWS):
        cols = slice(g * LANES, (g + 1) * LANES)
        acc = pad_s[:, cols]
        shift = 1
        while shift < win:
            acc = acc + pltpu.roll(acc, shift, axis=0)
            shift *= 2
        cnt_top = jnp.minimum(t_top + 1, win).astype(F32)
        mean = jnp.concatenate([acc[POOL_PAD:2 * POOL_PAD] / cnt_top, acc[2 * POOL_PAD:] * (1.0 / win)],
                               axis=0)
        pooled.append((mean - p[:, cols]).astype(BF16))
    tail = pad_s[tb:tb + POOL_PAD, :]
    pad_s[0:POOL_PAD, :] = tail
    upds = []
    for sl in chunks:
        kt = (kr[sl] * wst_ref[...]).T.astype(BF16)
        upd = [_dot(jnp.concatenate([kt[h * half:(h + 1) * half],
                                     kt[LANES + h * half:LANES + (h + 1) * half]], axis=0),
                    v[sl, h * DV:(h + 1) * DV]) for h in range(HEADS)]
        upds.append(jnp.concatenate([u[:half] for u in upd] + [u[half:] for u in upd], axis=0))
    g_d = _dot(xb, w_ref[:, c_gd[0]:c_gd[1]])

    state = st_s[...]
    states = []
    for upd in upds:
        states.append(state.astype(BF16))
        state = dec_ref[...] * state + upd
    st_s[...] = state
    yd = jnp.concatenate([_dot(pooled[g], wp_ref[g]) for g in range(len(POOL_WINDOWS))], axis=1)
    o_inters = [_dot(qm, s_b) * wint_ref[...] for qm, s_b in zip(qms, states)]
    o_d = yd * psc_ref[...] * _silu(g_d)
    rng = rng_ref[...]
    o_chunks = []
    for sl, sc, o_inter in zip(chunks, scs, o_inters):
        o_heads = []
        for h in range(HEADS):
            rows = slice(h * c_len, (h + 1) * c_len)
            o = _dot(sc[rows], v[sl, h * DV:(h + 1) * DV]) + o_inter[rows]
            mu = jnp.mean(o, axis=-1, keepdims=True)
            oc = o - mu
            var = jnp.mean(oc * oc, axis=-1, keepdims=True)
            o_heads.append(oc * lax.rsqrt(var + LN_EPS) * rng[:, h * DV:(h + 1) * DV])
        o_chunks.append(jnp.concatenate(o_heads, axis=1))
    y = _dot(o_d.astype(BF16), wout_ref[BRANCH:2 * BRANCH, :])
    o_c = jnp.concatenate(o_chunks, axis=0) * _silu(g_c)
    y = y + _dot(o_c.astype(BF16), wout_ref[0:BRANCH, :])

    o_ref[0] = _layer_norm_rows(DN_ALPHA * x + y, lng_ref[...], lnb_ref[...])


def _full_spec(a):
    nd = a.ndim
    return pl.BlockSpec(a.shape, lambda *_, _nd=nd: (0,) * _nd)


def _compiler_params():
    return pltpu.CompilerParams(dimension_semantics=("arbitrary", "arbitrary"),
                                vmem_limit_bytes=VMEM_LIMIT_BYTES)


def _row(a):
    return a.reshape(1, -1).astype(F32)


def _even_layer(x, w_in, w_a2, b_a, gla_norm_g, sgu_ln_g, sgu_ln_b, w_s, b_s, w_out, ln_g, ln_b):
    B, S, D = x.shape
    tb = TOKEN_BLOCK
    window = MXU_TILE + LANES
    w = jnp.pad(w_in.astype(BF16), ((0, 0), (0, EVEN_LR + 3 * BRANCH - MXU_TILE + window - EVEN_COLS)))
    shift = jnp.asarray(np.arange(window)[:, None] == np.arange(MXU_TILE)[None, :] + GLA_RANK, BF16)
    wa2 = jnp.pad(w_a2, ((0, LANES - GLA_RANK), (0, 0))).astype(BF16)
    bs = jnp.repeat(jnp.transpose(b_s), LANES, axis=1).astype(F32)
    params = (w, wa2, shift, _row(b_a), _row(gla_norm_g), _row(sgu_ln_g), _row(sgu_ln_b), w_s.astype(F32), bs,
              w_out.astype(BF16), _row(ln_g), _row(ln_b))
    tok_spec = pl.BlockSpec((1, tb, D), lambda b, i: (b, i, 0))
    return pl.pallas_call(
        _even_kernel,
        out_shape=jax.ShapeDtypeStruct((B, S, D), F32),
        grid=(B, S // tb),
        in_specs=[tok_spec] + [_full_spec(p) for p in params],
        out_specs=tok_spec,
        scratch_shapes=[
            pltpu.VMEM((QK, DV), F32),
            pltpu.VMEM((D, QK), BF16),
            pltpu.VMEM((D, 3 * BRANCH), BF16),
        ],
        compiler_params=_compiler_params(),
        name="even_layer_gla_sgu",
    )(x, *params)


def _retention_tables():
    c = RET_CHUNK
    log_gamma = np.log(1.0 - 2.0 ** (-5.0 - np.arange(HEADS, dtype=np.float64)))
    idx = np.arange(c, dtype=np.float64)
    rel = idx[:, None] - idx[None, :]
    dmat = np.where(rel >= 0, np.exp(np.maximum(rel, 0.0)[None] * log_gamma[:, None, None]), 0.0)
    w_inter = np.exp((idx + 1.0)[None] * log_gamma[:, None])
    w_state = np.exp((c - 1.0 - idx)[None] * log_gamma[:, None])
    lane_head = (np.arange(QK) % LANES) // (DK // 2)
    dmat = dmat.reshape(HEADS * c, c)
    wint = np.broadcast_to(w_inter[:, :, None], (HEADS, c, DV)).reshape(HEADS * c, DV)
    wst = w_state[lane_head, :].T
    dec = np.broadcast_to(np.exp(c * log_gamma)[lane_head][:, None], (QK, DV))
    inv = ROPE_BASE ** (-np.arange(DK // 2, dtype=np.float64) / (DK // 2))
    inv = np.tile(inv, HEADS)[None, :].astype(np.float32)
    off = np.arange(TOKEN_BLOCK, dtype=np.float64)[:, None] * inv.astype(np.float64)
    f = lambda a: jnp.asarray(np.ascontiguousarray(a), F32)
    return f(inv), f(np.cos(off)), f(np.sin(off)), f(dmat), f(wint), f(wst), f(dec)


def _odd_layer(x, positions, w_in, ret_norm_g, w_pool, pool_scale, w_out, ln_g, ln_b):
    B, S, D = x.shape
    tb = TOKEN_BLOCK
    j = np.arange(2 * QK)
    src = (j // QK) * QK + ((j % LANES) // (DK // 2)) * DK + ((j % QK) // LANES) * (DK // 2) + j % (DK // 2)
    perm = jnp.asarray(src[None, :] == np.arange(2 * QK)[:, None], BF16)
    params = (w_in.astype(BF16), perm, *_retention_tables(), _row(ret_norm_g), w_pool.astype(BF16),
              _row(pool_scale), w_out.astype(BF16), _row(ln_g), _row(ln_b))
    pos_blocks = positions.reshape(B, S // tb, tb)
    runs = jnp.all(pos_blocks == pos_blocks[:, :, :1] + jnp.arange(tb, dtype=positions.dtype), axis=-1)
    tok_spec = pl.BlockSpec((1, tb, D), lambda b, i: (b, i, 0))
    pos_spec = pl.BlockSpec((1, 1, tb), lambda b, i: (b, 0, i))
    return pl.pallas_call(
        _odd_kernel,
        out_shape=jax.ShapeDtypeStruct((B, S, D), F32),
        grid=(B, S // tb),
        in_specs=[pl.BlockSpec(memory_space=pltpu.SMEM), tok_spec, pos_spec] + [_full_spec(p) for p in params],
        out_specs=tok_spec,
        scratch_shapes=[
            pltpu.VMEM((QK, DV), F32),
            pltpu.VMEM((POOL_PAD + tb, BRANCH), F32),
            pltpu.VMEM((tb, LANES), F32),
            pltpu.VMEM((tb, LANES), F32),
            pltpu.VMEM((D, 2 * QK), BF16),
        ],
        compiler_params=_compiler_params(),
        name="odd_layer_retention_pool",
    )(runs.astype(jnp.int32), x, positions.reshape(B, 1, S), *params)


def kernel(x, positions, l0_w_in, l0_w_a2, l0_b_a, l0_gla_norm_g, l0_sgu_ln_g, l0_sgu_ln_b, l0_w_s, l0_b_s, l0_w_out, l0_ln_g, l0_ln_b, l1_w_in, l1_ret_norm_g, l1_w_pool, l1_pool_scale, l1_w_out, l1_ln_g, l1_ln_b):
    x = _even_layer(x, l0_w_in, l0_w_a2, l0_b_a, l0_gla_norm_g, l0_sgu_ln_g, l0_sgu_ln_b, l0_w_s, l0_b_s,
                    l0_w_out, l0_ln_g, l0_ln_b)
    return _odd_layer(x, positions, l1_w_in, l1_ret_norm_g, l1_w_pool, l1_pool_scale, l1_w_out, l1_ln_g, l1_ln_b)
```

```python
import functools

import jax
import jax.numpy as jnp
import numpy as np
from jax import lax
from jax.experimental import pallas as pl
from jax.experimental.pallas import tpu as pltpu

F32 = jnp.float32
BF16 = jnp.bfloat16

D_MODEL = 1024
BRANCH = 512
HEADS = 4
DK = 64
DV = 128
QK = HEADS * DK
GLA_RANK = 16
GLA_TAU = 16.0
GLA_CHUNK = 64
SGU_CHUNK = 128
RET_CHUNK = 128
ROPE_BASE = 10000.0
POOL_WINDOWS = (2, 4, 8, 16)
POOL_PAD = 16
DN_ALPHA = 4.0 ** 0.25
LN_EPS = 1e-5
LANES = 128
TOKEN_BLOCK = 1024
VMEM_LIMIT_BYTES = 56 * 1024 * 1024
EVEN_Q, EVEN_K, EVEN_V, EVEN_GA, EVEN_LR = 0, QK, 2 * QK, 2 * QK + BRANCH, 2 * QK + 2 * BRANCH
EVEN_U, EVEN_SV, EVEN_GB = 0, BRANCH, 2 * BRANCH
EVEN_COLS = EVEN_LR + GLA_RANK + 3 * BRANCH
MXU_TILE = 256


def _dot(a, b):
    return jnp.dot(a, b, preferred_element_type=F32)


def _dot_nt(a, b):
    return lax.dot_general(a, b, (((1,), (1,)), ((), ())), preferred_element_type=F32)


def _silu(x):
    hx = 0.5 * x
    return hx + hx * jnp.tanh(hx)


def _gelu_tanh(x):
    c = np.sqrt(2.0 / np.pi)
    hx = 0.5 * x
    return hx + hx * jnp.tanh(x * (np.float32(c) + np.float32(c * 0.044715) * (x * x)))


def _log_sigmoid(z):
    return jnp.minimum(z, 0.0) - jnp.log(1.0 + jnp.exp(-jnp.abs(z)))


def _layer_norm_rows(r, g, b, eps=LN_EPS):
    mu = jnp.mean(r, axis=-1, keepdims=True)
    c = r - mu
    var = jnp.mean(c * c, axis=-1, keepdims=True)
    return c * lax.rsqrt(var + eps) * g + b


def _deepnorm(x, y_scaled, g, b):
    return _layer_norm_rows(x + y_scaled, g, b, LN_EPS / DN_ALPHA ** 2)


def _head_mask(h, natural):
    lane = lax.broadcasted_iota(jnp.int32, (1, QK), 1)
    head = lane // DK if natural else (lane % LANES) // (DK // 2)
    return (head == h).astype(F32)


def _even_kernel(x_ref, w_ref, wa2_ref, shift_ref, ba_ref, gng_ref, slg_ref, slb_ref, ws_ref, bs_ref, wout_ref,
                 lng_ref, lnb_ref, o_ref, st_s, wz_s, wb_s):
    @pl.when((pl.program_id(0) == 0) & (pl.program_id(1) == 0))
    def _():
        wz_s[...] = _dot(w_ref[:, EVEN_LR:EVEN_LR + LANES], wa2_ref[...]).astype(BF16)
        window = shift_ref.shape[0]
        for t in range(3 * BRANCH // MXU_TILE):
            lo = EVEN_LR + t * MXU_TILE
            wb_s[:, t * MXU_TILE:(t + 1) * MXU_TILE] = _dot(w_ref[:, lo:lo + window], shift_ref[...]).astype(BF16)

    @pl.when(pl.program_id(1) == 0)
    def _():
        st_s[...] = jnp.zeros_like(st_s)

    refs = (w_ref, wb_s, wz_s, ba_ref, gng_ref, slg_ref, slb_ref, ws_ref, bs_ref, wout_ref, lng_ref, lnb_ref)
    o_ref[0], st_s[...] = _even_block(x_ref[0], st_s[...], *refs)


def _even_block(x, state, w_ref, wb_ref, wz_ref, ba_ref, gng_ref, slg_ref, slb_ref, ws_ref, bs_ref, wout_ref,
                lng_ref, lnb_ref):
    tb = x.shape[0]
    xb = x.astype(BF16)
    c_len = GLA_CHUNK
    t_len = SGU_CHUNK
    chunks = [slice(c * c_len, (c + 1) * c_len) for c in range(tb // c_len)]

    z = _dot(xb, wz_ref[...]) + ba_ref[...]
    q = _dot(xb, w_ref[:, EVEN_Q:EVEN_Q + QK])
    k = _dot(xb, w_ref[:, EVEN_K:EVEN_K + QK])
    la = _log_sigmoid(z) * (1.0 / GLA_TAU)
    v = _dot(xb, w_ref[:, EVEN_V:EVEN_V + BRANCH]).astype(BF16)

    row = lax.broadcasted_iota(jnp.int32, (LANES, LANES), 0)
    col = lax.broadcasted_iota(jnp.int32, (LANES, LANES), 1)
    tril2 = ((row >= col) & (row // c_len == col // c_len)).astype(BF16)
    la_hi = la.astype(BF16)
    la_lo = (la - la_hi.astype(F32)).astype(BF16)
    b = jnp.concatenate(
        [_dot(tril2, la_hi[n * LANES:(n + 1) * LANES]) + _dot(tril2, la_lo[n * LANES:(n + 1) * LANES])
         for n in range(tb // LANES)], axis=0)

    u = _dot(xb, wb_ref[:, EVEN_U:EVEN_U + BRANCH])
    q_dec = q * jnp.exp(b) * (DK ** -0.5)
    k_dec = k * jnp.exp(-b)
    k_dec_b = k_dec.astype(BF16)
    masks = [_head_mask(h, True) for h in range(HEADS)]
    qms = [jnp.concatenate([q_dec[sl] * masks[h] for h in range(HEADS)], axis=0).astype(BF16)
           for sl in chunks]
    sv = _dot(xb, wb_ref[:, EVEN_SV:EVEN_SV + BRANCH])

    crow = lax.broadcasted_iota(jnp.int32, (HEADS * c_len, c_len), 0) % c_len
    ccol = lax.broadcasted_iota(jnp.int32, (HEADS * c_len, c_len), 1)
    causal = crow >= ccol
    scs = [jnp.where(causal, _dot_nt(qm, k_dec_b[sl]), 0.0).astype(BF16)
           for qm, sl in zip(qms, chunks)]
    u = _gelu_tanh(u)
    upds, dcols = [], []
    for c, sl in enumerate(chunks):
        kt = k_dec[sl].T.astype(BF16)
        upds.append(jnp.concatenate([_dot(kt[h * DK:(h + 1) * DK], v[sl, h * DV:(h + 1) * DV])
                                     for h in range(HEADS)], axis=0))
        b_last = b[(c + 1) * c_len - 1:(c + 1) * c_len, :]
        dcols.append(jnp.exp(jnp.broadcast_to(b_last, (LANES, QK)).T))
    g_a = _dot(xb, w_ref[:, EVEN_GA:EVEN_GA + BRANCH])
    sv = _gelu_tanh(sv)
    slg = slg_ref[...]
    slb = slb_ref[...]
    svn = jnp.concatenate(
        [_layer_norm_rows(sv[:, g * LANES:(g + 1) * LANES], slg[:, g * LANES:(g + 1) * LANES],
                          slb[:, g * LANES:(g + 1) * LANES]) for g in range(BRANCH // LANES)],
        axis=1).astype(BF16)
    g_b = _dot(xb, wb_ref[:, EVEN_GB:EVEN_GB + BRANCH])

    states = []
    for upd, dcol in zip(upds, dcols):
        states.append(state.astype(BF16))
        state = dcol * (state + upd)

    trow = lax.broadcasted_iota(jnp.int32, (t_len, t_len), 0)
    tcol = lax.broadcasted_iota(jnp.int32, (t_len, t_len), 1)
    s_groups = []
    for g in range(BRANCH // LANES):
        w = jnp.where(trow >= tcol, ws_ref[g], 0.0).astype(BF16)
        cols = slice(g * LANES, (g + 1) * LANES)
        mixed = []
        for n in range(0, tb // t_len, 2):
            pair = _dot(w, jnp.concatenate([svn[n * t_len:(n + 1) * t_len, cols],
                                            svn[(n + 1) * t_len:(n + 2) * t_len, cols]], axis=1))
            mixed += [pair[:, :LANES], pair[:, LANES:]]
        s_groups.append(jnp.concatenate(mixed, axis=0))
    s = jnp.concatenate(s_groups, axis=1) + jnp.concatenate([bs_ref[...]] * (tb // t_len), axis=0)

    gng = gng_ref[...]
    o_inters = [_dot(qm, s_b) for qm, s_b in zip(qms, states)]
    o_b = u * s * _silu(g_b)
    o_chunks = []
    for sl, sc, o_inter in zip(chunks, scs, o_inters):
        o_heads = []
        for h in range(HEADS):
            rows = slice(h * c_len, (h + 1) * c_len)
            o = _dot(sc[rows], v[sl, h * DV:(h + 1) * DV]) + o_inter[rows]
            o = o * lax.rsqrt(jnp.mean(o * o, axis=-1, keepdims=True) + LN_EPS)
            o_heads.append(o * gng[:, h * DV:(h + 1) * DV])
        o_chunks.append(jnp.concatenate(o_heads, axis=1))
    y = _dot(o_b.astype(BF16), wout_ref[BRANCH:2 * BRANCH, :])
    o_a = jnp.concatenate(o_chunks, axis=0) * _silu(g_a)
    y = y + _dot(o_a.astype(BF16), wout_ref[0:BRANCH, :])

    return _deepnorm(x, y, lng_ref[...], lnb_ref[...]), state


def _odd_kernel(run_ref, x_ref, pos_ref, w_ref, perm_ref, inv_ref, ctab_ref, stab_ref, dmat_ref, wint_ref,
                wst_ref, dec_ref, rng_ref, wp_ref, psc_ref, wout_ref, lng_ref, lnb_ref, o_ref, st_s, pad_s,
                cos_s, sin_s, wqk_s):
    tb = x_ref.shape[1]
    seq = pl.program_id(0)
    i = pl.program_id(1)

    @pl.when((seq == 0) & (i == 0))
    def _():
        wqk_s[...] = _dot(w_ref[:, 0:2 * QK], perm_ref[...]).astype(BF16)

    @pl.when(i == 0)
    def _():
        st_s[...] = jnp.zeros_like(st_s)
        pad_s[0:POOL_PAD, :] = jnp.zeros((POOL_PAD, BRANCH), F32)

    c_len = RET_CHUNK
    half = DK // 2
    chunks = [slice(c * c_len, (c + 1) * c_len) for c in range(tb // c_len)]
    c_q, c_k, c_v, c_gc, c_p, c_gd = [(o, o + n) for o, n in zip(
        (0, QK, 2 * QK, 2 * QK + BRANCH, 2 * QK + 2 * BRANCH, 2 * QK + 3 * BRANCH),
        (QK, QK, BRANCH, BRANCH, BRANCH, BRANCH))]

    consecutive = run_ref[seq, i] == 1

    @pl.when(consecutive)
    def _():
        a0 = pos_ref[0, :, 0:1].astype(F32) * inv_ref[...]
        c0 = jnp.cos(a0)
        s0 = jnp.sin(a0)
        cos_s[...] = c0 * ctab_ref[...] - s0 * stab_ref[...]
        sin_s[...] = s0 * ctab_ref[...] + c0 * stab_ref[...]

    @pl.when(jnp.logical_not(consecutive))
    def _():
        pos_rows = jnp.broadcast_to(pos_ref[0].astype(F32), (LANES, tb)).T
        ang = pos_rows * inv_ref[...]
        cos_s[...] = jnp.cos(ang)
        sin_s[...] = jnp.sin(ang)

    x = x_ref[0]
    xb = x.astype(BF16)

    q = _dot(xb, wqk_s[:, c_q[0]:c_q[1]]) * (DK ** -0.5)
    k = _dot(xb, wqk_s[:, c_k[0]:c_k[1]])
    cos = cos_s[...]
    sin = sin_s[...]
    v = _dot(xb, w_ref[:, c_v[0]:c_v[1]]).astype(BF16)
    p = _dot(xb, w_ref[:, c_p[0]:c_p[1]])
    q1, q2 = q[:, :LANES], q[:, LANES:]
    k1, k2 = k[:, :LANES], k[:, LANES:]
    qr = jnp.concatenate([q1 * cos - q2 * sin, q1 * sin + q2 * cos], axis=1)
    kr = jnp.concatenate([k1 * cos - k2 * sin, k1 * sin + k2 * cos], axis=1)
    kr_b = kr.astype(BF16)
    masks = [_head_mask(h, False) for h in range(HEADS)]
    qms = [jnp.concatenate([qr[sl] * masks[h] for h in range(HEADS)], axis=0).astype(BF16)
           for sl in chunks]
    g_c = _dot(xb, w_ref[:, c_gc[0]:c_gc[1]])

    scs = [(_dot_nt(qm, kr_b[sl]) * dmat_ref[...]).astype(BF16) for qm, sl in zip(qms, chunks)]
    pad_s[POOL_PAD:POOL_PAD + tb, :] = p
    t_top = i * tb + lax.broadcasted_iota(jnp.int32, (POOL_PAD, LANES), 0)
    pooled = []
    for g, win in enumerate(POOL_WINDOWS):
        cols = slice(g * LANES, (g + 1) * LANES)
        acc = pad_s[:, cols]
        shift = 1
        while shift < win:
            acc = acc + pltpu.roll(acc, shift, axis=0)
            shift *= 2
        cnt_top = jnp.minimum(t_top + 1, win).astype(F32)
        mean = jnp.concatenate([acc[POOL_PAD:2 * POOL_PAD] / cnt_top, acc[2 * POOL_PAD:] * (1.0 / win)],
                               axis=0)
        pooled.append((mean - p[:, cols]).astype(BF16))
    tail = pad_s[tb:tb + POOL_PAD, :]
    pad_s[0:POOL_PAD, :] = tail
    upds = []
    for sl in chunks:
        kt = (kr[sl] * wst_ref[...]).T.astype(BF16)
        upd = [_dot(jnp.concatenate([kt[h * half:(h + 1) * half],
                                     kt[LANES + h * half:LANES + (h + 1) * half]], axis=0),
                    v[sl, h * DV:(h + 1) * DV]) for h in range(HEADS)]
        upds.append(jnp.concatenate([u[:half] for u in upd] + [u[half:] for u in upd], axis=0))
    g_d = _dot(xb, w_ref[:, c_gd[0]:c_gd[1]])

    state = st_s[...]
    states = []
    for upd in upds:
        states.append(state.astype(BF16))
        state = dec_ref[...] * state + upd
    st_s[...] = state
    yd = jnp.concatenate([_dot(pooled[g], wp_ref[g]) for g in range(len(POOL_WINDOWS))], axis=1)
    o_inters = [_dot(qm, s_b) * wint_ref[...] for qm, s_b in zip(qms, states)]
    o_d = yd * psc_ref[...] * _silu(g_d)
    rng = rng_ref[...]
    o_chunks = []
    for sl, sc, o_inter in zip(chunks, scs, o_inters):
        o_heads = []
        for h in range(HEADS):
            rows = slice(h * c_len, (h + 1) * c_len)
            o = _dot(sc[rows], v[sl, h * DV:(h + 1) * DV]) + o_inter[rows]
            mu = jnp.mean(o, axis=-1, keepdims=True)
            oc = o - mu
            var = jnp.mean(oc * oc, axis=-1, keepdims=True)
            o_heads.append(oc * lax.rsqrt(var + LN_EPS) * rng[:, h * DV:(h + 1) * DV])
        o_chunks.append(jnp.concatenate(o_heads, axis=1))
    y = _dot(o_d.astype(BF16), wout_ref[BRANCH:2 * BRANCH, :])
    o_c = jnp.concatenate(o_chunks, axis=0) * _silu(g_c)
    y = y + _dot(o_c.astype(BF16), wout_ref[0:BRANCH, :])

    o_ref[0] = _deepnorm(x, y, lng_ref[...], lnb_ref[...])


def _full_spec(a):
    nd = a.ndim
    return pl.BlockSpec(a.shape, lambda *_, _nd=nd: (0,) * _nd)


def _compiler_params():
    return pltpu.CompilerParams(dimension_semantics=("arbitrary", "arbitrary"),
                                vmem_limit_bytes=VMEM_LIMIT_BYTES)


def _row(a):
    return a.reshape(1, -1).astype(F32)


def _scaled_out_proj(w_out):
    return (w_out * (1.0 / DN_ALPHA)).astype(BF16)


def _even_layer(x, w_in, w_a2, b_a, gla_norm_g, sgu_ln_g, sgu_ln_b, w_s, b_s, w_out, ln_g, ln_b):
    B, S, D = x.shape
    tb = TOKEN_BLOCK
    window = MXU_TILE + LANES
    w = jnp.pad(w_in.astype(BF16), ((0, 0), (0, EVEN_LR + 3 * BRANCH - MXU_TILE + window - EVEN_COLS)))
    shift = jnp.asarray(np.arange(window)[:, None] == np.arange(MXU_TILE)[None, :] + GLA_RANK, BF16)
    wa2 = jnp.pad(w_a2, ((0, LANES - GLA_RANK), (0, 0))).astype(BF16)
    bs = jnp.repeat(jnp.transpose(b_s), LANES, axis=1).astype(F32)
    params = (w, wa2, shift, _row(b_a), _row(gla_norm_g), _row(sgu_ln_g), _row(sgu_ln_b), w_s.astype(F32), bs,
              _scaled_out_proj(w_out), _row(ln_g), _row(ln_b))
    tok_spec = pl.BlockSpec((1, tb, D), lambda b, i: (b, i, 0))
    return pl.pallas_call(
        _even_kernel,
        out_shape=jax.ShapeDtypeStruct((B, S, D), F32),
        grid=(B, S // tb),
        in_specs=[tok_spec] + [_full_spec(p) for p in params],
        out_specs=tok_spec,
        scratch_shapes=[
            pltpu.VMEM((QK, DV), F32),
            pltpu.VMEM((D, QK), BF16),
            pltpu.VMEM((D, 3 * BRANCH), BF16),
        ],
        compiler_params=_compiler_params(),
        name="even_layer_gla_sgu",
    )(x, *params)


def _retention_tables():
    c = RET_CHUNK
    log_gamma = np.log(1.0 - 2.0 ** (-5.0 - np.arange(HEADS, dtype=np.float64)))
    idx = np.arange(c, dtype=np.float64)
    rel = idx[:, None] - idx[None, :]
    dmat = np.where(rel >= 0, np.exp(np.maximum(rel, 0.0)[None] * log_gamma[:, None, None]), 0.0)
    w_inter = np.exp((idx + 1.0)[None] * log_gamma[:, None])
    w_state = np.exp((c - 1.0 - idx)[None] * log_gamma[:, None])
    lane_head = (np.arange(QK) % LANES) // (DK // 2)
    dmat = dmat.reshape(HEADS * c, c)
    wint = np.broadcast_to(w_inter[:, :, None], (HEADS, c, DV)).reshape(HEADS * c, DV)
    wst = w_state[lane_head, :].T
    dec = np.broadcast_to(np.exp(c * log_gamma)[lane_head][:, None], (QK, DV))
    inv = ROPE_BASE ** (-np.arange(DK // 2, dtype=np.float64) / (DK // 2))
    inv = np.tile(inv, HEADS)[None, :].astype(np.float32)
    off = np.arange(TOKEN_BLOCK, dtype=np.float64)[:, None] * inv.astype(np.float64)
    f = lambda a: jnp.asarray(np.ascontiguousarray(a), F32)
    return f(inv), f(np.cos(off)), f(np.sin(off)), f(dmat), f(wint), f(wst), f(dec)


def _odd_layer(x, positions, w_in, ret_norm_g, w_pool, pool_scale, w_out, ln_g, ln_b):
    B, S, D = x.shape
    tb = TOKEN_BLOCK
    j = np.arange(2 * QK)
    src = (j // QK) * QK + ((j % LANES) // (DK // 2)) * DK + ((j % QK) // LANES) * (DK // 2) + j % (DK // 2)
    perm = jnp.asarray(src[None, :] == np.arange(2 * QK)[:, None], BF16)
    params = (w_in.astype(BF16), perm, *_retention_tables(), _row(ret_norm_g), w_pool.astype(BF16),
              _row(pool_scale), _scaled_out_proj(w_out), _row(ln_g), _row(ln_b))
    pos_blocks = positions.reshape(B, S // tb, tb)
    runs = jnp.all(pos_blocks == pos_blocks[:, :, :1] + jnp.arange(tb, dtype=positions.dtype), axis=-1)
    tok_spec = pl.BlockSpec((1, tb, D), lambda b, i: (b, i, 0))
    pos_spec = pl.BlockSpec((1, 1, tb), lambda b, i: (b, 0, i))
    return pl.pallas_call(
        _odd_kernel,
        out_shape=jax.ShapeDtypeStruct((B, S, D), F32),
        grid=(B, S // tb),
        in_specs=[pl.BlockSpec(memory_space=pltpu.SMEM), tok_spec, pos_spec] + [_full_spec(p) for p in params],
        out_specs=tok_spec,
        scratch_shapes=[
            pltpu.VMEM((QK, DV), F32),
            pltpu.VMEM((POOL_PAD + tb, BRANCH), F32),
            pltpu.VMEM((tb, LANES), F32),
            pltpu.VMEM((tb, LANES), F32),
            pltpu.VMEM((D, 2 * QK), BF16),
        ],
        compiler_params=_compiler_params(),
        name="odd_layer_retention_pool",
    )(runs.astype(jnp.int32), x, positions.reshape(B, 1, S), *params)


def kernel(x, positions, l0_w_in, l0_w_a2, l0_b_a, l0_gla_norm_g, l0_sgu_ln_g, l0_sgu_ln_b, l0_w_s, l0_b_s, l0_w_out, l0_ln_g, l0_ln_b, l1_w_in, l1_ret_norm_g, l1_w_pool, l1_pool_scale, l1_w_out, l1_ln_g, l1_ln_b):
    x = _even_layer(x, l0_w_in, l0_w_a2, l0_b_a, l0_gla_norm_g, l0_sgu_ln_g, l0_sgu_ln_b, l0_w_s, l0_b_s,
                    l0_w_out, l0_ln_g, l0_ln_b)
    return _odd_layer(x, positions, l1_w_in, l1_ret_norm_g, l1_w_pool, l1_pool_scale, l1_w_out, l1_ln_g, l1_ln_b)
```

```python
import functools

import jax
import jax.numpy as jnp
import numpy as np
from jax import lax
from jax.experimental import pallas as pl
from jax.experimental.pallas import tpu as pltpu

F32 = jnp.float32
BF16 = jnp.bfloat16

D_MODEL = 1024
BRANCH = 512
HEADS = 4
DK = 64
DV = 128
QK = HEADS * DK
GLA_RANK = 16
GLA_TAU = 16.0
GLA_CHUNK = 64
SGU_CHUNK = 128
RET_CHUNK = 128
ROPE_BASE = 10000.0
POOL_WINDOWS = (2, 4, 8, 16)
POOL_PAD = 16
DN_ALPHA = 4.0 ** 0.25
LN_EPS = 1e-5
LANES = 128
TOKEN_BLOCK = 1024
VMEM_LIMIT_BYTES = 56 * 1024 * 1024
EVEN_Q, EVEN_K, EVEN_V, EVEN_GA, EVEN_LR = 0, QK, 2 * QK, 2 * QK + BRANCH, 2 * QK + 2 * BRANCH
EVEN_U, EVEN_SV, EVEN_GB = 0, BRANCH, 2 * BRANCH
EVEN_COLS = EVEN_LR + GLA_RANK + 3 * BRANCH
MXU_TILE = 256


def _dot(a, b):
    return jnp.dot(a, b, preferred_element_type=F32)


def _dot_nt(a, b):
    return lax.dot_general(a, b, (((1,), (1,)), ((), ())), preferred_element_type=F32)


def _silu(x):
    hx = 0.5 * x
    return hx + hx * jnp.tanh(hx)


def _gelu_tanh(x):
    c = np.sqrt(2.0 / np.pi)
    hx = 0.5 * x
    return hx + hx * jnp.tanh(x * (np.float32(c) + np.float32(c * 0.044715) * (x * x)))


def _log_sigmoid(z):
    return jnp.minimum(z, 0.0) - jnp.log(1.0 + jnp.exp(-jnp.abs(z)))


def _layer_norm_rows(r, g, b, eps=LN_EPS):
    mu = jnp.mean(r, axis=-1, keepdims=True)
    c = r - mu
    var = jnp.mean(c * c, axis=-1, keepdims=True)
    return c * lax.rsqrt(var + eps) * g + b


def _deepnorm(x, y_scaled, g, b):
    return _layer_norm_rows(x + y_scaled, g, b, LN_EPS / DN_ALPHA ** 2)


def _head_mask(h, natural):
    lane = lax.broadcasted_iota(jnp.int32, (1, QK), 1)
    head = lane // DK if natural else (lane % LANES) // (DK // 2)
    return (head == h).astype(F32)


def _even_kernel(x_ref, w_ref, wa2_ref, shift_ref, ba_ref, gng_ref, slg_ref, slb_ref, ws_ref, bs_ref, wout_ref,
                 lng_ref, lnb_ref, o_ref, st_s, wz_s, wb_s):
    @pl.when((pl.program_id(0) == 0) & (pl.program_id(1) == 0))
    def _():
        wz_s[...] = _dot(w_ref[:, EVEN_LR:EVEN_LR + LANES], wa2_ref[...]).astype(BF16)
        window = shift_ref.shape[0]
        for t in range(3 * BRANCH // MXU_TILE):
            lo = EVEN_LR + t * MXU_TILE
            wb_s[:, t * MXU_TILE:(t + 1) * MXU_TILE] = _dot(w_ref[:, lo:lo + window], shift_ref[...]).astype(BF16)

    @pl.when(pl.program_id(1) == 0)
    def _():
        st_s[...] = jnp.zeros_like(st_s)

    refs = (w_ref, wb_s, wz_s, ba_ref, gng_ref, slg_ref, slb_ref, ws_ref, bs_ref, wout_ref, lng_ref, lnb_ref)
    o_ref[0], st_s[...] = _even_block(x_ref[0], st_s[...], *refs)


def _even_block(x, state, w_ref, wb_ref, wz_ref, ba_ref, gng_ref, slg_ref, slb_ref, ws_ref, bs_ref, wout_ref,
                lng_ref, lnb_ref):
    tb = x.shape[0]
    xb = x.astype(BF16)
    c_len = GLA_CHUNK
    t_len = SGU_CHUNK
    chunks = [slice(c * c_len, (c + 1) * c_len) for c in range(tb // c_len)]

    z = _dot(xb, wz_ref[...]) + ba_ref[...]
    q = _dot(xb, w_ref[:, EVEN_Q:EVEN_Q + QK])
    k = _dot(xb, w_ref[:, EVEN_K:EVEN_K + QK])
    la = _log_sigmoid(z) * (1.0 / GLA_TAU)
    v = _dot(xb, w_ref[:, EVEN_V:EVEN_V + BRANCH]).astype(BF16)

    row = lax.broadcasted_iota(jnp.int32, (LANES, LANES), 0)
    col = lax.broadcasted_iota(jnp.int32, (LANES, LANES), 1)
    tril2 = ((row >= col) & (row // c_len == col // c_len)).astype(BF16)
    la_hi = la.astype(BF16)
    la_lo = (la - la_hi.astype(F32)).astype(BF16)
    b = jnp.concatenate(
        [_dot(tril2, la_hi[n * LANES:(n + 1) * LANES]) + _dot(tril2, la_lo[n * LANES:(n + 1) * LANES])
         for n in range(tb // LANES)], axis=0)

    u = _dot(xb, wb_ref[:, EVEN_U:EVEN_U + BRANCH])
    q_dec = q * jnp.exp(b) * (DK ** -0.5)
    k_dec = k * jnp.exp(-b)
    masks = [_head_mask(h, True).astype(BF16) for h in range(HEADS)]
    q_dec_b = q_dec.astype(BF16)
    qms = [jnp.concatenate([q_dec_b[sl] * masks[h] for h in range(HEADS)], axis=0)
           for sl in chunks]
    sv = _dot(xb, wb_ref[:, EVEN_SV:EVEN_SV + BRANCH])

    u = _gelu_tanh(u)
    kts, upds, dcols = [], [], []
    for c, sl in enumerate(chunks):
        kt = k_dec[sl].T.astype(BF16)
        kts.append(kt)
        upds.append(jnp.concatenate([_dot(kt[h * DK:(h + 1) * DK], v[sl, h * DV:(h + 1) * DV])
                                     for h in range(HEADS)], axis=0))
        b_last = b[(c + 1) * c_len - 1:(c + 1) * c_len, :]
        dcols.append(jnp.exp(jnp.broadcast_to(b_last, (LANES, QK)).T))
    g_a = _dot(xb, w_ref[:, EVEN_GA:EVEN_GA + BRANCH])
    sv = _gelu_tanh(sv)
    slg = slg_ref[...]
    slb = slb_ref[...]
    svn = jnp.concatenate(
        [_layer_norm_rows(sv[:, g * LANES:(g + 1) * LANES], slg[:, g * LANES:(g + 1) * LANES],
                          slb[:, g * LANES:(g + 1) * LANES]) for g in range(BRANCH // LANES)],
        axis=1).astype(BF16)
    g_b = _dot(xb, wb_ref[:, EVEN_GB:EVEN_GB + BRANCH])

    states = []
    for upd, dcol in zip(upds, dcols):
        states.append(state.astype(BF16))
        state = dcol * (state + upd)

    trow = lax.broadcasted_iota(jnp.int32, (t_len, t_len), 0)
    tcol = lax.broadcasted_iota(jnp.int32, (t_len, t_len), 1)
    s_groups = []
    for g in range(BRANCH // LANES):
        w = jnp.where(trow >= tcol, ws_ref[g], 0.0).astype(BF16)
        cols = slice(g * LANES, (g + 1) * LANES)
        mixed = []
        for n in range(0, tb // t_len, 2):
            pair = _dot(w, jnp.concatenate([svn[n * t_len:(n + 1) * t_len, cols],
                                            svn[(n + 1) * t_len:(n + 2) * t_len, cols]], axis=1))
            mixed += [pair[:, :LANES], pair[:, LANES:]]
        s_groups.append(jnp.concatenate(mixed, axis=0))
    s = jnp.concatenate(s_groups, axis=1) + jnp.concatenate([bs_ref[...]] * (tb // t_len), axis=0)

    gng = gng_ref[...]
    crow = lax.broadcasted_iota(jnp.int32, (HEADS * c_len, c_len), 0) % c_len
    ccol = lax.broadcasted_iota(jnp.int32, (HEADS * c_len, c_len), 1)
    causal = crow >= ccol
    fill = jnp.zeros((QK, MXU_TILE - DV - c_len), BF16)
    fused = [_dot(qm, jnp.concatenate([s_b, kt, fill], axis=1)) for qm, s_b, kt in zip(qms, states, kts)]
    o_b = u * s * _silu(g_b)
    o_chunks = []
    for sl, f in zip(chunks, fused):
        o_inter = f[:, 0:DV]
        sc = jnp.where(causal, f[:, DV:DV + c_len], 0.0).astype(BF16)
        o_heads = []
        for h in range(HEADS):
            rows = slice(h * c_len, (h + 1) * c_len)
            o = _dot(sc[rows], v[sl, h * DV:(h + 1) * DV]) + o_inter[rows]
            o = o * lax.rsqrt(jnp.mean(o * o, axis=-1, keepdims=True) + LN_EPS)
            o_heads.append(o * gng[:, h * DV:(h + 1) * DV])
        o_chunks.append(jnp.concatenate(o_heads, axis=1))
    y = _dot(o_b.astype(BF16), wout_ref[BRANCH:2 * BRANCH, :])
    o_a = jnp.concatenate(o_chunks, axis=0) * _silu(g_a)
    y = y + _dot(o_a.astype(BF16), wout_ref[0:BRANCH, :])

    return _deepnorm(x, y, lng_ref[...], lnb_ref[...]), state


def _odd_kernel(run_ref, x_ref, pos_ref, w_ref, perm_ref, inv_ref, ctab_ref, stab_ref, dmat_ref, wint_ref,
                wst_ref, dec_ref, rng_ref, wp_ref, psc_ref, wout_ref, lng_ref, lnb_ref, o_ref, st_s, pad_s,
                cos_s, sin_s, wqk_s):
    tb = x_ref.shape[1]
    seq = pl.program_id(0)
    i = pl.program_id(1)

    @pl.when((seq == 0) & (i == 0))
    def _():
        wqk_s[...] = _dot(w_ref[:, 0:2 * QK], perm_ref[...]).astype(BF16)

    @pl.when(i == 0)
    def _():
        st_s[...] = jnp.zeros_like(st_s)
        pad_s[0:POOL_PAD, :] = jnp.zeros((POOL_PAD, BRANCH), F32)

    c_len = RET_CHUNK
    half = DK // 2
    chunks = [slice(c * c_len, (c + 1) * c_len) for c in range(tb // c_len)]
    c_q, c_k, c_v, c_gc, c_p, c_gd = [(o, o + n) for o, n in zip(
        (0, QK, 2 * QK, 2 * QK + BRANCH, 2 * QK + 2 * BRANCH, 2 * QK + 3 * BRANCH),
        (QK, QK, BRANCH, BRANCH, BRANCH, BRANCH))]

    consecutive = run_ref[seq, i] == 1

    @pl.when(consecutive)
    def _():
        a0 = pos_ref[0, :, 0:1].astype(F32) * inv_ref[...]
        c0 = jnp.cos(a0)
        s0 = jnp.sin(a0)
        cos_s[...] = c0 * ctab_ref[...] - s0 * stab_ref[...]
        sin_s[...] = s0 * ctab_ref[...] + c0 * stab_ref[...]

    @pl.when(jnp.logical_not(consecutive))
    def _():
        pos_rows = jnp.broadcast_to(pos_ref[0].astype(F32), (LANES, tb)).T
        ang = pos_rows * inv_ref[...]
        cos_s[...] = jnp.cos(ang)
        sin_s[...] = jnp.sin(ang)

    x = x_ref[0]
    xb = x.astype(BF16)

    q = _dot(xb, wqk_s[:, c_q[0]:c_q[1]]) * (DK ** -0.5)
    k = _dot(xb, wqk_s[:, c_k[0]:c_k[1]])
    cos = cos_s[...]
    sin = sin_s[...]
    v = _dot(xb, w_ref[:, c_v[0]:c_v[1]]).astype(BF16)
    p = _dot(xb, w_ref[:, c_p[0]:c_p[1]])
    q1, q2 = q[:, :LANES], q[:, LANES:]
    k1, k2 = k[:, :LANES], k[:, LANES:]
    qr = jnp.concatenate([q1 * cos - q2 * sin, q1 * sin + q2 * cos], axis=1)
    kr = jnp.concatenate([k1 * cos - k2 * sin, k1 * sin + k2 * cos], axis=1)
    masks = [_head_mask(h, False).astype(BF16) for h in range(HEADS)]
    qr_b = qr.astype(BF16)
    qms = [jnp.concatenate([qr_b[sl] * masks[h] for h in range(HEADS)], axis=0)
           for sl in chunks]
    g_c = _dot(xb, w_ref[:, c_gc[0]:c_gc[1]])

    pad_s[POOL_PAD:POOL_PAD + tb, :] = p
    t_top = i * tb + lax.broadcasted_iota(jnp.int32, (POOL_PAD, LANES), 0)
    pooled = []
    for g, win in enumerate(POOL_WINDOWS):
        cols = slice(g * LANES, (g + 1) * LANES)
        acc = pad_s[:, cols]
        shift = 1
        while shift < win:
            acc = acc + pltpu.roll(acc, shift, axis=0)
            shift *= 2
        cnt_top = jnp.minimum(t_top + 1, win).astype(F32)
        mean = jnp.concatenate([acc[POOL_PAD:2 * POOL_PAD] / cnt_top, acc[2 * POOL_PAD:] * (1.0 / win)],
                               axis=0)
        pooled.append((mean - p[:, cols]).astype(BF16))
    tail = pad_s[tb:tb + POOL_PAD, :]
    pad_s[0:POOL_PAD, :] = tail
    kts, upds = [], []
    for sl in chunks:
        kt = (kr[sl] * wst_ref[...]).T.astype(BF16)
        kts.append(kt)
        upd = [_dot(jnp.concatenate([kt[h * half:(h + 1) * half],
                                     kt[LANES + h * half:LANES + (h + 1) * half]], axis=0),
                    v[sl, h * DV:(h + 1) * DV]) for h in range(HEADS)]
        upds.append(jnp.concatenate([u[:half] for u in upd] + [u[half:] for u in upd], axis=0))
    g_d = _dot(xb, w_ref[:, c_gd[0]:c_gd[1]])

    state = st_s[...]
    states = []
    for upd in upds:
        states.append(state.astype(BF16))
        state = dec_ref[...] * state + upd
    st_s[...] = state
    yd = jnp.concatenate([_dot(jnp.concatenate([pooled[2 * n], pooled[2 * n + 1]], axis=1), wp_ref[n])
                          for n in range(len(POOL_WINDOWS) // 2)], axis=1)
    fused = [_dot(qm, jnp.concatenate([s_b, kt], axis=1)) for qm, s_b, kt in zip(qms, states, kts)]
    o_d = yd * psc_ref[...] * _silu(g_d)
    rng = rng_ref[...]
    o_chunks = []
    for sl, f in zip(chunks, fused):
        o_inter = f[:, 0:DV] * wint_ref[...]
        sc = (f[:, DV:DV + c_len] * dmat_ref[...]).astype(BF16)
        o_heads = []
        for h in range(HEADS):
            rows = slice(h * c_len, (h + 1) * c_len)
            o = _dot(sc[rows], v[sl, h * DV:(h + 1) * DV]) + o_inter[rows]
            mu = jnp.mean(o, axis=-1, keepdims=True)
            oc = o - mu
            var = jnp.mean(oc * oc, axis=-1, keepdims=True)
            o_heads.append(oc * lax.rsqrt(var + LN_EPS) * rng[:, h * DV:(h + 1) * DV])
        o_chunks.append(jnp.concatenate(o_heads, axis=1))
    y = _dot(o_d.astype(BF16), wout_ref[BRANCH:2 * BRANCH, :])
    o_c = jnp.concatenate(o_chunks, axis=0) * _silu(g_c)
    y = y + _dot(o_c.astype(BF16), wout_ref[0:BRANCH, :])

    o_ref[0] = _deepnorm(x, y, lng_ref[...], lnb_ref[...])


def _full_spec(a):
    nd = a.ndim
    return pl.BlockSpec(a.shape, lambda *_, _nd=nd: (0,) * _nd)


def _compiler_params():
    return pltpu.CompilerParams(dimension_semantics=("arbitrary", "arbitrary"),
                                vmem_limit_bytes=VMEM_LIMIT_BYTES)


def _row(a):
    return a.reshape(1, -1).astype(F32)


def _scaled_out_proj(w_out):
    return (w_out * (1.0 / DN_ALPHA)).astype(BF16)


def _even_layer(x, w_in, w_a2, b_a, gla_norm_g, sgu_ln_g, sgu_ln_b, w_s, b_s, w_out, ln_g, ln_b):
    B, S, D = x.shape
    tb = TOKEN_BLOCK
    window = MXU_TILE + LANES
    w = jnp.pad(w_in.astype(BF16), ((0, 0), (0, EVEN_LR + 3 * BRANCH - MXU_TILE + window - EVEN_COLS)))
    shift = jnp.asarray(np.arange(window)[:, None] == np.arange(MXU_TILE)[None, :] + GLA_RANK, BF16)
    wa2 = jnp.pad(w_a2, ((0, LANES - GLA_RANK), (0, 0))).astype(BF16)
    bs = jnp.repeat(jnp.transpose(b_s), LANES, axis=1).astype(F32)
    params = (w, wa2, shift, _row(b_a), _row(gla_norm_g), _row(sgu_ln_g), _row(sgu_ln_b), w_s.astype(F32), bs,
              _scaled_out_proj(w_out), _row(ln_g), _row(ln_b))
    tok_spec = pl.BlockSpec((1, tb, D), lambda b, i: (b, i, 0))
    return pl.pallas_call(
        _even_kernel,
        out_shape=jax.ShapeDtypeStruct((B, S, D), F32),
        grid=(B, S // tb),
        in_specs=[tok_spec] + [_full_spec(p) for p in params],
        out_specs=tok_spec,
        scratch_shapes=[
            pltpu.VMEM((QK, DV), F32),
            pltpu.VMEM((D, QK), BF16),
            pltpu.VMEM((D, 3 * BRANCH), BF16),
        ],
        compiler_params=_compiler_params(),
        name="even_layer_gla_sgu",
    )(x, *params)


def _retention_tables():
    c = RET_CHUNK
    log_gamma = np.log(1.0 - 2.0 ** (-5.0 - np.arange(HEADS, dtype=np.float64)))
    idx = np.arange(c, dtype=np.float64)
    rel = idx[:, None] - idx[None, :]
    dmat = np.where(rel >= 0, np.exp((idx[:, None] - (c - 1.0))[None] * log_gamma[:, None, None]), 0.0)
    w_inter = np.exp((idx + 1.0)[None] * log_gamma[:, None])
    w_state = np.exp((c - 1.0 - idx)[None] * log_gamma[:, None])
    lane_head = (np.arange(QK) % LANES) // (DK // 2)
    dmat = dmat.reshape(HEADS * c, c)
    wint = np.broadcast_to(w_inter[:, :, None], (HEADS, c, DV)).reshape(HEADS * c, DV)
    wst = w_state[lane_head, :].T
    dec = np.broadcast_to(np.exp(c * log_gamma)[lane_head][:, None], (QK, DV))
    inv = ROPE_BASE ** (-np.arange(DK // 2, dtype=np.float64) / (DK // 2))
    inv = np.tile(inv, HEADS)[None, :].astype(np.float32)
    off = np.arange(TOKEN_BLOCK, dtype=np.float64)[:, None] * inv.astype(np.float64)
    f = lambda a: jnp.asarray(np.ascontiguousarray(a), F32)
    return f(inv), f(np.cos(off)), f(np.sin(off)), f(dmat), f(wint), f(wst), f(dec)


def _odd_layer(x, positions, w_in, ret_norm_g, w_pool, pool_scale, w_out, ln_g, ln_b):
    B, S, D = x.shape
    tb = TOKEN_BLOCK
    j = np.arange(2 * QK)
    src = (j // QK) * QK + ((j % LANES) // (DK // 2)) * DK + ((j % QK) // LANES) * (DK // 2) + j % (DK // 2)
    perm = jnp.asarray(src[None, :] == np.arange(2 * QK)[:, None], BF16)
    zero = jnp.zeros_like(w_pool[0])
    wp = jnp.stack([jnp.block([[w_pool[2 * n], zero], [zero, w_pool[2 * n + 1]]])
                    for n in range(len(POOL_WINDOWS) // 2)]).astype(BF16)
    params = (w_in.astype(BF16), perm, *_retention_tables(), _row(ret_norm_g), wp,
              _row(pool_scale), _scaled_out_proj(w_out), _row(ln_g), _row(ln_b))
    pos_blocks = positions.reshape(B, S // tb, tb)
    runs = jnp.all(pos_blocks == pos_blocks[:, :, :1] + jnp.arange(tb, dtype=positions.dtype), axis=-1)
    tok_spec = pl.BlockSpec((1, tb, D), lambda b, i: (b, i, 0))
    pos_spec = pl.BlockSpec((1, 1, tb), lambda b, i: (b, 0, i))
    return pl.pallas_call(
        _odd_kernel,
        out_shape=jax.ShapeDtypeStruct((B, S, D), F32),
        grid=(B, S // tb),
        in_specs=[pl.BlockSpec(memory_space=pltpu.SMEM), tok_spec, pos_spec] + [_full_spec(p) for p in params],
        out_specs=tok_spec,
        scratch_shapes=[
            pltpu.VMEM((QK, DV), F32),
            pltpu.VMEM((POOL_PAD + tb, BRANCH), F32),
            pltpu.VMEM((tb, LANES), F32),
            pltpu.VMEM((tb, LANES), F32),
            pltpu.VMEM((D, 2 * QK), BF16),
        ],
        compiler_params=_compiler_params(),
        name="odd_layer_retention_pool",
    )(runs.astype(jnp.int32), x, positions.reshape(B, 1, S), *params)


def kernel(x, positions, l0_w_in, l0_w_a2, l0_b_a, l0_gla_norm_g, l0_sgu_ln_g, l0_sgu_ln_b, l0_w_s, l0_b_s, l0_w_out, l0_ln_g, l0_ln_b, l1_w_in, l1_ret_norm_g, l1_w_pool, l1_pool_scale, l1_w_out, l1_ln_g, l1_ln_b):
    x = _even_layer(x, l0_w_in, l0_w_a2, l0_b_a, l0_gla_norm_g, l0_sgu_ln_g, l0_sgu_ln_b, l0_w_s, l0_b_s,
                    l0_w_out, l0_ln_g, l0_ln_b)
    return _odd_layer(x, positions, l1_w_in, l1_ret_norm_g, l1_w_pool, l1_pool_scale, l1_w_out, l1_ln_g, l1_ln_b)
```

```python
import functools

import jax
import jax.numpy as jnp
import numpy as np
from jax import lax
from jax.experimental import pallas as pl
from jax.experimental.pallas import tpu as pltpu

F32 = jnp.float32
BF16 = jnp.bfloat16

D_MODEL = 1024
BRANCH = 512
HEADS = 4
DK = 64
DV = 128
QK = HEADS * DK
GLA_RANK = 16
GLA_TAU = 16.0
GLA_CHUNK = 64
SGU_CHUNK = 128
RET_CHUNK = 128
ROPE_BASE = 10000.0
POOL_WINDOWS = (2, 4, 8, 16)
POOL_PAD = 16
DN_ALPHA = 4.0 ** 0.25
LN_EPS = 1e-5
LANES = 128
TOKEN_BLOCK = 1024
NORM_ROW_GROUPS = 4
VMEM_LIMIT_BYTES = 56 * 1024 * 1024
EVEN_Q, EVEN_K, EVEN_V, EVEN_GA, EVEN_LR = 0, QK, 2 * QK, 2 * QK + BRANCH, 2 * QK + 2 * BRANCH
EVEN_U, EVEN_SV, EVEN_GB = 0, BRANCH, 2 * BRANCH
EVEN_COLS = EVEN_LR + GLA_RANK + 3 * BRANCH
MXU_TILE = 256


def _dot(a, b):
    return jnp.dot(a, b, preferred_element_type=F32)


def _dot_nt(a, b):
    return lax.dot_general(a, b, (((1,), (1,)), ((), ())), preferred_element_type=F32)


def _silu(x):
    hx = 0.5 * x
    return hx + hx * jnp.tanh(hx)


def _gelu_tanh(x):
    c = np.sqrt(2.0 / np.pi)
    hx = 0.5 * x
    return hx + hx * jnp.tanh(x * (np.float32(c) + np.float32(c * 0.044715) * (x * x)))


def _log_sigmoid(z):
    return jnp.minimum(z, 0.0) - jnp.log(1.0 + jnp.exp(-jnp.abs(z)))


def _layer_norm_rows(r, g, b, eps=LN_EPS):
    mu = jnp.mean(r, axis=-1, keepdims=True)
    c = r - mu
    var = jnp.mean(c * c, axis=-1, keepdims=True)
    return c * lax.rsqrt(var + eps) * g + b


def _deepnorm(x, y_scaled, g, b):
    return _layer_norm_rows(x + y_scaled, g, b, LN_EPS / DN_ALPHA ** 2)


def _project_and_norm(x, o_first, o_second, wout_ref, lng_ref, lnb_ref):
    rows_per_group = x.shape[0] // NORM_ROW_GROUPS
    outs = []
    for n in range(NORM_ROW_GROUPS):
        rows = slice(n * rows_per_group, (n + 1) * rows_per_group)
        y = (_dot(o_second[rows].astype(BF16), wout_ref[BRANCH:2 * BRANCH, :])
             + _dot(o_first[rows].astype(BF16), wout_ref[0:BRANCH, :]))
        outs.append(_deepnorm(x[rows], y, lng_ref[...], lnb_ref[...]))
    return jnp.concatenate(outs, axis=0)


def _head_mask(h, natural):
    lane = lax.broadcasted_iota(jnp.int32, (1, QK), 1)
    head = lane // DK if natural else (lane % LANES) // (DK // 2)
    return (head == h).astype(F32)


def _even_kernel(x_ref, w_ref, wa2_ref, shift_ref, ba_ref, gng_ref, slg_ref, slb_ref, ws_ref, bs_ref, wout_ref,
                 lng_ref, lnb_ref, o_ref, st_s, wz_s, wb_s):
    @pl.when((pl.program_id(0) == 0) & (pl.program_id(1) == 0))
    def _():
        wz_s[...] = _dot(w_ref[:, EVEN_LR:EVEN_LR + LANES], wa2_ref[...]).astype(BF16)
        window = shift_ref.shape[0]
        for t in range(3 * BRANCH // MXU_TILE):
            lo = EVEN_LR + t * MXU_TILE
            wb_s[:, t * MXU_TILE:(t + 1) * MXU_TILE] = _dot(w_ref[:, lo:lo + window], shift_ref[...]).astype(BF16)

    @pl.when(pl.program_id(1) == 0)
    def _():
        st_s[...] = jnp.zeros_like(st_s)

    refs = (w_ref, wb_s, wz_s, ba_ref, gng_ref, slg_ref, slb_ref, ws_ref, bs_ref, wout_ref, lng_ref, lnb_ref)
    o_ref[0], st_s[...] = _even_block(x_ref[0], st_s[...], *refs)


def _even_block(x, state, w_ref, wb_ref, wz_ref, ba_ref, gng_ref, slg_ref, slb_ref, ws_ref, bs_ref, wout_ref,
                lng_ref, lnb_ref):
    tb = x.shape[0]
    xb = x.astype(BF16)
    c_len = GLA_CHUNK
    t_len = SGU_CHUNK
    chunks = [slice(c * c_len, (c + 1) * c_len) for c in range(tb // c_len)]

    z = _dot(xb, wz_ref[...]) + ba_ref[...]
    q = _dot(xb, w_ref[:, EVEN_Q:EVEN_Q + QK])
    k = _dot(xb, w_ref[:, EVEN_K:EVEN_K + QK])
    la = _log_sigmoid(z) * (1.0 / GLA_TAU)
    v = _dot(xb, w_ref[:, EVEN_V:EVEN_V + BRANCH]).astype(BF16)

    row = lax.broadcasted_iota(jnp.int32, (LANES, 2 * LANES), 0)
    col = lax.broadcasted_iota(jnp.int32, (LANES, 2 * LANES), 1) % LANES
    tril2 = ((row >= col) & (row // c_len == col // c_len)).astype(BF16)
    la_hi = la.astype(BF16)
    la_lo = (la - la_hi.astype(F32)).astype(BF16)
    b = jnp.concatenate(
        [_dot(tril2, jnp.concatenate([la_hi[n * LANES:(n + 1) * LANES], la_lo[n * LANES:(n + 1) * LANES]],
                                     axis=0))
         for n in range(tb // LANES)], axis=0)

    u = _dot(xb, wb_ref[:, EVEN_U:EVEN_U + BRANCH])
    q_dec = q * jnp.exp(b) * (DK ** -0.5)
    k_dec = k * jnp.exp(-b)
    masks = [_head_mask(h, True).astype(BF16) for h in range(HEADS)]
    q_dec_b = q_dec.astype(BF16)
    qms = [jnp.concatenate([q_dec_b[sl] * masks[h] for h in range(HEADS)], axis=0)
           for sl in chunks]
    sv = _dot(xb, wb_ref[:, EVEN_SV:EVEN_SV + BRANCH])

    u = _gelu_tanh(u)
    kts, upds, dcols = [], [], []
    for c, sl in enumerate(chunks):
        kt = k_dec[sl].T.astype(BF16)
        kts.append(kt)
        upds.append(jnp.concatenate([_dot(kt[h * DK:(h + 1) * DK], v[sl, h * DV:(h + 1) * DV])
                                     for h in range(HEADS)], axis=0))
        b_last = b[(c + 1) * c_len - 1:(c + 1) * c_len, :]
        dcols.append(jnp.exp(jnp.broadcast_to(b_last, (LANES, QK)).T))
    g_a = _dot(xb, w_ref[:, EVEN_GA:EVEN_GA + BRANCH])
    sv = _gelu_tanh(sv)
    slg = slg_ref[...]
    slb = slb_ref[...]
    svn = jnp.concatenate(
        [_layer_norm_rows(sv[:, g * LANES:(g + 1) * LANES], slg[:, g * LANES:(g + 1) * LANES],
                          slb[:, g * LANES:(g + 1) * LANES]) for g in range(BRANCH // LANES)],
        axis=1).astype(BF16)
    g_b = _dot(xb, wb_ref[:, EVEN_GB:EVEN_GB + BRANCH])

    states = []
    for upd, dcol in zip(upds, dcols):
        states.append(state.astype(BF16))
        state = dcol * (state + upd)

    trow = lax.broadcasted_iota(jnp.int32, (t_len, t_len), 0)
    tcol = lax.broadcasted_iota(jnp.int32, (t_len, t_len), 1)
    s_groups = []
    for g in range(BRANCH // LANES):
        w = jnp.where(trow >= tcol, ws_ref[g], 0.0).astype(BF16)
        cols = slice(g * LANES, (g + 1) * LANES)
        mixed = []
        for n in range(0, tb // t_len, 2):
            pair = _dot(w, jnp.concatenate([svn[n * t_len:(n + 1) * t_len, cols],
                                            svn[(n + 1) * t_len:(n + 2) * t_len, cols]], axis=1))
            mixed += [pair[:, :LANES], pair[:, LANES:]]
        s_groups.append(jnp.concatenate(mixed, axis=0))
    s = jnp.concatenate(s_groups, axis=1) + jnp.concatenate([bs_ref[...]] * (tb // t_len), axis=0)

    gng = gng_ref[...]
    crow = lax.broadcasted_iota(jnp.int32, (HEADS * c_len, c_len), 0) % c_len
    ccol = lax.broadcasted_iota(jnp.int32, (HEADS * c_len, c_len), 1)
    causal = crow >= ccol
    fill = jnp.zeros((QK, MXU_TILE - DV - c_len), BF16)
    fused = [_dot(qm, jnp.concatenate([s_b, kt, fill], axis=1)) for qm, s_b, kt in zip(qms, states, kts)]
    o_b = u * s * _silu(g_b)
    o_chunks = []
    for sl, f in zip(chunks, fused):
        o_inter = f[:, 0:DV]
        sc = jnp.where(causal, f[:, DV:DV + c_len], 0.0).astype(BF16)
        o_heads = []
        for h in range(HEADS):
            rows = slice(h * c_len, (h + 1) * c_len)
            o = _dot(sc[rows], v[sl, h * DV:(h + 1) * DV]) + o_inter[rows]
            o = o * lax.rsqrt(jnp.mean(o * o, axis=-1, keepdims=True) + LN_EPS)
            o_heads.append(o * gng[:, h * DV:(h + 1) * DV])
        o_chunks.append(jnp.concatenate(o_heads, axis=1))
    o_a = jnp.concatenate(o_chunks, axis=0) * _silu(g_a)
    return _project_and_norm(x, o_a, o_b, wout_ref, lng_ref, lnb_ref), state


def _odd_kernel(run_ref, x_ref, pos_ref, w_ref, perm_ref, inv_ref, ctab_ref, stab_ref, dmat_ref, wint_ref,
                wst_ref, dec_ref, rng_ref, wp_ref, psc_ref, wout_ref, lng_ref, lnb_ref, o_ref, st_s, pad_s,
                cos_s, sin_s, wqk_s):
    tb = x_ref.shape[1]
    seq = pl.program_id(0)
    i = pl.program_id(1)

    @pl.when((seq == 0) & (i == 0))
    def _():
        wqk_s[...] = _dot(w_ref[:, 0:2 * QK], perm_ref[...]).astype(BF16)

    @pl.when(i == 0)
    def _():
        st_s[...] = jnp.zeros_like(st_s)
        pad_s[0:POOL_PAD, :] = jnp.zeros((POOL_PAD, BRANCH), F32)

    c_len = RET_CHUNK
    half = DK // 2
    chunks = [slice(c * c_len, (c + 1) * c_len) for c in range(tb // c_len)]
    c_q, c_k, c_v, c_gc, c_p, c_gd = [(o, o + n) for o, n in zip(
        (0, QK, 2 * QK, 2 * QK + BRANCH, 2 * QK + 2 * BRANCH, 2 * QK + 3 * BRANCH),
        (QK, QK, BRANCH, BRANCH, BRANCH, BRANCH))]

    consecutive = run_ref[seq, i] == 1

    @pl.when(consecutive)
    def _():
        a0 = pos_ref[0, :, 0:1].astype(F32) * inv_ref[...]
        c0 = jnp.cos(a0)
        s0 = jnp.sin(a0)
        cos_s[...] = c0 * ctab_ref[...] - s0 * stab_ref[...]
        sin_s[...] = s0 * ctab_ref[...] + c0 * stab_ref[...]

    @pl.when(jnp.logical_not(consecutive))
    def _():
        pos_rows = jnp.broadcast_to(pos_ref[0].astype(F32), (LANES, tb)).T
        ang = pos_rows * inv_ref[...]
        cos_s[...] = jnp.cos(ang)
        sin_s[...] = jnp.sin(ang)

    x = x_ref[0]
    xb = x.astype(BF16)

    q = _dot(xb, wqk_s[:, c_q[0]:c_q[1]]) * (DK ** -0.5)
    k = _dot(xb, wqk_s[:, c_k[0]:c_k[1]])
    cos = cos_s[...]
    sin = sin_s[...]
    v = _dot(xb, w_ref[:, c_v[0]:c_v[1]]).astype(BF16)
    p = _dot(xb, w_ref[:, c_p[0]:c_p[1]])
    q1, q2 = q[:, :LANES], q[:, LANES:]
    k1, k2 = k[:, :LANES], k[:, LANES:]
    qr = jnp.concatenate([q1 * cos - q2 * sin, q1 * sin + q2 * cos], axis=1)
    kr = jnp.concatenate([k1 * cos - k2 * sin, k1 * sin + k2 * cos], axis=1)
    masks = [_head_mask(h, False).astype(BF16) for h in range(HEADS)]
    qr_b = qr.astype(BF16)
    qms = [jnp.concatenate([qr_b[sl] * masks[h] for h in range(HEADS)], axis=0)
           for sl in chunks]
    g_c = _dot(xb, w_ref[:, c_gc[0]:c_gc[1]])

    pad_s[POOL_PAD:POOL_PAD + tb, :] = p
    t_top = i * tb + lax.broadcasted_iota(jnp.int32, (POOL_PAD, LANES), 0)
    pooled = []
    for g, win in enumerate(POOL_WINDOWS):
        cols = slice(g * LANES, (g + 1) * LANES)
        acc = pad_s[:, cols]
        shift = 1
        while shift < win:
            acc = acc + pltpu.roll(acc, shift, axis=0)
            shift *= 2
        cnt_top = jnp.minimum(t_top + 1, win).astype(F32)
        mean = jnp.concatenate([acc[POOL_PAD:2 * POOL_PAD] / cnt_top, acc[2 * POOL_PAD:] * (1.0 / win)],
                               axis=0)
        pooled.append((mean - p[:, cols]).astype(BF16))
    tail = pad_s[tb:tb + POOL_PAD, :]
    pad_s[0:POOL_PAD, :] = tail
    kts, upds = [], []
    for sl in chunks:
        kt = (kr[sl] * wst_ref[...]).T.astype(BF16)
        kts.append(kt)
        upd = [_dot(jnp.concatenate([kt[h * half:(h + 1) * half],
                                     kt[LANES + h * half:LANES + (h + 1) * half]], axis=0),
                    v[sl, h * DV:(h + 1) * DV]) for h in range(HEADS)]
        upds.append(jnp.concatenate([u[:half] for u in upd] + [u[half:] for u in upd], axis=0))
    g_d = _dot(xb, w_ref[:, c_gd[0]:c_gd[1]])

    state = st_s[...]
    states = []
    for upd in upds:
        states.append(state.astype(BF16))
        state = dec_ref[...] * state + upd
    st_s[...] = state
    yd = jnp.concatenate([_dot(jnp.concatenate([pooled[2 * n], pooled[2 * n + 1]], axis=1), wp_ref[n])
                          for n in range(len(POOL_WINDOWS) // 2)], axis=1)
    fused = [_dot(qm, jnp.concatenate([s_b, kt], axis=1)) for qm, s_b, kt in zip(qms, states, kts)]
    o_d = yd * psc_ref[...] * _silu(g_d)
    rng = rng_ref[...]
    o_chunks = []
    for sl, f in zip(chunks, fused):
        o_inter = f[:, 0:DV] * wint_ref[...]
        sc = (f[:, DV:DV + c_len] * dmat_ref[...]).astype(BF16)
        o_heads = []
        for h in range(HEADS):
            rows = slice(h * c_len, (h + 1) * c_len)
            o = _dot(sc[rows], v[sl, h * DV:(h + 1) * DV]) + o_inter[rows]
            mu = jnp.mean(o, axis=-1, keepdims=True)
            oc = o - mu
            var = jnp.mean(oc * oc, axis=-1, keepdims=True)
            o_heads.append(oc * lax.rsqrt(var + LN_EPS) * rng[:, h * DV:(h + 1) * DV])
        o_chunks.append(jnp.concatenate(o_heads, axis=1))
    o_c = jnp.concatenate(o_chunks, axis=0) * _silu(g_c)
    o_ref[0] = _project_and_norm(x, o_c, o_d, wout_ref, lng_ref, lnb_ref)


def _full_spec(a):
    nd = a.ndim
    return pl.BlockSpec(a.shape, lambda *_, _nd=nd: (0,) * _nd)


def _compiler_params():
    return pltpu.CompilerParams(dimension_semantics=("arbitrary", "arbitrary"),
                                vmem_limit_bytes=VMEM_LIMIT_BYTES)


def _row(a):
    return a.reshape(1, -1).astype(F32)


def _scaled_out_proj(w_out):
    return (w_out * (1.0 / DN_ALPHA)).astype(BF16)


def _even_layer(x, w_in, w_a2, b_a, gla_norm_g, sgu_ln_g, sgu_ln_b, w_s, b_s, w_out, ln_g, ln_b):
    B, S, D = x.shape
    tb = TOKEN_BLOCK
    window = MXU_TILE + LANES
    w = jnp.pad(w_in.astype(BF16), ((0, 0), (0, EVEN_LR + 3 * BRANCH - MXU_TILE + window - EVEN_COLS)))
    shift = jnp.asarray(np.arange(window)[:, None] == np.arange(MXU_TILE)[None, :] + GLA_RANK, BF16)
    wa2 = jnp.pad(w_a2, ((0, LANES - GLA_RANK), (0, 0))).astype(BF16)
    bs = jnp.repeat(jnp.transpose(b_s), LANES, axis=1).astype(F32)
    params = (w, wa2, shift, _row(b_a), _row(gla_norm_g), _row(sgu_ln_g), _row(sgu_ln_b), w_s.astype(F32), bs,
              _scaled_out_proj(w_out), _row(ln_g), _row(ln_b))
    tok_spec = pl.BlockSpec((1, tb, D), lambda b, i: (b, i, 0))
    return pl.pallas_call(
        _even_kernel,
        out_shape=jax.ShapeDtypeStruct((B, S, D), F32),
        grid=(B, S // tb),
        in_specs=[tok_spec] + [_full_spec(p) for p in params],
        out_specs=tok_spec,
        scratch_shapes=[
            pltpu.VMEM((QK, DV), F32),
            pltpu.VMEM((D, QK), BF16),
            pltpu.VMEM((D, 3 * BRANCH), BF16),
        ],
        compiler_params=_compiler_params(),
        name="even_layer_gla_sgu",
    )(x, *params)


def _retention_tables():
    c = RET_CHUNK
    log_gamma = np.log(1.0 - 2.0 ** (-5.0 - np.arange(HEADS, dtype=np.float64)))
    idx = np.arange(c, dtype=np.float64)
    rel = idx[:, None] - idx[None, :]
    dmat = np.where(rel >= 0, np.exp((idx[:, None] - (c - 1.0))[None] * log_gamma[:, None, None]), 0.0)
    w_inter = np.exp((idx + 1.0)[None] * log_gamma[:, None])
    w_state = np.exp((c - 1.0 - idx)[None] * log_gamma[:, None])
    lane_head = (np.arange(QK) % LANES) // (DK // 2)
    dmat = dmat.reshape(HEADS * c, c)
    wint = np.broadcast_to(w_inter[:, :, None], (HEADS, c, DV)).reshape(HEADS * c, DV)
    wst = w_state[lane_head, :].T
    dec = np.broadcast_to(np.exp(c * log_gamma)[lane_head][:, None], (QK, DV))
    inv = ROPE_BASE ** (-np.arange(DK // 2, dtype=np.float64) / (DK // 2))
    inv = np.tile(inv, HEADS)[None, :].astype(np.float32)
    off = np.arange(TOKEN_BLOCK, dtype=np.float64)[:, None] * inv.astype(np.float64)
    f = lambda a: jnp.asarray(np.ascontiguousarray(a), F32)
    return f(inv), f(np.cos(off)), f(np.sin(off)), f(dmat), f(wint), f(wst), f(dec)


def _odd_layer(x, positions, w_in, ret_norm_g, w_pool, pool_scale, w_out, ln_g, ln_b):
    B, S, D = x.shape
    tb = TOKEN_BLOCK
    j = np.arange(2 * QK)
    src = (j // QK) * QK + ((j % LANES) // (DK // 2)) * DK + ((j % QK) // LANES) * (DK // 2) + j % (DK // 2)
    perm = jnp.asarray(src[None, :] == np.arange(2 * QK)[:, None], BF16)
    zero = jnp.zeros_like(w_pool[0])
    wp = jnp.stack([jnp.block([[w_pool[2 * n], zero], [zero, w_pool[2 * n + 1]]])
                    for n in range(len(POOL_WINDOWS) // 2)]).astype(BF16)
    params = (w_in.astype(BF16), perm, *_retention_tables(), _row(ret_norm_g), wp,
              _row(pool_scale), _scaled_out_proj(w_out), _row(ln_g), _row(ln_b))
    pos_blocks = positions.reshape(B, S // tb, tb)
    runs = jnp.all(pos_blocks == pos_blocks[:, :, :1] + jnp.arange(tb, dtype=positions.dtype), axis=-1)
    tok_spec = pl.BlockSpec((1, tb, D), lambda b, i: (b, i, 0))
    pos_spec = pl.BlockSpec((1, 1, tb), lambda b, i: (b, 0, i))
    return pl.pallas_call(
        _odd_kernel,
        out_shape=jax.ShapeDtypeStruct((B, S, D), F32),
        grid=(B, S // tb),
        in_specs=[pl.BlockSpec(memory_space=pltpu.SMEM), tok_spec, pos_spec] + [_full_spec(p) for p in params],
        out_specs=tok_spec,
        scratch_shapes=[
            pltpu.VMEM((QK, DV), F32),
            pltpu.VMEM((POOL_PAD + tb, BRANCH), F32),
            pltpu.VMEM((tb, LANES), F32),
            pltpu.VMEM((tb, LANES), F32),
            pltpu.VMEM((D, 2 * QK), BF16),
        ],
        compiler_params=_compiler_params(),
        name="odd_layer_retention_pool",
    )(runs.astype(jnp.int32), x, positions.reshape(B, 1, S), *params)


def kernel(x, positions, l0_w_in, l0_w_a2, l0_b_a, l0_gla_norm_g, l0_sgu_ln_g, l0_sgu_ln_b, l0_w_s, l0_b_s, l0_w_out, l0_ln_g, l0_ln_b, l1_w_in, l1_ret_norm_g, l1_w_pool, l1_pool_scale, l1_w_out, l1_ln_g, l1_ln_b):
    x = _even_layer(x, l0_w_in, l0_w_a2, l0_b_a, l0_gla_norm_g, l0_sgu_ln_g, l0_sgu_ln_b, l0_w_s, l0_b_s,
                    l0_w_out, l0_ln_g, l0_ln_b)
    return _odd_layer(x, positions, l1_w_in, l1_ret_norm_g, l1_w_pool, l1_pool_scale, l1_w_out, l1_ln_g, l1_ln_b)
```

```python
import functools

import jax
import jax.numpy as jnp
import numpy as np
from jax import lax
from jax.experimental import pallas as pl
from jax.experimental.pallas import tpu as pltpu

F32 = jnp.float32
BF16 = jnp.bfloat16

D_MODEL = 1024
BRANCH = 512
HEADS = 4
DK = 64
DV = 128
QK = HEADS * DK
GLA_RANK = 16
GLA_TAU = 16.0
GLA_CHUNK = 64
SGU_CHUNK = 128
RET_CHUNK = 128
ROPE_BASE = 10000.0
POOL_WINDOWS = (2, 4, 8, 16)
POOL_PAD = 16
DN_ALPHA = 4.0 ** 0.25
LN_EPS = 1e-5
LANES = 128
TOKEN_BLOCK = 1024
STAGE_ROWS = 128
NORM_ROW_GROUPS = 4
VMEM_LIMIT_BYTES = 56 * 1024 * 1024
EVEN_Q, EVEN_K, EVEN_V, EVEN_GA, EVEN_LR = 0, QK, 2 * QK, 2 * QK + BRANCH, 2 * QK + 2 * BRANCH
EVEN_U, EVEN_SV, EVEN_GB = 0, BRANCH, 2 * BRANCH
EVEN_COLS = EVEN_LR + GLA_RANK + 3 * BRANCH
MXU_TILE = 256


def _dot(a, b):
    return jnp.dot(a, b, preferred_element_type=F32)


def _dot_nt(a, b):
    return lax.dot_general(a, b, (((1,), (1,)), ((), ())), preferred_element_type=F32)


def _silu(x):
    hx = 0.5 * x
    return hx + hx * jnp.tanh(hx)


def _gelu_tanh(x):
    c = np.sqrt(2.0 / np.pi)
    hx = 0.5 * x
    return hx + hx * jnp.tanh(x * (np.float32(c) + np.float32(c * 0.044715) * (x * x)))


def _log_sigmoid(z):
    return jnp.minimum(z, 0.0) - jnp.log(1.0 + jnp.exp(-jnp.abs(z)))


def _layer_norm_rows(r, g, b, eps=LN_EPS):
    mu = jnp.mean(r, axis=-1, keepdims=True)
    c = r - mu
    var = jnp.mean(c * c, axis=-1, keepdims=True)
    return c * lax.rsqrt(var + eps) * g + b


def _deepnorm(x, y_scaled, g, b):
    return _layer_norm_rows(x + y_scaled, g, b, LN_EPS / DN_ALPHA ** 2)


def _project_and_norm(x, o_first, o_second, wout_ref, lng_ref, lnb_ref):
    rows_per_group = x.shape[0] // NORM_ROW_GROUPS
    outs = []
    for n in range(NORM_ROW_GROUPS):
        rows = slice(n * rows_per_group, (n + 1) * rows_per_group)
        y = (_dot(o_second[rows].astype(BF16), wout_ref[BRANCH:2 * BRANCH, :])
             + _dot(o_first[rows].astype(BF16), wout_ref[0:BRANCH, :]))
        outs.append(_deepnorm(x[rows], y, lng_ref[...], lnb_ref[...]))
    return jnp.concatenate(outs, axis=0)


def _head_mask(h, natural):
    lane = lax.broadcasted_iota(jnp.int32, (1, QK), 1)
    head = lane // DK if natural else (lane % LANES) // (DK // 2)
    return (head == h).astype(F32)


def _load_as_bf16(src_hbm, dst_ref, stage_ref, sem, scale=None):
    rows, cols = src_hbm.shape
    n = rows // STAGE_ROWS

    def copy(k):
        return pltpu.make_async_copy(src_hbm.at[pl.ds(k * STAGE_ROWS, STAGE_ROWS), :], stage_ref.at[k % 2],
                                     sem.at[k % 2])

    copy(0).start()
    for k in range(n):
        if k + 1 < n:
            copy(k + 1).start()
        copy(k).wait()
        block = stage_ref[k % 2]
        if scale is not None:
            block = block * scale
        dst_ref[k * STAGE_ROWS:(k + 1) * STAGE_ROWS, :] = block.astype(BF16)


def _even_kernel(x_ref, w_hbm, wa2_ref, shift_ref, ba_ref, gng_ref, slg_ref, slb_ref, ws_ref, bs_ref, wout_hbm,
                 lng_ref, lnb_ref, o_ref, st_s, wz_s, wb_s, w_s, wout_s, wstage_s, ostage_s, sem):
    @pl.when((pl.program_id(0) == 0) & (pl.program_id(1) == 0))
    def _():
        _load_as_bf16(w_hbm, w_s, wstage_s, sem)
        _load_as_bf16(wout_hbm, wout_s, ostage_s, sem, 1.0 / DN_ALPHA)
        wz_s[...] = _dot(w_s[:, EVEN_LR:EVEN_LR + LANES], wa2_ref[...]).astype(BF16)
        window = shift_ref.shape[0]
        for t in range(3 * BRANCH // MXU_TILE):
            lo = EVEN_LR + t * MXU_TILE
            if lo + window <= EVEN_COLS:
                win = w_s[:, lo:lo + window]
            else:
                whole = (EVEN_COLS - lo) // LANES * LANES
                win = jnp.concatenate([w_s[:, lo:lo + whole], w_s[:, lo + whole:EVEN_COLS],
                                       jnp.zeros((D_MODEL, lo + window - EVEN_COLS), BF16)], axis=1)
            wb_s[:, t * MXU_TILE:(t + 1) * MXU_TILE] = _dot(win, shift_ref[...]).astype(BF16)

    @pl.when(pl.program_id(1) == 0)
    def _():
        st_s[...] = jnp.zeros_like(st_s)

    refs = (w_s, wb_s, wz_s, ba_ref, gng_ref, slg_ref, slb_ref, ws_ref, bs_ref, wout_s, lng_ref, lnb_ref)
    o_ref[0], st_s[...] = _even_block(x_ref[0], st_s[...], *refs)


def _even_block(x, state, w_ref, wb_ref, wz_ref, ba_ref, gng_ref, slg_ref, slb_ref, ws_ref, bs_ref, wout_ref,
                lng_ref, lnb_ref):
    tb = x.shape[0]
    xb = x.astype(BF16)
    c_len = GLA_CHUNK
    t_len = SGU_CHUNK
    chunks = [slice(c * c_len, (c + 1) * c_len) for c in range(tb // c_len)]

    z = _dot(xb, wz_ref[...]) + ba_ref[...]
    q = _dot(xb, w_ref[:, EVEN_Q:EVEN_Q + QK])
    k = _dot(xb, w_ref[:, EVEN_K:EVEN_K + QK])
    la = _log_sigmoid(z) * (1.0 / GLA_TAU)
    v = _dot(xb, w_ref[:, EVEN_V:EVEN_V + BRANCH]).astype(BF16)

    row = lax.broadcasted_iota(jnp.int32, (LANES, 2 * LANES), 0)
    col = lax.broadcasted_iota(jnp.int32, (LANES, 2 * LANES), 1) % LANES
    tril2 = ((row >= col) & (row // c_len == col // c_len)).astype(BF16)
    la_hi = la.astype(BF16)
    la_lo = (la - la_hi.astype(F32)).astype(BF16)
    b = jnp.concatenate(
        [_dot(tril2, jnp.concatenate([la_hi[n * LANES:(n + 1) * LANES], la_lo[n * LANES:(n + 1) * LANES]],
                                     axis=0))
         for n in range(tb // LANES)], axis=0)

    u = _dot(xb, wb_ref[:, EVEN_U:EVEN_U + BRANCH])
    q_dec = q * jnp.exp(b) * (DK ** -0.5)
    k_dec = k * jnp.exp(-b)
    masks = [_head_mask(h, True).astype(BF16) for h in range(HEADS)]
    q_dec_b = q_dec.astype(BF16)
    qms = [jnp.concatenate([q_dec_b[sl] * masks[h] for h in range(HEADS)], axis=0)
           for sl in chunks]
    sv = _dot(xb, wb_ref[:, EVEN_SV:EVEN_SV + BRANCH])

    u = _gelu_tanh(u)
    kts, upds, dcols = [], [], []
    for c, sl in enumerate(chunks):
        kt = k_dec[sl].T.astype(BF16)
        kts.append(kt)
        upds.append(jnp.concatenate([_dot(kt[h * DK:(h + 1) * DK], v[sl, h * DV:(h + 1) * DV])
                                     for h in range(HEADS)], axis=0))
        b_last = b[(c + 1) * c_len - 1:(c + 1) * c_len, :]
        dcols.append(jnp.exp(jnp.broadcast_to(b_last, (LANES, QK)).T))
    g_a = _dot(xb, w_ref[:, EVEN_GA:EVEN_GA + BRANCH])
    sv = _gelu_tanh(sv)
    slg = slg_ref[...]
    slb = slb_ref[...]
    svn = jnp.concatenate(
        [_layer_norm_rows(sv[:, g * LANES:(g + 1) * LANES], slg[:, g * LANES:(g + 1) * LANES],
                          slb[:, g * LANES:(g + 1) * LANES]) for g in range(BRANCH // LANES)],
        axis=1).astype(BF16)
    g_b = _dot(xb, wb_ref[:, EVEN_GB:EVEN_GB + BRANCH])

    states = []
    for upd, dcol in zip(upds, dcols):
        states.append(state.astype(BF16))
        state = dcol * (state + upd)

    trow = lax.broadcasted_iota(jnp.int32, (t_len, t_len), 0)
    tcol = lax.broadcasted_iota(jnp.int32, (t_len, t_len), 1)
    s_groups = []
    for g in range(BRANCH // LANES):
        w = jnp.where(trow >= tcol, ws_ref[g], 0.0).astype(BF16)
        cols = slice(g * LANES, (g + 1) * LANES)
        mixed = []
        for n in range(0, tb // t_len, 2):
            pair = _dot(w, jnp.concatenate([svn[n * t_len:(n + 1) * t_len, cols],
                                            svn[(n + 1) * t_len:(n + 2) * t_len, cols]], axis=1))
            mixed += [pair[:, :LANES], pair[:, LANES:]]
        s_groups.append(jnp.concatenate(mixed, axis=0))
    s = jnp.concatenate(s_groups, axis=1) + jnp.concatenate([bs_ref[...]] * (tb // t_len), axis=0)

    gng = gng_ref[...]
    crow = lax.broadcasted_iota(jnp.int32, (HEADS * c_len, c_len), 0) % c_len
    ccol = lax.broadcasted_iota(jnp.int32, (HEADS * c_len, c_len), 1)
    causal = crow >= ccol
    fill = jnp.zeros((QK, MXU_TILE - DV - c_len), BF16)
    fused = [_dot(qm, jnp.concatenate([s_b, kt, fill], axis=1)) for qm, s_b, kt in zip(qms, states, kts)]
    o_b = u * s * _silu(g_b)
    o_chunks = []
    for sl, f in zip(chunks, fused):
        o_inter = f[:, 0:DV]
        sc = jnp.where(causal, f[:, DV:DV + c_len], 0.0).astype(BF16)
        o_heads = []
        for h in range(HEADS):
            rows = slice(h * c_len, (h + 1) * c_len)
            o = _dot(sc[rows], v[sl, h * DV:(h + 1) * DV]) + o_inter[rows]
            o = o * lax.rsqrt(jnp.mean(o * o, axis=-1, keepdims=True) + LN_EPS)
            o_heads.append(o * gng[:, h * DV:(h + 1) * DV])
        o_chunks.append(jnp.concatenate(o_heads, axis=1))
    o_a = jnp.concatenate(o_chunks, axis=0) * _silu(g_a)
    return _project_and_norm(x, o_a, o_b, wout_ref, lng_ref, lnb_ref), state


def _odd_kernel(run_ref, x_ref, pos_ref, w_hbm, perm_ref, inv_ref, ctab_ref, stab_ref, dmat_ref, wint_ref,
                wst_ref, dec_ref, rng_ref, wp_ref, psc_ref, wout_hbm, lng_ref, lnb_ref, o_ref, st_s, pad_s,
                cos_s, sin_s, wqk_s, w_ref, wout_ref, wstage_s, ostage_s, sem):
    tb = x_ref.shape[1]
    seq = pl.program_id(0)
    i = pl.program_id(1)

    @pl.when((seq == 0) & (i == 0))
    def _():
        _load_as_bf16(w_hbm, w_ref, wstage_s, sem)
        _load_as_bf16(wout_hbm, wout_ref, ostage_s, sem, 1.0 / DN_ALPHA)
        wqk_s[...] = _dot(w_ref[:, 0:2 * QK], perm_ref[...]).astype(BF16)

    @pl.when(i == 0)
    def _():
        st_s[...] = jnp.zeros_like(st_s)
        pad_s[0:POOL_PAD, :] = jnp.zeros((POOL_PAD, BRANCH), F32)

    c_len = RET_CHUNK
    half = DK // 2
    chunks = [slice(c * c_len, (c + 1) * c_len) for c in range(tb // c_len)]
    c_q, c_k, c_v, c_gc, c_p, c_gd = [(o, o + n) for o, n in zip(
        (0, QK, 2 * QK, 2 * QK + BRANCH, 2 * QK + 2 * BRANCH, 2 * QK + 3 * BRANCH),
        (QK, QK, BRANCH, BRANCH, BRANCH, BRANCH))]

    consecutive = run_ref[seq, i] == 1

    @pl.when(consecutive)
    def _():
        a0 = pos_ref[0, :, 0:1].astype(F32) * inv_ref[...]
        c0 = jnp.cos(a0)
        s0 = jnp.sin(a0)
        cos_s[...] = c0 * ctab_ref[...] - s0 * stab_ref[...]
        sin_s[...] = s0 * ctab_ref[...] + c0 * stab_ref[...]

    @pl.when(jnp.logical_not(consecutive))
    def _():
        pos_rows = jnp.broadcast_to(pos_ref[0].astype(F32), (LANES, tb)).T
        ang = pos_rows * inv_ref[...]
        cos_s[...] = jnp.cos(ang)
        sin_s[...] = jnp.sin(ang)

    x = x_ref[0]
    xb = x.astype(BF16)

    q = _dot(xb, wqk_s[:, c_q[0]:c_q[1]]) * (DK ** -0.5)
    k = _dot(xb, wqk_s[:, c_k[0]:c_k[1]])
    cos = cos_s[...]
    sin = sin_s[...]
    v = _dot(xb, w_ref[:, c_v[0]:c_v[1]]).astype(BF16)
    p = _dot(xb, w_ref[:, c_p[0]:c_p[1]])
    q1, q2 = q[:, :LANES], q[:, LANES:]
    k1, k2 = k[:, :LANES], k[:, LANES:]
    qr = jnp.concatenate([q1 * cos - q2 * sin, q1 * sin + q2 * cos], axis=1)
    kr = jnp.concatenate([k1 * cos - k2 * sin, k1 * sin + k2 * cos], axis=1)
    masks = [_head_mask(h, False).astype(BF16) for h in range(HEADS)]
    qr_b = qr.astype(BF16)
    qms = [jnp.concatenate([qr_b[sl] * masks[h] for h in range(HEADS)], axis=0)
           for sl in chunks]
    g_c = _dot(xb, w_ref[:, c_gc[0]:c_gc[1]])

    pad_s[POOL_PAD:POOL_PAD + tb, :] = p
    t_top = i * tb + lax.broadcasted_iota(jnp.int32, (POOL_PAD, LANES), 0)
    pooled = []
    for g, win in enumerate(POOL_WINDOWS):
        cols = slice(g * LANES, (g + 1) * LANES)
        acc = pad_s[:, cols]
        shift = 1
        while shift < win:
            acc = acc + pltpu.roll(acc, shift, axis=0)
            shift *= 2
        cnt_top = jnp.minimum(t_top + 1, win).astype(F32)
        mean = jnp.concatenate([acc[POOL_PAD:2 * POOL_PAD] / cnt_top, acc[2 * POOL_PAD:] * (1.0 / win)],
                               axis=0)
        pooled.append((mean - p[:, cols]).astype(BF16))
    tail = pad_s[tb:tb + POOL_PAD, :]
    pad_s[0:POOL_PAD, :] = tail
    kts, upds = [], []
    for sl in chunks:
        kt = (kr[sl] * wst_ref[...]).T.astype(BF16)
        kts.append(kt)
        upd = [_dot(jnp.concatenate([kt[h * half:(h + 1) * half],
                                     kt[LANES + h * half:LANES + (h + 1) * half]], axis=0),
                    v[sl, h * DV:(h + 1) * DV]) for h in range(HEADS)]
        upds.append(jnp.concatenate([u[:half] for u in upd] + [u[half:] for u in upd], axis=0))
    g_d = _dot(xb, w_ref[:, c_gd[0]:c_gd[1]])

    state = st_s[...]
    states = []
    for upd in upds:
        states.append(state.astype(BF16))
        state = dec_ref[...] * state + upd
    st_s[...] = state
    yd = jnp.concatenate([_dot(jnp.concatenate([pooled[2 * n], pooled[2 * n + 1]], axis=1), wp_ref[n])
                          for n in range(len(POOL_WINDOWS) // 2)], axis=1)
    fused = [_dot(qm, jnp.concatenate([s_b, kt], axis=1)) for qm, s_b, kt in zip(qms, states, kts)]
    o_d = yd * psc_ref[...] * _silu(g_d)
    rng = rng_ref[...]
    o_chunks = []
    for sl, f in zip(chunks, fused):
        o_inter = f[:, 0:DV] * wint_ref[...]
        sc = (f[:, DV:DV + c_len] * dmat_ref[...]).astype(BF16)
        o_heads = []
        for h in range(HEADS):
            rows = slice(h * c_len, (h + 1) * c_len)
            o = _dot(sc[rows], v[sl, h * DV:(h + 1) * DV]) + o_inter[rows]
            mu = jnp.mean(o, axis=-1, keepdims=True)
            oc = o - mu
            var = jnp.mean(oc * oc, axis=-1, keepdims=True)
            o_heads.append(oc * lax.rsqrt(var + LN_EPS) * rng[:, h * DV:(h + 1) * DV])
        o_chunks.append(jnp.concatenate(o_heads, axis=1))
    o_c = jnp.concatenate(o_chunks, axis=0) * _silu(g_c)
    o_ref[0] = _project_and_norm(x, o_c, o_d, wout_ref, lng_ref, lnb_ref)


def _full_spec(a):
    nd = a.ndim
    return pl.BlockSpec(a.shape, lambda *_, _nd=nd: (0,) * _nd)


def _compiler_params():
    return pltpu.CompilerParams(dimension_semantics=("arbitrary", "arbitrary"),
                                vmem_limit_bytes=VMEM_LIMIT_BYTES)


def _row(a):
    return a.reshape(1, -1).astype(F32)


def _hbm_spec():
    return pl.BlockSpec(memory_space=pl.ANY)


def _even_layer(x, w_in, w_a2, b_a, gla_norm_g, sgu_ln_g, sgu_ln_b, w_s, b_s, w_out, ln_g, ln_b):
    B, S, D = x.shape
    tb = TOKEN_BLOCK
    window = MXU_TILE + LANES
    shift = jnp.asarray(np.arange(window)[:, None] == np.arange(MXU_TILE)[None, :] + GLA_RANK, BF16)
    wa2 = jnp.pad(w_a2, ((0, LANES - GLA_RANK), (0, 0))).astype(BF16)
    bs = jnp.repeat(jnp.transpose(b_s), LANES, axis=1).astype(F32)
    params = (w_in, wa2, shift, _row(b_a), _row(gla_norm_g), _row(sgu_ln_g), _row(sgu_ln_b), w_s.astype(F32), bs,
              w_out, _row(ln_g), _row(ln_b))
    in_hbm = (w_in, w_out)
    tok_spec = pl.BlockSpec((1, tb, D), lambda b, i: (b, i, 0))
    return pl.pallas_call(
        _even_kernel,
        out_shape=jax.ShapeDtypeStruct((B, S, D), F32),
        grid=(B, S // tb),
        in_specs=[tok_spec] + [_hbm_spec() if any(p is h for h in in_hbm) else _full_spec(p) for p in params],
        out_specs=tok_spec,
        scratch_shapes=[
            pltpu.VMEM((QK, DV), F32),
            pltpu.VMEM((D, QK), BF16),
            pltpu.VMEM((D, 3 * BRANCH), BF16),
            pltpu.VMEM(w_in.shape, BF16),
            pltpu.VMEM(w_out.shape, BF16),
            pltpu.VMEM((2, STAGE_ROWS, w_in.shape[1]), F32),
            pltpu.VMEM((2, STAGE_ROWS, w_out.shape[1]), F32),
            pltpu.SemaphoreType.DMA((2,)),
        ],
        compiler_params=_compiler_params(),
        name="even_layer_gla_sgu",
    )(x, *params)


def _retention_tables():
    c = RET_CHUNK
    log_gamma = np.log(1.0 - 2.0 ** (-5.0 - np.arange(HEADS, dtype=np.float64)))
    idx = np.arange(c, dtype=np.float64)
    rel = idx[:, None] - idx[None, :]
    dmat = np.where(rel >= 0, np.exp((idx[:, None] - (c - 1.0))[None] * log_gamma[:, None, None]), 0.0)
    w_inter = np.exp((idx + 1.0)[None] * log_gamma[:, None])
    w_state = np.exp((c - 1.0 - idx)[None] * log_gamma[:, None])
    lane_head = (np.arange(QK) % LANES) // (DK // 2)
    dmat = dmat.reshape(HEADS * c, c)
    wint = np.broadcast_to(w_inter[:, :, None], (HEADS, c, DV)).reshape(HEADS * c, DV)
    wst = w_state[lane_head, :].T
    dec = np.broadcast_to(np.exp(c * log_gamma)[lane_head][:, None], (QK, DV))
    inv = ROPE_BASE ** (-np.arange(DK // 2, dtype=np.float64) / (DK // 2))
    inv = np.tile(inv, HEADS)[None, :].astype(np.float32)
    off = np.arange(TOKEN_BLOCK, dtype=np.float64)[:, None] * inv.astype(np.float64)
    f = lambda a: jnp.asarray(np.ascontiguousarray(a), F32)
    return f(inv), f(np.cos(off)), f(np.sin(off)), f(dmat), f(wint), f(wst), f(dec)


def _odd_layer(x, positions, w_in, ret_norm_g, w_pool, pool_scale, w_out, ln_g, ln_b):
    B, S, D = x.shape
    tb = TOKEN_BLOCK
    j = np.arange(2 * QK)
    src = (j // QK) * QK + ((j % LANES) // (DK // 2)) * DK + ((j % QK) // LANES) * (DK // 2) + j % (DK // 2)
    perm = jnp.asarray(src[None, :] == np.arange(2 * QK)[:, None], BF16)
    zero = jnp.zeros_like(w_pool[0])
    wp = jnp.stack([jnp.block([[w_pool[2 * n], zero], [zero, w_pool[2 * n + 1]]])
                    for n in range(len(POOL_WINDOWS) // 2)]).astype(BF16)
    params = (w_in, perm, *_retention_tables(), _row(ret_norm_g), wp, _row(pool_scale), w_out, _row(ln_g),
              _row(ln_b))
    in_hbm = (w_in, w_out)
    pos_blocks = positions.reshape(B, S // tb, tb)
    runs = jnp.all(pos_blocks == pos_blocks[:, :, :1] + jnp.arange(tb, dtype=positions.dtype), axis=-1)
    tok_spec = pl.BlockSpec((1, tb, D), lambda b, i: (b, i, 0))
    pos_spec = pl.BlockSpec((1, 1, tb), lambda b, i: (b, 0, i))
    return pl.pallas_call(
        _odd_kernel,
        out_shape=jax.ShapeDtypeStruct((B, S, D), F32),
        grid=(B, S // tb),
        in_specs=[pl.BlockSpec(memory_space=pltpu.SMEM), tok_spec, pos_spec]
                 + [_hbm_spec() if any(p is h for h in in_hbm) else _full_spec(p) for p in params],
        out_specs=tok_spec,
        scratch_shapes=[
            pltpu.VMEM((QK, DV), F32),
            pltpu.VMEM((POOL_PAD + tb, BRANCH), F32),
            pltpu.VMEM((tb, LANES), F32),
            pltpu.VMEM((tb, LANES), F32),
            pltpu.VMEM((D, 2 * QK), BF16),
            pltpu.VMEM(w_in.shape, BF16),
            pltpu.VMEM(w_out.shape, BF16),
            pltpu.VMEM((2, STAGE_ROWS, w_in.shape[1]), F32),
            pltpu.VMEM((2, STAGE_ROWS, w_out.shape[1]), F32),
            pltpu.SemaphoreType.DMA((2,)),
        ],
        compiler_params=_compiler_params(),
        name="odd_layer_retention_pool",
    )(runs.astype(jnp.int32), x, positions.reshape(B, 1, S), *params)


def kernel(x, positions, l0_w_in, l0_w_a2, l0_b_a, l0_gla_norm_g, l0_sgu_ln_g, l0_sgu_ln_b, l0_w_s, l0_b_s, l0_w_out, l0_ln_g, l0_ln_b, l1_w_in, l1_ret_norm_g, l1_w_pool, l1_pool_scale, l1_w_out, l1_ln_g, l1_ln_b):
    x = _even_layer(x, l0_w_in, l0_w_a2, l0_b_a, l0_gla_norm_g, l0_sgu_ln_g, l0_sgu_ln_b, l0_w_s, l0_b_s,
                    l0_w_out, l0_ln_g, l0_ln_b)
    return _odd_layer(x, positions, l1_w_in, l1_ret_norm_g, l1_w_pool, l1_pool_scale, l1_w_out, l1_ln_g, l1_ln_b)
```

```python
import functools

import jax
import jax.numpy as jnp
import numpy as np
from jax import lax
from jax.experimental import pallas as pl
from jax.experimental.pallas import tpu as pltpu

F32 = jnp.float32
BF16 = jnp.bfloat16

D_MODEL = 1024
BRANCH = 512
HEADS = 4
DK = 64
DV = 128
QK = HEADS * DK
GLA_RANK = 16
GLA_TAU = 16.0
GLA_CHUNK = 64
SGU_CHUNK = 128
RET_CHUNK = 128
ROPE_BASE = 10000.0
POOL_WINDOWS = (2, 4, 8, 16)
POOL_PAD = 16
DN_ALPHA = 4.0 ** 0.25
LN_EPS = 1e-5
LANES = 128
TOKEN_BLOCK = 1024
NORM_ROW_GROUPS = 4
VMEM_LIMIT_BYTES = 56 * 1024 * 1024
EVEN_Q, EVEN_K, EVEN_V, EVEN_GA, EVEN_LR = 0, QK, 2 * QK, 2 * QK + BRANCH, 2 * QK + 2 * BRANCH
EVEN_U, EVEN_SV, EVEN_GB = EVEN_LR + GLA_RANK, EVEN_LR + GLA_RANK + BRANCH, EVEN_LR + GLA_RANK + 2 * BRANCH
MXU_TILE = 256
STAGE_BYTES = 2 * 1024 * 1024


def _dot(a, b):
    return jnp.dot(a, b, preferred_element_type=F32)


def _dot_nt(a, b):
    return lax.dot_general(a, b, (((1,), (1,)), ((), ())), preferred_element_type=F32)


def _silu(x):
    hx = 0.5 * x
    return hx + hx * jnp.tanh(hx)


def _gelu_tanh(x):
    c = np.sqrt(2.0 / np.pi)
    hx = 0.5 * x
    return hx + hx * jnp.tanh(x * (np.float32(c) + np.float32(c * 0.044715) * (x * x)))


def _log_sigmoid(z):
    return jnp.minimum(z, 0.0) - jnp.log(1.0 + jnp.exp(-jnp.abs(z)))


def _layer_norm_rows(r, g, b, eps=LN_EPS):
    mu = jnp.mean(r, axis=-1, keepdims=True)
    c = r - mu
    var = jnp.mean(c * c, axis=-1, keepdims=True)
    return c * lax.rsqrt(var + eps) * g + b


def _deepnorm(x, y_scaled, g, b):
    return _layer_norm_rows(x + y_scaled, g, b, LN_EPS / DN_ALPHA ** 2)


def _project_and_norm(x, o_first, o_second, wout_ref, lng_ref, lnb_ref):
    rows_per_group = x.shape[0] // NORM_ROW_GROUPS
    outs = []
    for n in range(NORM_ROW_GROUPS):
        rows = slice(n * rows_per_group, (n + 1) * rows_per_group)
        y = (_dot(o_second[rows].astype(BF16), wout_ref[BRANCH:2 * BRANCH, :])
             + _dot(o_first[rows].astype(BF16), wout_ref[0:BRANCH, :]))
        outs.append(_deepnorm(x[rows], y, lng_ref[...], lnb_ref[...]))
    return jnp.concatenate(outs, axis=0)


def _head_mask(h, natural):
    lane = lax.broadcasted_iota(jnp.int32, (1, QK), 1)
    head = lane // DK if natural else (lane % LANES) // (DK // 2)
    return (head == h).astype(F32)


def _load_as_bf16(src_hbm, dst_ref, stage_ref, sem, scale=None):
    rows = src_hbm.shape[0]
    per_copy = stage_ref.shape[1]
    starts = list(range(0, rows, per_copy))

    def copy(k):
        n_rows = min(per_copy, rows - starts[k])
        return pltpu.make_async_copy(src_hbm.at[pl.ds(starts[k], n_rows), :],
                                     stage_ref.at[k % 2, pl.ds(0, n_rows), :], sem.at[k % 2])

    copy(0).start()
    for k, start in enumerate(starts):
        if k + 1 < len(starts):
            copy(k + 1).start()
        copy(k).wait()
        n_rows = min(per_copy, rows - start)
        block = stage_ref[k % 2, 0:n_rows, :]
        if scale is not None:
            block = block * scale
        dst_ref[start:start + n_rows, :] = block.astype(BF16)


def _even_kernel(x_ref, wt_hbm, wa2t_ref, ba_ref, gng_ref, slg_ref, slb_ref, ws_ref, bs_ref, wout_hbm,
                 lng_ref, lnb_ref, o_ref, st_s, wzt_s, wt_s, wout_s, wstage_s, ostage_s, sem):
    @pl.when((pl.program_id(0) == 0) & (pl.program_id(1) == 0))
    def _():
        _load_as_bf16(wt_hbm, wt_s, wstage_s, sem)
        _load_as_bf16(wout_hbm, wout_s, ostage_s, sem, 1.0 / DN_ALPHA)
        wzt_s[...] = _dot(wa2t_ref[...], wt_s[EVEN_LR:EVEN_LR + LANES, :]).astype(BF16)

    @pl.when(pl.program_id(1) == 0)
    def _():
        st_s[...] = jnp.zeros_like(st_s)

    refs = (wt_s, wzt_s, ba_ref, gng_ref, slg_ref, slb_ref, ws_ref, bs_ref, wout_s, lng_ref, lnb_ref)
    o_ref[0], st_s[...] = _even_block(x_ref[0], st_s[...], *refs)


def _even_block(x, state, wt_ref, wzt_ref, ba_ref, gng_ref, slg_ref, slb_ref, ws_ref, bs_ref, wout_ref, lng_ref,
                lnb_ref):
    tb = x.shape[0]
    xb = x.astype(BF16)
    c_len = GLA_CHUNK
    t_len = SGU_CHUNK
    chunks = [slice(c * c_len, (c + 1) * c_len) for c in range(tb // c_len)]

    z = _dot_nt(xb, wzt_ref[...]) + ba_ref[...]
    q = _dot_nt(xb, wt_ref[EVEN_Q:EVEN_Q + QK, :])
    k = _dot_nt(xb, wt_ref[EVEN_K:EVEN_K + QK, :])
    la = _log_sigmoid(z) * (1.0 / GLA_TAU)
    v = _dot_nt(xb, wt_ref[EVEN_V:EVEN_V + BRANCH, :]).astype(BF16)

    row = lax.broadcasted_iota(jnp.int32, (LANES, 2 * LANES), 0)
    col = lax.broadcasted_iota(jnp.int32, (LANES, 2 * LANES), 1) % LANES
    tril2 = ((row >= col) & (row // c_len == col // c_len)).astype(BF16)
    la_hi = la.astype(BF16)
    la_lo = (la - la_hi.astype(F32)).astype(BF16)
    b = jnp.concatenate(
        [_dot(tril2, jnp.concatenate([la_hi[n * LANES:(n + 1) * LANES], la_lo[n * LANES:(n + 1) * LANES]],
                                     axis=0))
         for n in range(tb // LANES)], axis=0)

    u = _dot_nt(xb, wt_ref[EVEN_U:EVEN_U + BRANCH, :])
    q_dec = q * jnp.exp(b) * (DK ** -0.5)
    k_dec = k * jnp.exp(-b)
    masks = [_head_mask(h, True).astype(BF16) for h in range(HEADS)]
    q_dec_b = q_dec.astype(BF16)
    qms = [jnp.concatenate([q_dec_b[sl] * masks[h] for h in range(HEADS)], axis=0)
           for sl in chunks]
    sv = _dot_nt(xb, wt_ref[EVEN_SV:EVEN_SV + BRANCH, :])

    u = _gelu_tanh(u)
    kts, upds, dcols = [], [], []
    for c, sl in enumerate(chunks):
        kt = k_dec[sl].T.astype(BF16)
        kts.append(kt)
        upds.append(jnp.concatenate([_dot(kt[h * DK:(h + 1) * DK], v[sl, h * DV:(h + 1) * DV])
                                     for h in range(HEADS)], axis=0))
        b_last = b[(c + 1) * c_len - 1:(c + 1) * c_len, :]
        dcols.append(jnp.exp(jnp.broadcast_to(b_last, (LANES, QK)).T))
    g_a = _dot_nt(xb, wt_ref[EVEN_GA:EVEN_GA + BRANCH, :])
    sv = _gelu_tanh(sv)
    slg = slg_ref[...]
    slb = slb_ref[...]
    svn = jnp.concatenate(
        [_layer_norm_rows(sv[:, g * LANES:(g + 1) * LANES], slg[:, g * LANES:(g + 1) * LANES],
                          slb[:, g * LANES:(g + 1) * LANES]) for g in range(BRANCH // LANES)],
        axis=1).astype(BF16)
    g_b = _dot_nt(xb, wt_ref[EVEN_GB:EVEN_GB + BRANCH, :])

    states = []
    for upd, dcol in zip(upds, dcols):
        states.append(state.astype(BF16))
        state = dcol * (state + upd)

    trow = lax.broadcasted_iota(jnp.int32, (t_len, t_len), 0)
    tcol = lax.broadcasted_iota(jnp.int32, (t_len, t_len), 1)
    s_groups = []
    for g in range(BRANCH // LANES):
        w = jnp.where(trow >= tcol, ws_ref[g], 0.0).astype(BF16)
        cols = slice(g * LANES, (g + 1) * LANES)
        mixed = []
        for n in range(0, tb // t_len, 2):
            pair = _dot(w, jnp.concatenate([svn[n * t_len:(n + 1) * t_len, cols],
                                            svn[(n + 1) * t_len:(n + 2) * t_len, cols]], axis=1))
            mixed += [pair[:, :LANES], pair[:, LANES:]]
        s_groups.append(jnp.concatenate(mixed, axis=0))
    s = jnp.concatenate(s_groups, axis=1) + jnp.concatenate([bs_ref[...]] * (tb // t_len), axis=0)

    gng = gng_ref[...]
    crow = lax.broadcasted_iota(jnp.int32, (HEADS * c_len, c_len), 0) % c_len
    ccol = lax.broadcasted_iota(jnp.int32, (HEADS * c_len, c_len), 1)
    causal = crow >= ccol
    fill = jnp.zeros((QK, MXU_TILE - DV - c_len), BF16)
    fused = [_dot(qm, jnp.concatenate([s_b, kt, fill], axis=1)) for qm, s_b, kt in zip(qms, states, kts)]
    o_b = u * s * _silu(g_b)
    o_chunks = []
    for sl, f in zip(chunks, fused):
        o_inter = f[:, 0:DV]
        sc = jnp.where(causal, f[:, DV:DV + c_len], 0.0).astype(BF16)
        o_heads = []
        for h in range(HEADS):
            rows = slice(h * c_len, (h + 1) * c_len)
            o = _dot(sc[rows], v[sl, h * DV:(h + 1) * DV]) + o_inter[rows]
            o = o * lax.rsqrt(jnp.mean(o * o, axis=-1, keepdims=True) + LN_EPS)
            o_heads.append(o * gng[:, h * DV:(h + 1) * DV])
        o_chunks.append(jnp.concatenate(o_heads, axis=1))
    o_a = jnp.concatenate(o_chunks, axis=0) * _silu(g_a)
    return _project_and_norm(x, o_a, o_b, wout_ref, lng_ref, lnb_ref), state


def _odd_kernel(run_ref, x_ref, pos_ref, w_hbm, perm_ref, inv_ref, ctab_ref, stab_ref, dmat_ref, wint_ref,
                wst_ref, dec_ref, rng_ref, wp_ref, psc_ref, wout_hbm, lng_ref, lnb_ref, o_ref, st_s, pad_s,
                cos_s, sin_s, wqk_s, w_ref, wout_ref, wstage_s, ostage_s, sem):
    tb = x_ref.shape[1]
    seq = pl.program_id(0)
    i = pl.program_id(1)

    @pl.when((seq == 0) & (i == 0))
    def _():
        _load_as_bf16(w_hbm, w_ref, wstage_s, sem)
        _load_as_bf16(wout_hbm, wout_ref, ostage_s, sem, 1.0 / DN_ALPHA)
        wqk_s[...] = _dot(w_ref[:, 0:2 * QK], perm_ref[...]).astype(BF16)

    @pl.when(i == 0)
    def _():
        st_s[...] = jnp.zeros_like(st_s)
        pad_s[0:POOL_PAD, :] = jnp.zeros((POOL_PAD, BRANCH), F32)

    c_len = RET_CHUNK
    half = DK // 2
    chunks = [slice(c * c_len, (c + 1) * c_len) for c in range(tb // c_len)]
    c_q, c_k, c_v, c_gc, c_p, c_gd = [(o, o + n) for o, n in zip(
        (0, QK, 2 * QK, 2 * QK + BRANCH, 2 * QK + 2 * BRANCH, 2 * QK + 3 * BRANCH),
        (QK, QK, BRANCH, BRANCH, BRANCH, BRANCH))]

    consecutive = run_ref[seq, i] == 1

    @pl.when(consecutive)
    def _():
        a0 = pos_ref[0, :, 0:1].astype(F32) * inv_ref[...]
        c0 = jnp.cos(a0)
        s0 = jnp.sin(a0)
        cos_s[...] = c0 * ctab_ref[...] - s0 * stab_ref[...]
        sin_s[...] = s0 * ctab_ref[...] + c0 * stab_ref[...]

    @pl.when(jnp.logical_not(consecutive))
    def _():
        pos_rows = jnp.broadcast_to(pos_ref[0].astype(F32), (LANES, tb)).T
        ang = pos_rows * inv_ref[...]
        cos_s[...] = jnp.cos(ang)
        sin_s[...] = jnp.sin(ang)

    x = x_ref[0]
    xb = x.astype(BF16)

    q = _dot(xb, wqk_s[:, c_q[0]:c_q[1]]) * (DK ** -0.5)
    k = _dot(xb, wqk_s[:, c_k[0]:c_k[1]])
    cos = cos_s[...]
    sin = sin_s[...]
    v = _dot(xb, w_ref[:, c_v[0]:c_v[1]]).astype(BF16)
    p = _dot(xb, w_ref[:, c_p[0]:c_p[1]])
    q1, q2 = q[:, :LANES], q[:, LANES:]
    k1, k2 = k[:, :LANES], k[:, LANES:]
    qr = jnp.concatenate([q1 * cos - q2 * sin, q1 * sin + q2 * cos], axis=1)
    kr = jnp.concatenate([k1 * cos - k2 * sin, k1 * sin + k2 * cos], axis=1)
    masks = [_head_mask(h, False).astype(BF16) for h in range(HEADS)]
    qr_b = qr.astype(BF16)
    qms = [jnp.concatenate([qr_b[sl] * masks[h] for h in range(HEADS)], axis=0)
           for sl in chunks]
    g_c = _dot(xb, w_ref[:, c_gc[0]:c_gc[1]])

    pad_s[POOL_PAD:POOL_PAD + tb, :] = p
    t_top = i * tb + lax.broadcasted_iota(jnp.int32, (POOL_PAD, LANES), 0)
    pooled = []
    for g, win in enumerate(POOL_WINDOWS):
        cols = slice(g * LANES, (g + 1) * LANES)
        acc = pad_s[:, cols]
        shift = 1
        while shift < win:
            acc = acc + pltpu.roll(acc, shift, axis=0)
            shift *= 2
        cnt_top = jnp.minimum(t_top + 1, win).astype(F32)
        mean = jnp.concatenate([acc[POOL_PAD:2 * POOL_PAD] / cnt_top, acc[2 * POOL_PAD:] * (1.0 / win)],
                               axis=0)
        pooled.append((mean - p[:, cols]).astype(BF16))
    tail = pad_s[tb:tb + POOL_PAD, :]
    pad_s[0:POOL_PAD, :] = tail
    kts, upds = [], []
    for sl in chunks:
        kt = (kr[sl] * wst_ref[...]).T.astype(BF16)
        kts.append(kt)
        upd = [_dot(jnp.concatenate([kt[h * half:(h + 1) * half],
                                     kt[LANES + h * half:LANES + (h + 1) * half]], axis=0),
                    v[sl, h * DV:(h + 1) * DV]) for h in range(HEADS)]
        upds.append(jnp.concatenate([u[:half] for u in upd] + [u[half:] for u in upd], axis=0))
    g_d = _dot(xb, w_ref[:, c_gd[0]:c_gd[1]])

    state = st_s[...]
    states = []
    for upd in upds:
        states.append(state.astype(BF16))
        state = dec_ref[...] * state + upd
    st_s[...] = state
    yd = jnp.concatenate([_dot(jnp.concatenate([pooled[2 * n], pooled[2 * n + 1]], axis=1), wp_ref[n])
                          for n in range(len(POOL_WINDOWS) // 2)], axis=1)
    fused = [_dot(qm, jnp.concatenate([s_b, kt], axis=1)) for qm, s_b, kt in zip(qms, states, kts)]
    o_d = yd * psc_ref[...] * _silu(g_d)
    rng = rng_ref[...]
    o_chunks = []
    for sl, f in zip(chunks, fused):
        o_inter = f[:, 0:DV] * wint_ref[...]
        sc = (f[:, DV:DV + c_len] * dmat_ref[...]).astype(BF16)
        o_heads = []
        for h in range(HEADS):
            rows = slice(h * c_len, (h + 1) * c_len)
            o = _dot(sc[rows], v[sl, h * DV:(h + 1) * DV]) + o_inter[rows]
            mu = jnp.mean(o, axis=-1, keepdims=True)
            oc = o - mu
            var = jnp.mean(oc * oc, axis=-1, keepdims=True)
            o_heads.append(oc * lax.rsqrt(var + LN_EPS) * rng[:, h * DV:(h + 1) * DV])
        o_chunks.append(jnp.concatenate(o_heads, axis=1))
    o_c = jnp.concatenate(o_chunks, axis=0) * _silu(g_c)
    o_ref[0] = _project_and_norm(x, o_c, o_d, wout_ref, lng_ref, lnb_ref)


def _full_spec(a):
    nd = a.ndim
    return pl.BlockSpec(a.shape, lambda *_, _nd=nd: (0,) * _nd)


def _compiler_params():
    return pltpu.CompilerParams(dimension_semantics=("arbitrary", "arbitrary"),
                                vmem_limit_bytes=VMEM_LIMIT_BYTES)


def _row(a):
    return a.reshape(1, -1).astype(F32)


def _hbm_spec():
    return pl.BlockSpec(memory_space=pl.ANY)


def _stage(w):
    rows = min(w.shape[0], STAGE_BYTES // (4 * w.shape[1]) // LANES * LANES)
    return pltpu.VMEM((2, rows, w.shape[1]), F32)


def _even_layer(x, w_in, w_a2, b_a, gla_norm_g, sgu_ln_g, sgu_ln_b, w_s, b_s, w_out, ln_g, ln_b):
    B, S, D = x.shape
    tb = TOKEN_BLOCK
    w_in_t = w_in.T
    wa2t = jnp.pad(w_a2.T, ((0, 0), (0, LANES - GLA_RANK))).astype(BF16)
    bs = jnp.repeat(jnp.transpose(b_s), LANES, axis=1).astype(F32)
    params = (w_in_t, wa2t, _row(b_a), _row(gla_norm_g), _row(sgu_ln_g), _row(sgu_ln_b), w_s.astype(F32), bs,
              w_out, _row(ln_g), _row(ln_b))
    in_hbm = (w_in_t, w_out)
    tok_spec = pl.BlockSpec((1, tb, D), lambda b, i: (b, i, 0))
    return pl.pallas_call(
        _even_kernel,
        out_shape=jax.ShapeDtypeStruct((B, S, D), F32),
        grid=(B, S // tb),
        in_specs=[tok_spec] + [_hbm_spec() if any(p is h for h in in_hbm) else _full_spec(p) for p in params],
        out_specs=tok_spec,
        scratch_shapes=[
            pltpu.VMEM((QK, DV), F32),
            pltpu.VMEM((QK, D), BF16),
            pltpu.VMEM(w_in_t.shape, BF16),
            pltpu.VMEM(w_out.shape, BF16),
            _stage(w_in_t),
            _stage(w_out),
            pltpu.SemaphoreType.DMA((2,)),
        ],
        compiler_params=_compiler_params(),
        name="even_layer_gla_sgu",
    )(x, *params)


def _retention_tables():
    c = RET_CHUNK
    log_gamma = np.log(1.0 - 2.0 ** (-5.0 - np.arange(HEADS, dtype=np.float64)))
    idx = np.arange(c, dtype=np.float64)
    rel = idx[:, None] - idx[None, :]
    dmat = np.where(rel >= 0, np.exp((idx[:, None] - (c - 1.0))[None] * log_gamma[:, None, None]), 0.0)
    w_inter = np.exp((idx + 1.0)[None] * log_gamma[:, None])
    w_state = np.exp((c - 1.0 - idx)[None] * log_gamma[:, None])
    lane_head = (np.arange(QK) % LANES) // (DK // 2)
    dmat = dmat.reshape(HEADS * c, c)
    wint = np.broadcast_to(w_inter[:, :, None], (HEADS, c, DV)).reshape(HEADS * c, DV)
    wst = w_state[lane_head, :].T
    dec = np.broadcast_to(np.exp(c * log_gamma)[lane_head][:, None], (QK, DV))
    inv = ROPE_BASE ** (-np.arange(DK // 2, dtype=np.float64) / (DK // 2))
    inv = np.tile(inv, HEADS)[None, :].astype(np.float32)
    off = np.arange(TOKEN_BLOCK, dtype=np.float64)[:, None] * inv.astype(np.float64)
    f = lambda a: jnp.asarray(np.ascontiguousarray(a), F32)
    return f(inv), f(np.cos(off)), f(np.sin(off)), f(dmat), f(wint), f(wst), f(dec)


def _odd_layer(x, positions, w_in, ret_norm_g, w_pool, pool_scale, w_out, ln_g, ln_b):
    B, S, D = x.shape
    tb = TOKEN_BLOCK
    j = np.arange(2 * QK)
    src = (j // QK) * QK + ((j % LANES) // (DK // 2)) * DK + ((j % QK) // LANES) * (DK // 2) + j % (DK // 2)
    perm = jnp.asarray(src[None, :] == np.arange(2 * QK)[:, None], BF16)
    zero = jnp.zeros_like(w_pool[0])
    wp = jnp.stack([jnp.block([[w_pool[2 * n], zero], [zero, w_pool[2 * n + 1]]])
                    for n in range(len(POOL_WINDOWS) // 2)]).astype(BF16)
    params = (w_in, perm, *_retention_tables(), _row(ret_norm_g), wp, _row(pool_scale), w_out, _row(ln_g),
              _row(ln_b))
    in_hbm = (w_in, w_out)
    pos_blocks = positions.reshape(B, S // tb, tb)
    runs = jnp.all(pos_blocks == pos_blocks[:, :, :1] + jnp.arange(tb, dtype=positions.dtype), axis=-1)
    tok_spec = pl.BlockSpec((1, tb, D), lambda b, i: (b, i, 0))
    pos_spec = pl.BlockSpec((1, 1, tb), lambda b, i: (b, 0, i))
    return pl.pallas_call(
        _odd_kernel,
        out_shape=jax.ShapeDtypeStruct((B, S, D), F32),
        grid=(B, S // tb),
        in_specs=[pl.BlockSpec(memory_space=pltpu.SMEM), tok_spec, pos_spec]
                 + [_hbm_spec() if any(p is h for h in in_hbm) else _full_spec(p) for p in params],
        out_specs=tok_spec,
        scratch_shapes=[
            pltpu.VMEM((QK, DV), F32),
            pltpu.VMEM((POOL_PAD + tb, BRANCH), F32),
            pltpu.VMEM((tb, LANES), F32),
            pltpu.VMEM((tb, LANES), F32),
            pltpu.VMEM((D, 2 * QK), BF16),
            pltpu.VMEM(w_in.shape, BF16),
            pltpu.VMEM(w_out.shape, BF16),
            _stage(w_in),
            _stage(w_out),
            pltpu.SemaphoreType.DMA((2,)),
        ],
        compiler_params=_compiler_params(),
        name="odd_layer_retention_pool",
    )(runs.astype(jnp.int32), x, positions.reshape(B, 1, S), *params)


def kernel(x, positions, l0_w_in, l0_w_a2, l0_b_a, l0_gla_norm_g, l0_sgu_ln_g, l0_sgu_ln_b, l0_w_s, l0_b_s, l0_w_out, l0_ln_g, l0_ln_b, l1_w_in, l1_ret_norm_g, l1_w_pool, l1_pool_scale, l1_w_out, l1_ln_g, l1_ln_b):
    x = _even_layer(x, l0_w_in, l0_w_a2, l0_b_a, l0_gla_norm_g, l0_sgu_ln_g, l0_sgu_ln_b, l0_w_s, l0_b_s,
                    l0_w_out, l0_ln_g, l0_ln_b)
    return _odd_layer(x, positions, l1_w_in, l1_ret_norm_g, l1_w_pool, l1_pool_scale, l1_w_out, l1_ln_g, l1_ln_b)
```

```python
import functools

import jax
import jax.numpy as jnp
import numpy as np
from jax import lax
from jax.experimental import pallas as pl
from jax.experimental.pallas import tpu as pltpu

F32 = jnp.float32
BF16 = jnp.bfloat16

D_MODEL = 1024
BRANCH = 512
HEADS = 4
DK = 64
DV = 128
QK = HEADS * DK
GLA_RANK = 16
GLA_TAU = 16.0
GLA_CHUNK = 64
SGU_CHUNK = 128
RET_CHUNK = 128
ROPE_BASE = 10000.0
POOL_WINDOWS = (2, 4, 8, 16)
POOL_PAD = 16
DN_ALPHA = 4.0 ** 0.25
LN_EPS = 1e-5
LANES = 128
TOKEN_BLOCK = 1024
NORM_ROW_GROUPS = 4
VMEM_LIMIT_BYTES = 56 * 1024 * 1024
EVEN_Q, EVEN_K, EVEN_V, EVEN_GA, EVEN_LR = 0, QK, 2 * QK, 2 * QK + BRANCH, 2 * QK + 2 * BRANCH
EVEN_U, EVEN_SV, EVEN_GB = EVEN_LR + GLA_RANK, EVEN_LR + GLA_RANK + BRANCH, EVEN_LR + GLA_RANK + 2 * BRANCH
MXU_TILE = 256
STAGE_BYTES = 2 * 1024 * 1024


def _dot(a, b):
    return jnp.dot(a, b, preferred_element_type=F32)


def _dot_nt(a, b):
    return lax.dot_general(a, b, (((1,), (1,)), ((), ())), preferred_element_type=F32)


def _silu(x):
    hx = 0.5 * x
    return hx + hx * jnp.tanh(hx)


def _gelu_tanh(x):
    c = np.sqrt(2.0 / np.pi)
    hx = 0.5 * x
    return hx + hx * jnp.tanh(x * (np.float32(c) + np.float32(c * 0.044715) * (x * x)))


def _log_sigmoid(z):
    return jnp.minimum(z, 0.0) - jnp.log(1.0 + jnp.exp(-jnp.abs(z)))


def _layer_norm_rows(r, g, b, eps=LN_EPS):
    mu = jnp.mean(r, axis=-1, keepdims=True)
    c = r - mu
    var = jnp.mean(c * c, axis=-1, keepdims=True)
    return c * lax.rsqrt(var + eps) * g + b


def _deepnorm(x, y_scaled, g, b):
    return _layer_norm_rows(x + y_scaled, g, b, LN_EPS / DN_ALPHA ** 2)


def _project_and_norm(x, o_first, o_second, wout_ref, lng_ref, lnb_ref):
    rows_per_group = x.shape[0] // NORM_ROW_GROUPS
    outs = []
    for n in range(NORM_ROW_GROUPS):
        rows = slice(n * rows_per_group, (n + 1) * rows_per_group)
        y = (_dot(o_second[rows].astype(BF16), wout_ref[BRANCH:2 * BRANCH, :])
             + _dot(o_first[rows].astype(BF16), wout_ref[0:BRANCH, :]))
        outs.append(_deepnorm(x[rows], y, lng_ref[...], lnb_ref[...]))
    return jnp.concatenate(outs, axis=0)


def _head_mask(h, natural):
    lane = lax.broadcasted_iota(jnp.int32, (1, QK), 1)
    head = lane // DK if natural else (lane % LANES) // (DK // 2)
    return (head == h).astype(F32)


def _load_as_bf16(src_hbm, dst_ref, stage_ref, sem, scale=None):
    rows = src_hbm.shape[0]
    per_copy = stage_ref.shape[1]
    starts = list(range(0, rows, per_copy))

    def copy(k):
        n_rows = min(per_copy, rows - starts[k])
        return pltpu.make_async_copy(src_hbm.at[pl.ds(starts[k], n_rows), :],
                                     stage_ref.at[k % 2, pl.ds(0, n_rows), :], sem.at[k % 2])

    copy(0).start()
    for k, start in enumerate(starts):
        if k + 1 < len(starts):
            copy(k + 1).start()
        copy(k).wait()
        n_rows = min(per_copy, rows - start)
        block = stage_ref[k % 2, 0:n_rows, :]
        if scale is not None:
            block = block * scale
        dst_ref[start:start + n_rows, :] = block.astype(BF16)


def _even_kernel(x_ref, wt_hbm, wa2t_ref, ba_ref, gng_ref, slg_ref, slb_ref, ws_ref, bs_ref, wout_hbm,
                 lng_ref, lnb_ref, o_ref, st_s, wzt_s, wt_s, wout_s, wstage_s, ostage_s, sem):
    @pl.when((pl.program_id(0) == 0) & (pl.program_id(1) == 0))
    def _():
        _load_as_bf16(wt_hbm, wt_s, wstage_s, sem)
        _load_as_bf16(wout_hbm, wout_s, ostage_s, sem, 1.0 / DN_ALPHA)
        wzt_s[...] = _dot(wa2t_ref[...], wt_s[EVEN_LR:EVEN_LR + LANES, :]).astype(BF16)

    @pl.when(pl.program_id(1) == 0)
    def _():
        st_s[...] = jnp.zeros_like(st_s)

    refs = (wt_s, wzt_s, ba_ref, gng_ref, slg_ref, slb_ref, ws_ref, bs_ref, wout_s, lng_ref, lnb_ref)
    o_ref[0], st_s[...] = _even_block(x_ref[0], st_s[...], *refs)


def _even_block(x, state, wt_ref, wzt_ref, ba_ref, gng_ref, slg_ref, slb_ref, ws_ref, bs_ref, wout_ref, lng_ref,
                lnb_ref):
    tb = x.shape[0]
    xb = x.astype(BF16)
    c_len = GLA_CHUNK
    t_len = SGU_CHUNK
    chunks = [slice(c * c_len, (c + 1) * c_len) for c in range(tb // c_len)]

    z = _dot_nt(xb, wzt_ref[...]) + ba_ref[...]
    q = _dot_nt(xb, wt_ref[EVEN_Q:EVEN_Q + QK, :])
    k = _dot_nt(xb, wt_ref[EVEN_K:EVEN_K + QK, :])
    la = _log_sigmoid(z) * (1.0 / GLA_TAU)
    v = _dot_nt(xb, wt_ref[EVEN_V:EVEN_V + BRANCH, :]).astype(BF16)

    row = lax.broadcasted_iota(jnp.int32, (LANES, 2 * LANES), 0)
    col = lax.broadcasted_iota(jnp.int32, (LANES, 2 * LANES), 1) % LANES
    tril2 = ((row >= col) & (row // c_len == col // c_len)).astype(BF16)
    la_hi = la.astype(BF16)
    la_lo = (la - la_hi.astype(F32)).astype(BF16)
    b = jnp.concatenate(
        [_dot(tril2, jnp.concatenate([la_hi[n * LANES:(n + 1) * LANES], la_lo[n * LANES:(n + 1) * LANES]],
                                     axis=0))
         for n in range(tb // LANES)], axis=0)

    u = _dot_nt(xb, wt_ref[EVEN_U:EVEN_U + BRANCH, :])
    q_dec = q * jnp.exp(b) * (DK ** -0.5)
    k_dec = k * jnp.exp(-b)
    masks = [_head_mask(h, True).astype(BF16) for h in range(HEADS)]
    q_dec_b = q_dec.astype(BF16)
    qms = [jnp.concatenate([q_dec_b[sl] * masks[h] for h in range(HEADS)], axis=0)
           for sl in chunks]
    sv = _dot_nt(xb, wt_ref[EVEN_SV:EVEN_SV + BRANCH, :])

    u = _gelu_tanh(u)
    kts, upds, dcols = [], [], []
    for c, sl in enumerate(chunks):
        kt = k_dec[sl].astype(BF16).T
        kts.append(kt)
        upds.append(jnp.concatenate([_dot(kt[h * DK:(h + 1) * DK], v[sl, h * DV:(h + 1) * DV])
                                     for h in range(HEADS)], axis=0))
        b_last = b[(c + 1) * c_len - 1:(c + 1) * c_len, :]
        dcols.append(jnp.exp(jnp.broadcast_to(b_last, (LANES, QK)).T))
    g_a = _dot_nt(xb, wt_ref[EVEN_GA:EVEN_GA + BRANCH, :])
    sv = _gelu_tanh(sv)
    slg = slg_ref[...]
    slb = slb_ref[...]
    svn = jnp.concatenate(
        [_layer_norm_rows(sv[:, g * LANES:(g + 1) * LANES], slg[:, g * LANES:(g + 1) * LANES],
                          slb[:, g * LANES:(g + 1) * LANES]) for g in range(BRANCH // LANES)],
        axis=1).astype(BF16)
    g_b = _dot_nt(xb, wt_ref[EVEN_GB:EVEN_GB + BRANCH, :])

    states = []
    for upd, dcol in zip(upds, dcols):
        states.append(state.astype(BF16))
        state = dcol * (state + upd)

    trow = lax.broadcasted_iota(jnp.int32, (t_len, t_len), 0)
    tcol = lax.broadcasted_iota(jnp.int32, (t_len, t_len), 1)
    s_groups = []
    for g in range(BRANCH // LANES):
        w = jnp.where(trow >= tcol, ws_ref[g], 0.0).astype(BF16)
        cols = slice(g * LANES, (g + 1) * LANES)
        mixed = []
        for n in range(0, tb // t_len, 2):
            pair = _dot(w, jnp.concatenate([svn[n * t_len:(n + 1) * t_len, cols],
                                            svn[(n + 1) * t_len:(n + 2) * t_len, cols]], axis=1))
            mixed += [pair[:, :LANES], pair[:, LANES:]]
        s_groups.append(jnp.concatenate(mixed, axis=0))
    s = jnp.concatenate(s_groups, axis=1) + jnp.concatenate([bs_ref[...]] * (tb // t_len), axis=0)

    gng = gng_ref[...]
    crow = lax.broadcasted_iota(jnp.int32, (HEADS * c_len, c_len), 0) % c_len
    ccol = lax.broadcasted_iota(jnp.int32, (HEADS * c_len, c_len), 1)
    causal = crow >= ccol
    fill = jnp.zeros((QK, MXU_TILE - DV - c_len), BF16)
    fused = [_dot(qm, jnp.concatenate([s_b, kt, fill], axis=1)) for qm, s_b, kt in zip(qms, states, kts)]
    o_b = u * s * _silu(g_b)
    o_chunks = []
    for sl, f in zip(chunks, fused):
        o_inter = f[:, 0:DV]
        sc = jnp.where(causal, f[:, DV:DV + c_len], 0.0).astype(BF16)
        o_heads = []
        for h in range(HEADS):
            rows = slice(h * c_len, (h + 1) * c_len)
            o = _dot(sc[rows], v[sl, h * DV:(h + 1) * DV]) + o_inter[rows]
            o = o * lax.rsqrt(jnp.mean(o * o, axis=-1, keepdims=True) + LN_EPS)
            o_heads.append(o * gng[:, h * DV:(h + 1) * DV])
        o_chunks.append(jnp.concatenate(o_heads, axis=1))
    o_a = jnp.concatenate(o_chunks, axis=0) * _silu(g_a)
    return _project_and_norm(x, o_a, o_b, wout_ref, lng_ref, lnb_ref), state


def _odd_kernel(run_ref, x_ref, pos_ref, w_hbm, perm_ref, inv_ref, ctab_ref, stab_ref, dmat_ref, wint_ref,
                wst_ref, dec_ref, rng_ref, wp_ref, psc_ref, wout_hbm, lng_ref, lnb_ref, o_ref, st_s, pad_s,
                cos_s, sin_s, wqk_s, w_ref, wout_ref, wstage_s, ostage_s, sem):
    tb = x_ref.shape[1]
    seq = pl.program_id(0)
    i = pl.program_id(1)

    @pl.when((seq == 0) & (i == 0))
    def _():
        _load_as_bf16(w_hbm, w_ref, wstage_s, sem)
        _load_as_bf16(wout_hbm, wout_ref, ostage_s, sem, 1.0 / DN_ALPHA)
        wqk_s[...] = _dot(w_ref[:, 0:2 * QK], perm_ref[...]).astype(BF16)

    @pl.when(i == 0)
    def _():
        st_s[...] = jnp.zeros_like(st_s)
        pad_s[0:POOL_PAD, :] = jnp.zeros((POOL_PAD, BRANCH), F32)

    c_len = RET_CHUNK
    half = DK // 2
    chunks = [slice(c * c_len, (c + 1) * c_len) for c in range(tb // c_len)]
    c_q, c_k, c_v, c_gc, c_p, c_gd = [(o, o + n) for o, n in zip(
        (0, QK, 2 * QK, 2 * QK + BRANCH, 2 * QK + 2 * BRANCH, 2 * QK + 3 * BRANCH),
        (QK, QK, BRANCH, BRANCH, BRANCH, BRANCH))]

    consecutive = run_ref[seq, i] == 1

    @pl.when(consecutive)
    def _():
        a0 = pos_ref[0, :, 0:1].astype(F32) * inv_ref[...]
        c0 = jnp.cos(a0)
        s0 = jnp.sin(a0)
        cos_s[...] = c0 * ctab_ref[...] - s0 * stab_ref[...]
        sin_s[...] = s0 * ctab_ref[...] + c0 * stab_ref[...]

    @pl.when(jnp.logical_not(consecutive))
    def _():
        pos_rows = jnp.broadcast_to(pos_ref[0].astype(F32), (LANES, tb)).T
        ang = pos_rows * inv_ref[...]
        cos_s[...] = jnp.cos(ang)
        sin_s[...] = jnp.sin(ang)

    x = x_ref[0]
    xb = x.astype(BF16)

    q = _dot(xb, wqk_s[:, c_q[0]:c_q[1]]) * (DK ** -0.5)
    k = _dot(xb, wqk_s[:, c_k[0]:c_k[1]])
    cos = cos_s[...]
    sin = sin_s[...]
    v = _dot(xb, w_ref[:, c_v[0]:c_v[1]]).astype(BF16)
    p = _dot(xb, w_ref[:, c_p[0]:c_p[1]])
    q1, q2 = q[:, :LANES], q[:, LANES:]
    k1, k2 = k[:, :LANES], k[:, LANES:]
    qr = jnp.concatenate([q1 * cos - q2 * sin, q1 * sin + q2 * cos], axis=1)
    kr = jnp.concatenate([k1 * cos - k2 * sin, k1 * sin + k2 * cos], axis=1)
    masks = [_head_mask(h, False).astype(BF16) for h in range(HEADS)]
    qr_b = qr.astype(BF16)
    qms = [jnp.concatenate([qr_b[sl] * masks[h] for h in range(HEADS)], axis=0)
           for sl in chunks]
    g_c = _dot(xb, w_ref[:, c_gc[0]:c_gc[1]])

    pad_s[POOL_PAD:POOL_PAD + tb, :] = p
    t_top = i * tb + lax.broadcasted_iota(jnp.int32, (POOL_PAD, LANES), 0)
    pooled = []
    for g, win in enumerate(POOL_WINDOWS):
        cols = slice(g * LANES, (g + 1) * LANES)
        acc = pad_s[:, cols]
        shift = 1
        while shift < win:
            acc = acc + pltpu.roll(acc, shift, axis=0)
            shift *= 2
        cnt_top = jnp.minimum(t_top + 1, win).astype(F32)
        mean = jnp.concatenate([acc[POOL_PAD:2 * POOL_PAD] / cnt_top, acc[2 * POOL_PAD:] * (1.0 / win)],
                               axis=0)
        pooled.append((mean - p[:, cols]).astype(BF16))
    tail = pad_s[tb:tb + POOL_PAD, :]
    pad_s[0:POOL_PAD, :] = tail
    kts, upds = [], []
    for sl in chunks:
        kt = (kr[sl] * wst_ref[...]).astype(BF16).T
        kts.append(kt)
        upd = [_dot(jnp.concatenate([kt[h * half:(h + 1) * half],
                                     kt[LANES + h * half:LANES + (h + 1) * half]], axis=0),
                    v[sl, h * DV:(h + 1) * DV]) for h in range(HEADS)]
        upds.append(jnp.concatenate([u[:half] for u in upd] + [u[half:] for u in upd], axis=0))
    g_d = _dot(xb, w_ref[:, c_gd[0]:c_gd[1]])

    state = st_s[...]
    states = []
    for upd in upds:
        states.append(state.astype(BF16))
        state = dec_ref[...] * state + upd
    st_s[...] = state
    yd = jnp.concatenate([_dot(jnp.concatenate([pooled[2 * n], pooled[2 * n + 1]], axis=1), wp_ref[n])
                          for n in range(len(POOL_WINDOWS) // 2)], axis=1)
    fused = [_dot(qm, jnp.concatenate([s_b, kt], axis=1)) for qm, s_b, kt in zip(qms, states, kts)]
    o_d = yd * psc_ref[...] * _silu(g_d)
    rng = rng_ref[...]
    o_chunks = []
    for sl, f in zip(chunks, fused):
        o_inter = f[:, 0:DV] * wint_ref[...]
        sc = (f[:, DV:DV + c_len] * dmat_ref[...]).astype(BF16)
        o_heads = []
        for h in range(HEADS):
            rows = slice(h * c_len, (h + 1) * c_len)
            o = _dot(sc[rows], v[sl, h * DV:(h + 1) * DV]) + o_inter[rows]
            mu = jnp.mean(o, axis=-1, keepdims=True)
            oc = o - mu
            var = jnp.mean(oc * oc, axis=-1, keepdims=True)
            o_heads.append(oc * lax.rsqrt(var + LN_EPS) * rng[:, h * DV:(h + 1) * DV])
        o_chunks.append(jnp.concatenate(o_heads, axis=1))
    o_c = jnp.concatenate(o_chunks, axis=0) * _silu(g_c)
    o_ref[0] = _project_and_norm(x, o_c, o_d, wout_ref, lng_ref, lnb_ref)


def _full_spec(a):
    nd = a.ndim
    return pl.BlockSpec(a.shape, lambda *_, _nd=nd: (0,) * _nd)


def _compiler_params():
    return pltpu.CompilerParams(dimension_semantics=("arbitrary", "arbitrary"),
                                vmem_limit_bytes=VMEM_LIMIT_BYTES)


def _row(a):
    return a.reshape(1, -1).astype(F32)


def _hbm_spec():
    return pl.BlockSpec(memory_space=pl.ANY)


def _stage(w):
    rows = min(w.shape[0], STAGE_BYTES // (4 * w.shape[1]) // LANES * LANES)
    return pltpu.VMEM((2, rows, w.shape[1]), F32)


def _even_layer(x, w_in, w_a2, b_a, gla_norm_g, sgu_ln_g, sgu_ln_b, w_s, b_s, w_out, ln_g, ln_b):
    B, S, D = x.shape
    tb = TOKEN_BLOCK
    w_in_t = w_in.T
    wa2t = jnp.pad(w_a2.T, ((0, 0), (0, LANES - GLA_RANK))).astype(BF16)
    bs = jnp.repeat(jnp.transpose(b_s), LANES, axis=1).astype(F32)
    params = (w_in_t, wa2t, _row(b_a), _row(gla_norm_g), _row(sgu_ln_g), _row(sgu_ln_b), w_s.astype(F32), bs,
              w_out, _row(ln_g), _row(ln_b))
    in_hbm = (w_in_t, w_out)
    tok_spec = pl.BlockSpec((1, tb, D), lambda b, i: (b, i, 0))
    return pl.pallas_call(
        _even_kernel,
        out_shape=jax.ShapeDtypeStruct((B, S, D), F32),
        grid=(B, S // tb),
        in_specs=[tok_spec] + [_hbm_spec() if any(p is h for h in in_hbm) else _full_spec(p) for p in params],
        out_specs=tok_spec,
        scratch_shapes=[
            pltpu.VMEM((QK, DV), F32),
            pltpu.VMEM((QK, D), BF16),
            pltpu.VMEM(w_in_t.shape, BF16),
            pltpu.VMEM(w_out.shape, BF16),
            _stage(w_in_t),
            _stage(w_out),
            pltpu.SemaphoreType.DMA((2,)),
        ],
        compiler_params=_compiler_params(),
        name="even_layer_gla_sgu",
    )(x, *params)


def _retention_tables():
    c = RET_CHUNK
    log_gamma = np.log(1.0 - 2.0 ** (-5.0 - np.arange(HEADS, dtype=np.float64)))
    idx = np.arange(c, dtype=np.float64)
    rel = idx[:, None] - idx[None, :]
    dmat = np.where(rel >= 0, np.exp((idx[:, None] - (c - 1.0))[None] * log_gamma[:, None, None]), 0.0)
    w_inter = np.exp((idx + 1.0)[None] * log_gamma[:, None])
    w_state = np.exp((c - 1.0 - idx)[None] * log_gamma[:, None])
    lane_head = (np.arange(QK) % LANES) // (DK // 2)
    dmat = dmat.reshape(HEADS * c, c)
    wint = np.broadcast_to(w_inter[:, :, None], (HEADS, c, DV)).reshape(HEADS * c, DV)
    wst = w_state[lane_head, :].T
    dec = np.broadcast_to(np.exp(c * log_gamma)[lane_head][:, None], (QK, DV))
    inv = ROPE_BASE ** (-np.arange(DK // 2, dtype=np.float64) / (DK // 2))
    inv = np.tile(inv, HEADS)[None, :].astype(np.float32)
    off = np.arange(TOKEN_BLOCK, dtype=np.float64)[:, None] * inv.astype(np.float64)
    f = lambda a: jnp.asarray(np.ascontiguousarray(a), F32)
    return f(inv), f(np.cos(off)), f(np.sin(off)), f(dmat), f(wint), f(wst), f(dec)


def _odd_layer(x, positions, w_in, ret_norm_g, w_pool, pool_scale, w_out, ln_g, ln_b):
    B, S, D = x.shape
    tb = TOKEN_BLOCK
    j = np.arange(2 * QK)
    src = (j // QK) * QK + ((j % LANES) // (DK // 2)) * DK + ((j % QK) // LANES) * (DK // 2) + j % (DK // 2)
    perm = jnp.asarray(src[None, :] == np.arange(2 * QK)[:, None], BF16)
    zero = jnp.zeros_like(w_pool[0])
    wp = jnp.stack([jnp.block([[w_pool[2 * n], zero], [zero, w_pool[2 * n + 1]]])
                    for n in range(len(POOL_WINDOWS) // 2)]).astype(BF16)
    params = (w_in, perm, *_retention_tables(), _row(ret_norm_g), wp, _row(pool_scale), w_out, _row(ln_g),
              _row(ln_b))
    in_hbm = (w_in, w_out)
    pos_blocks = positions.reshape(B, S // tb, tb)
    runs = jnp.all(pos_blocks == pos_blocks[:, :, :1] + jnp.arange(tb, dtype=positions.dtype), axis=-1)
    tok_spec = pl.BlockSpec((1, tb, D), lambda b, i: (b, i, 0))
    pos_spec = pl.BlockSpec((1, 1, tb), lambda b, i: (b, 0, i))
    return pl.pallas_call(
        _odd_kernel,
        out_shape=jax.ShapeDtypeStruct((B, S, D), F32),
        grid=(B, S // tb),
        in_specs=[pl.BlockSpec(memory_space=pltpu.SMEM), tok_spec, pos_spec]
                 + [_hbm_spec() if any(p is h for h in in_hbm) else _full_spec(p) for p in params],
        out_specs=tok_spec,
        scratch_shapes=[
            pltpu.VMEM((QK, DV), F32),
            pltpu.VMEM((POOL_PAD + tb, BRANCH), F32),
            pltpu.VMEM((tb, LANES), F32),
            pltpu.VMEM((tb, LANES), F32),
            pltpu.VMEM((D, 2 * QK), BF16),
            pltpu.VMEM(w_in.shape, BF16),
            pltpu.VMEM(w_out.shape, BF16),
            _stage(w_in),
            _stage(w_out),
            pltpu.SemaphoreType.DMA((2,)),
        ],
        compiler_params=_compiler_params(),
        name="odd_layer_retention_pool",
    )(runs.astype(jnp.int32), x, positions.reshape(B, 1, S), *params)


def kernel(x, positions, l0_w_in, l0_w_a2, l0_b_a, l0_gla_norm_g, l0_sgu_ln_g, l0_sgu_ln_b, l0_w_s, l0_b_s, l0_w_out, l0_ln_g, l0_ln_b, l1_w_in, l1_ret_norm_g, l1_w_pool, l1_pool_scale, l1_w_out, l1_ln_g, l1_ln_b):
    x = _even_layer(x, l0_w_in, l0_w_a2, l0_b_a, l0_gla_norm_g, l0_sgu_ln_g, l0_sgu_ln_b, l0_w_s, l0_b_s,
                    l0_w_out, l0_ln_g, l0_ln_b)
    return _odd_layer(x, positions, l1_w_in, l1_ret_norm_g, l1_w_pool, l1_pool_scale, l1_w_out, l1_ln_g, l1_ln_b)
```

```python
import functools

import jax
import jax.numpy as jnp
import numpy as np
from jax import lax
from jax.experimental import pallas as pl
from jax.experimental.pallas import tpu as pltpu

F32 = jnp.float32
BF16 = jnp.bfloat16

D_MODEL = 1024
BRANCH = 512
HEADS = 4
DK = 64
DV = 128
QK = HEADS * DK
GLA_RANK = 16
GLA_TAU = 16.0
GLA_CHUNK = 64
SGU_CHUNK = 128
RET_CHUNK = 128
ROPE_BASE = 10000.0
POOL_WINDOWS = (2, 4, 8, 16)
POOL_PAD = 16
DN_ALPHA = 4.0 ** 0.25
LN_EPS = 1e-5
LANES = 128
TOKEN_BLOCK = 1024
NORM_ROW_GROUPS = 4
VMEM_LIMIT_BYTES = 56 * 1024 * 1024
EVEN_Q, EVEN_K, EVEN_V, EVEN_GA, EVEN_LR = 0, QK, 2 * QK, 2 * QK + BRANCH, 2 * QK + 2 * BRANCH
EVEN_U, EVEN_SV, EVEN_GB = EVEN_LR + GLA_RANK, EVEN_LR + GLA_RANK + BRANCH, EVEN_LR + GLA_RANK + 2 * BRANCH
MXU_TILE = 256
STAGE_BYTES = 1024 * 1024
STAGE_SLOTS = 4
SUBLANES = 8


def _dot(a, b):
    return jnp.dot(a, b, preferred_element_type=F32)


def _dot_nt(a, b):
    return lax.dot_general(a, b, (((1,), (1,)), ((), ())), preferred_element_type=F32)


def _silu(x):
    hx = 0.5 * x
    return hx + hx * jnp.tanh(hx)


def _gelu_tanh(x):
    c = np.sqrt(2.0 / np.pi)
    hx = 0.5 * x
    return hx + hx * jnp.tanh(x * (np.float32(c) + np.float32(c * 0.044715) * (x * x)))


def _log_sigmoid(z):
    return jnp.minimum(z, 0.0) - jnp.log(1.0 + jnp.exp(-jnp.abs(z)))


def _layer_norm_rows(r, g, b, eps=LN_EPS):
    mu = jnp.mean(r, axis=-1, keepdims=True)
    c = r - mu
    var = jnp.mean(c * c, axis=-1, keepdims=True)
    return c * lax.rsqrt(var + eps) * g + b


def _deepnorm(x, y_scaled, g, b):
    return _layer_norm_rows(x + y_scaled, g, b, LN_EPS / DN_ALPHA ** 2)


def _project_and_norm(x, o_first, o_second, wout_ref, lng_ref, lnb_ref):
    rows_per_group = x.shape[0] // NORM_ROW_GROUPS
    outs = []
    for n in range(NORM_ROW_GROUPS):
        rows = slice(n * rows_per_group, (n + 1) * rows_per_group)
        y = (_dot(o_second[rows].astype(BF16), wout_ref[BRANCH:2 * BRANCH, :])
             + _dot(o_first[rows].astype(BF16), wout_ref[0:BRANCH, :]))
        outs.append(_deepnorm(x[rows], y, lng_ref[...], lnb_ref[...]))
    return jnp.concatenate(outs, axis=0)


def _head_mask(h, natural):
    lane = lax.broadcasted_iota(jnp.int32, (1, QK), 1)
    head = lane // DK if natural else (lane % LANES) // (DK // 2)
    return (head == h).astype(F32)


def _load_as_bf16(src_hbm, dst_ref, stage_ref, sem, scale=None):
    rows = src_hbm.shape[0]
    slots, per_copy = stage_ref.shape[0], stage_ref.shape[1]
    starts = list(range(0, rows, per_copy))

    def copy(k):
        n_rows = min(per_copy, rows - starts[k])
        return pltpu.make_async_copy(src_hbm.at[pl.ds(starts[k], n_rows), :],
                                     stage_ref.at[k % slots, pl.ds(0, n_rows), :], sem.at[k % slots])

    for k in range(min(slots - 1, len(starts))):
        copy(k).start()
    for k, start in enumerate(starts):
        if k + slots - 1 < len(starts):
            copy(k + slots - 1).start()
        copy(k).wait()
        n_rows = min(per_copy, rows - start)
        block = stage_ref[k % slots, 0:n_rows, :]
        if scale is not None:
            block = block * scale
        dst_ref[start:start + n_rows, :] = block.astype(BF16)


def _even_kernel(x_ref, wt_hbm, wa2t_ref, ba_ref, gng_ref, slg_ref, slb_ref, ws_ref, bs_ref, wout_hbm,
                 lng_ref, lnb_ref, o_ref, st_s, wzt_s, wt_s, wout_s, wstage_s, ostage_s, sem):
    @pl.when((pl.program_id(0) == 0) & (pl.program_id(1) == 0))
    def _():
        _load_as_bf16(wt_hbm, wt_s, wstage_s, sem)
        _load_as_bf16(wout_hbm, wout_s, ostage_s, sem, 1.0 / DN_ALPHA)
        wzt_s[...] = _dot(wa2t_ref[...], wt_s[EVEN_LR:EVEN_LR + LANES, :]).astype(BF16)

    @pl.when(pl.program_id(1) == 0)
    def _():
        st_s[...] = jnp.zeros_like(st_s)

    refs = (wt_s, wzt_s, ba_ref, gng_ref, slg_ref, slb_ref, ws_ref, bs_ref, wout_s, lng_ref, lnb_ref)
    o_ref[0], st_s[...] = _even_block(x_ref[0], st_s[...], *refs)


def _even_block(x, state, wt_ref, wzt_ref, ba_ref, gng_ref, slg_ref, slb_ref, ws_ref, bs_ref, wout_ref, lng_ref,
                lnb_ref):
    tb = x.shape[0]
    xb = x.astype(BF16)
    c_len = GLA_CHUNK
    t_len = SGU_CHUNK
    chunks = [slice(c * c_len, (c + 1) * c_len) for c in range(tb // c_len)]

    z = _dot_nt(xb, wzt_ref[...]) + ba_ref[...]
    q = _dot_nt(xb, wt_ref[EVEN_Q:EVEN_Q + QK, :])
    k = _dot_nt(xb, wt_ref[EVEN_K:EVEN_K + QK, :])
    la = _log_sigmoid(z) * (1.0 / GLA_TAU)
    v = _dot_nt(xb, wt_ref[EVEN_V:EVEN_V + BRANCH, :]).astype(BF16)

    row = lax.broadcasted_iota(jnp.int32, (LANES, 2 * LANES), 0)
    col = lax.broadcasted_iota(jnp.int32, (LANES, 2 * LANES), 1) % LANES
    tril2 = ((row >= col) & (row // c_len == col // c_len)).astype(BF16)
    la_hi = la.astype(BF16)
    la_lo = (la - la_hi.astype(F32)).astype(BF16)
    b = jnp.concatenate(
        [_dot(tril2, jnp.concatenate([la_hi[n * LANES:(n + 1) * LANES], la_lo[n * LANES:(n + 1) * LANES]],
                                     axis=0))
         for n in range(tb // LANES)], axis=0)

    u = _dot_nt(xb, wt_ref[EVEN_U:EVEN_U + BRANCH, :])
    q_dec = q * jnp.exp(b) * (DK ** -0.5)
    k_dec = k * jnp.exp(-b)
    masks = [_head_mask(h, True).astype(BF16) for h in range(HEADS)]
    q_dec_b = q_dec.astype(BF16)
    qms = [jnp.concatenate([q_dec_b[sl] * masks[h] for h in range(HEADS)], axis=0)
           for sl in chunks]
    sv = _dot_nt(xb, wt_ref[EVEN_SV:EVEN_SV + BRANCH, :])

    u = _gelu_tanh(u)
    kts, upds, dcols = [], [], []
    for c, sl in enumerate(chunks):
        kt = k_dec[sl].T.astype(BF16)
        kts.append(kt)
        upds.append(jnp.concatenate([_dot(kt[h * DK:(h + 1) * DK], v[sl, h * DV:(h + 1) * DV])
                                     for h in range(HEADS)], axis=0))
        b_last = b[(c + 1) * c_len - 1:(c + 1) * c_len, :]
        dcols.append(jnp.exp(jnp.broadcast_to(b_last, (LANES, QK)).T))
    g_a = _dot_nt(xb, wt_ref[EVEN_GA:EVEN_GA + BRANCH, :])
    sv = _gelu_tanh(sv)
    slg = slg_ref[...]
    slb = slb_ref[...]
    svn = jnp.concatenate(
        [_layer_norm_rows(sv[:, g * LANES:(g + 1) * LANES], slg[:, g * LANES:(g + 1) * LANES],
                          slb[:, g * LANES:(g + 1) * LANES]) for g in range(BRANCH // LANES)],
        axis=1).astype(BF16)
    g_b = _dot_nt(xb, wt_ref[EVEN_GB:EVEN_GB + BRANCH, :])

    states = []
    for upd, dcol in zip(upds, dcols):
        states.append(state.astype(BF16))
        state = dcol * (state + upd)

    trow = lax.broadcasted_iota(jnp.int32, (t_len, t_len), 0)
    tcol = lax.broadcasted_iota(jnp.int32, (t_len, t_len), 1)
    s_groups = []
    for g in range(BRANCH // LANES):
        w = jnp.where(trow >= tcol, ws_ref[g], 0.0).astype(BF16)
        cols = slice(g * LANES, (g + 1) * LANES)
        mixed = []
        for n in range(0, tb // t_len, 2):
            pair = _dot(w, jnp.concatenate([svn[n * t_len:(n + 1) * t_len, cols],
                                            svn[(n + 1) * t_len:(n + 2) * t_len, cols]], axis=1))
            mixed += [pair[:, :LANES], pair[:, LANES:]]
        s_groups.append(jnp.concatenate(mixed, axis=0))
    s = jnp.concatenate(s_groups, axis=1) + jnp.concatenate([bs_ref[...]] * (tb // t_len), axis=0)

    gng = gng_ref[...]
    crow = lax.broadcasted_iota(jnp.int32, (HEADS * c_len, c_len), 0) % c_len
    ccol = lax.broadcasted_iota(jnp.int32, (HEADS * c_len, c_len), 1)
    causal = crow >= ccol
    fill = jnp.zeros((QK, MXU_TILE - DV - c_len), BF16)
    fused = [_dot(qm, jnp.concatenate([s_b, kt, fill], axis=1)) for qm, s_b, kt in zip(qms, states, kts)]
    o_b = u * s * _silu(g_b)
    o_chunks = []
    for sl, f in zip(chunks, fused):
        o_inter = f[:, 0:DV]
        sc = jnp.where(causal, f[:, DV:DV + c_len], 0.0).astype(BF16)
        o_heads = []
        for h in range(HEADS):
            rows = slice(h * c_len, (h + 1) * c_len)
            o = _dot(sc[rows], v[sl, h * DV:(h + 1) * DV]) + o_inter[rows]
            o = o * lax.rsqrt(jnp.mean(o * o, axis=-1, keepdims=True) + LN_EPS)
            o_heads.append(o * gng[:, h * DV:(h + 1) * DV])
        o_chunks.append(jnp.concatenate(o_heads, axis=1))
    o_a = jnp.concatenate(o_chunks, axis=0) * _silu(g_a)
    return _project_and_norm(x, o_a, o_b, wout_ref, lng_ref, lnb_ref), state


def _odd_kernel(run_ref, x_ref, pos_ref, w_hbm, perm_ref, inv_ref, ctab_ref, stab_ref, dmat_ref, wint_ref,
                wst_ref, dec_ref, rng_ref, wp_ref, psc_ref, wout_hbm, lng_ref, lnb_ref, o_ref, st_s, pad_s,
                cos_s, sin_s, wqk_s, w_ref, wout_ref, wstage_s, ostage_s, sem):
    tb = x_ref.shape[1]
    seq = pl.program_id(0)
    i = pl.program_id(1)

    @pl.when((seq == 0) & (i == 0))
    def _():
        _load_as_bf16(w_hbm, w_ref, wstage_s, sem)
        _load_as_bf16(wout_hbm, wout_ref, ostage_s, sem, 1.0 / DN_ALPHA)
        wqk_s[...] = _dot(w_ref[:, 0:2 * QK], perm_ref[...]).astype(BF16)

    @pl.when(i == 0)
    def _():
        st_s[...] = jnp.zeros_like(st_s)
        pad_s[0:POOL_PAD, :] = jnp.zeros((POOL_PAD, BRANCH), F32)

    c_len = RET_CHUNK
    half = DK // 2
    chunks = [slice(c * c_len, (c + 1) * c_len) for c in range(tb // c_len)]
    c_q, c_k, c_v, c_gc, c_p, c_gd = [(o, o + n) for o, n in zip(
        (0, QK, 2 * QK, 2 * QK + BRANCH, 2 * QK + 2 * BRANCH, 2 * QK + 3 * BRANCH),
        (QK, QK, BRANCH, BRANCH, BRANCH, BRANCH))]

    consecutive = run_ref[seq, i] == 1

    @pl.when(consecutive)
    def _():
        a0 = pos_ref[0, :, 0:1].astype(F32) * inv_ref[...]
        c0 = jnp.cos(a0)
        s0 = jnp.sin(a0)
        cos_s[...] = c0 * ctab_ref[...] - s0 * stab_ref[...]
        sin_s[...] = s0 * ctab_ref[...] + c0 * stab_ref[...]

    @pl.when(jnp.logical_not(consecutive))
    def _():
        pos_rows = jnp.broadcast_to(pos_ref[0].astype(F32), (LANES, tb)).T
        ang = pos_rows * inv_ref[...]
        cos_s[...] = jnp.cos(ang)
        sin_s[...] = jnp.sin(ang)

    x = x_ref[0]
    xb = x.astype(BF16)

    q = _dot(xb, wqk_s[:, c_q[0]:c_q[1]]) * (DK ** -0.5)
    k = _dot(xb, wqk_s[:, c_k[0]:c_k[1]])
    cos = cos_s[...]
    sin = sin_s[...]
    v = _dot(xb, w_ref[:, c_v[0]:c_v[1]]).astype(BF16)
    p = _dot(xb, w_ref[:, c_p[0]:c_p[1]])
    q1, q2 = q[:, :LANES], q[:, LANES:]
    k1, k2 = k[:, :LANES], k[:, LANES:]
    qr = jnp.concatenate([q1 * cos - q2 * sin, q1 * sin + q2 * cos], axis=1)
    kr = jnp.concatenate([k1 * cos - k2 * sin, k1 * sin + k2 * cos], axis=1)
    masks = [_head_mask(h, False).astype(BF16) for h in range(HEADS)]
    qr_b = qr.astype(BF16)
    qms = [jnp.concatenate([qr_b[sl] * masks[h] for h in range(HEADS)], axis=0)
           for sl in chunks]
    g_c = _dot(xb, w_ref[:, c_gc[0]:c_gc[1]])

    pad_s[POOL_PAD:POOL_PAD + tb, :] = p
    t_top = i * tb + lax.broadcasted_iota(jnp.int32, (POOL_PAD, LANES), 0)
    pooled = []
    for g, win in enumerate(POOL_WINDOWS):
        cols = slice(g * LANES, (g + 1) * LANES)
        acc = pad_s[:, cols]
        shift = 1
        while shift < win:
            acc = acc + pltpu.roll(acc, shift, axis=0)
            shift *= 2
        cnt_top = jnp.minimum(t_top + 1, win).astype(F32)
        mean = jnp.concatenate([acc[POOL_PAD:2 * POOL_PAD] / cnt_top, acc[2 * POOL_PAD:] * (1.0 / win)],
                               axis=0)
        pooled.append((mean - p[:, cols]).astype(BF16))
    tail = pad_s[tb:tb + POOL_PAD, :]
    pad_s[0:POOL_PAD, :] = tail
    kts, upds = [], []
    for sl in chunks:
        kt = (kr[sl] * wst_ref[...]).T.astype(BF16)
        kts.append(kt)
        upd = [_dot(jnp.concatenate([kt[h * half:(h + 1) * half],
                                     kt[LANES + h * half:LANES + (h + 1) * half]], axis=0),
                    v[sl, h * DV:(h + 1) * DV]) for h in range(HEADS)]
        upds.append(jnp.concatenate([u[:half] for u in upd] + [u[half:] for u in upd], axis=0))
    g_d = _dot(xb, w_ref[:, c_gd[0]:c_gd[1]])

    state = st_s[...]
    states = []
    for upd in upds:
        states.append(state.astype(BF16))
        state = dec_ref[...] * state + upd
    st_s[...] = state
    yd = jnp.concatenate([_dot(jnp.concatenate([pooled[2 * n], pooled[2 * n + 1]], axis=1), wp_ref[n])
                          for n in range(len(POOL_WINDOWS) // 2)], axis=1)
    fused = [_dot(qm, jnp.concatenate([s_b, kt], axis=1)) for qm, s_b, kt in zip(qms, states, kts)]
    o_d = yd * psc_ref[...] * _silu(g_d)
    rng = rng_ref[...]
    o_chunks = []
    for sl, f in zip(chunks, fused):
        o_inter = f[:, 0:DV] * wint_ref[...]
        sc = (f[:, DV:DV + c_len] * dmat_ref[...]).astype(BF16)
        o_heads = []
        for h in range(HEADS):
            rows = slice(h * c_len, (h + 1) * c_len)
            o = _dot(sc[rows], v[sl, h * DV:(h + 1) * DV]) + o_inter[rows]
            mu = jnp.mean(o, axis=-1, keepdims=True)
            oc = o - mu
            var = jnp.mean(oc * oc, axis=-1, keepdims=True)
            o_heads.append(oc * lax.rsqrt(var + LN_EPS) * rng[:, h * DV:(h + 1) * DV])
        o_chunks.append(jnp.concatenate(o_heads, axis=1))
    o_c = jnp.concatenate(o_chunks, axis=0) * _silu(g_c)
    o_ref[0] = _project_and_norm(x, o_c, o_d, wout_ref, lng_ref, lnb_ref)


def _full_spec(a):
    nd = a.ndim
    return pl.BlockSpec(a.shape, lambda *_, _nd=nd: (0,) * _nd)


def _compiler_params():
    return pltpu.CompilerParams(dimension_semantics=("arbitrary", "arbitrary"),
                                vmem_limit_bytes=VMEM_LIMIT_BYTES)


def _row(a):
    return a.reshape(1, -1).astype(F32)


def _hbm_spec():
    return pl.BlockSpec(memory_space=pl.ANY)


def _stage(w):
    rows = min(w.shape[0], STAGE_BYTES // (4 * w.shape[1]) // SUBLANES * SUBLANES)
    return pltpu.VMEM((STAGE_SLOTS, rows, w.shape[1]), F32)


def _even_layer(x, w_in, w_a2, b_a, gla_norm_g, sgu_ln_g, sgu_ln_b, w_s, b_s, w_out, ln_g, ln_b):
    B, S, D = x.shape
    tb = TOKEN_BLOCK
    w_in_t = w_in.T
    wa2t = jnp.pad(w_a2.T, ((0, 0), (0, LANES - GLA_RANK))).astype(BF16)
    bs = jnp.repeat(jnp.transpose(b_s), LANES, axis=1).astype(F32)
    params = (w_in_t, wa2t, _row(b_a), _row(gla_norm_g), _row(sgu_ln_g), _row(sgu_ln_b), w_s.astype(F32), bs,
              w_out, _row(ln_g), _row(ln_b))
    in_hbm = (w_in_t, w_out)
    tok_spec = pl.BlockSpec((1, tb, D), lambda b, i: (b, i, 0))
    return pl.pallas_call(
        _even_kernel,
        out_shape=jax.ShapeDtypeStruct((B, S, D), F32),
        grid=(B, S // tb),
        in_specs=[tok_spec] + [_hbm_spec() if any(p is h for h in in_hbm) else _full_spec(p) for p in params],
        out_specs=tok_spec,
        scratch_shapes=[
            pltpu.VMEM((QK, DV), F32),
            pltpu.VMEM((QK, D), BF16),
            pltpu.VMEM(w_in_t.shape, BF16),
            pltpu.VMEM(w_out.shape, BF16),
            _stage(w_in_t),
            _stage(w_out),
            pltpu.SemaphoreType.DMA((STAGE_SLOTS,)),
        ],
        compiler_params=_compiler_params(),
        name="even_layer_gla_sgu",
    )(x, *params)


def _retention_tables():
    c = RET_CHUNK
    log_gamma = np.log(1.0 - 2.0 ** (-5.0 - np.arange(HEADS, dtype=np.float64)))
    idx = np.arange(c, dtype=np.float64)
    rel = idx[:, None] - idx[None, :]
    dmat = np.where(rel >= 0, np.exp((idx[:, None] - (c - 1.0))[None] * log_gamma[:, None, None]), 0.0)
    w_inter = np.exp((idx + 1.0)[None] * log_gamma[:, None])
    w_state = np.exp((c - 1.0 - idx)[None] * log_gamma[:, None])
    lane_head = (np.arange(QK) % LANES) // (DK // 2)
    dmat = dmat.reshape(HEADS * c, c)
    wint = np.broadcast_to(w_inter[:, :, None], (HEADS, c, DV)).reshape(HEADS * c, DV)
    wst = w_state[lane_head, :].T
    dec = np.broadcast_to(np.exp(c * log_gamma)[lane_head][:, None], (QK, DV))
    inv = ROPE_BASE ** (-np.arange(DK // 2, dtype=np.float64) / (DK // 2))
    inv = np.tile(inv, HEADS)[None, :].astype(np.float32)
    off = np.arange(TOKEN_BLOCK, dtype=np.float64)[:, None] * inv.astype(np.float64)
    f = lambda a: jnp.asarray(np.ascontiguousarray(a), F32)
    return f(inv), f(np.cos(off)), f(np.sin(off)), f(dmat), f(wint), f(wst), f(dec)


def _odd_layer(x, positions, w_in, ret_norm_g, w_pool, pool_scale, w_out, ln_g, ln_b):
    B, S, D = x.shape
    tb = TOKEN_BLOCK
    j = np.arange(2 * QK)
    src = (j // QK) * QK + ((j % LANES) // (DK // 2)) * DK + ((j % QK) // LANES) * (DK // 2) + j % (DK // 2)
    perm = jnp.asarray(src[None, :] == np.arange(2 * QK)[:, None], BF16)
    zero = jnp.zeros_like(w_pool[0])
    wp = jnp.stack([jnp.block([[w_pool[2 * n], zero], [zero, w_pool[2 * n + 1]]])
                    for n in range(len(POOL_WINDOWS) // 2)]).astype(BF16)
    params = (w_in, perm, *_retention_tables(), _row(ret_norm_g), wp, _row(pool_scale), w_out, _row(ln_g),
              _row(ln_b))
    in_hbm = (w_in, w_out)
    pos_blocks = positions.reshape(B, S // tb, tb)
    runs = jnp.all(pos_blocks == pos_blocks[:, :, :1] + jnp.arange(tb, dtype=positions.dtype), axis=-1)
    tok_spec = pl.BlockSpec((1, tb, D), lambda b, i: (b, i, 0))
    pos_spec = pl.BlockSpec((1, 1, tb), lambda b, i: (b, 0, i))
    return pl.pallas_call(
        _odd_kernel,
        out_shape=jax.ShapeDtypeStruct((B, S, D), F32),
        grid=(B, S // tb),
        in_specs=[pl.BlockSpec(memory_space=pltpu.SMEM), tok_spec, pos_spec]
                 + [_hbm_spec() if any(p is h for h in in_hbm) else _full_spec(p) for p in params],
        out_specs=tok_spec,
        scratch_shapes=[
            pltpu.VMEM((QK, DV), F32),
            pltpu.VMEM((POOL_PAD + tb, BRANCH), F32),
            pltpu.VMEM((tb, LANES), F32),
            pltpu.VMEM((tb, LANES), F32),
            pltpu.VMEM((D, 2 * QK), BF16),
            pltpu.VMEM(w_in.shape, BF16),
            pltpu.VMEM(w_out.shape, BF16),
            _stage(w_in),
            _stage(w_out),
            pltpu.SemaphoreType.DMA((STAGE_SLOTS,)),
        ],
        compiler_params=_compiler_params(),
        name="odd_layer_retention_pool",
    )(runs.astype(jnp.int32), x, positions.reshape(B, 1, S), *params)


def kernel(x, positions, l0_w_in, l0_w_a2, l0_b_a, l0_gla_norm_g, l0_sgu_ln_g, l0_sgu_ln_b, l0_w_s, l0_b_s, l0_w_out, l0_ln_g, l0_ln_b, l1_w_in, l1_ret_norm_g, l1_w_pool, l1_pool_scale, l1_w_out, l1_ln_g, l1_ln_b):
    x = _even_layer(x, l0_w_in, l0_w_a2, l0_b_a, l0_gla_norm_g, l0_sgu_ln_g, l0_sgu_ln_b, l0_w_s, l0_b_s,
                    l0_w_out, l0_ln_g, l0_ln_b)
    return _odd_layer(x, positions, l1_w_in, l1_ret_norm_g, l1_w_pool, l1_pool_scale, l1_w_out, l1_ln_g, l1_ln_b)
```

```python
import jax
import jax.numpy as jnp
import numpy as np
from jax import lax
from jax.experimental import pallas as pl
from jax.experimental.pallas import tpu as pltpu

F32 = jnp.float32
BF16 = jnp.bfloat16

D_MODEL = 1024
BRANCH = 512
HEADS = 4
DK = 64
DV = 128
QK = HEADS * DK
GLA_RANK = 16
GLA_TAU = 16.0
GLA_CHUNK = 64
SGU_CHUNK = 128
RET_CHUNK = 128
ROPE_BASE = 10000.0
POOL_WINDOWS = (2, 4, 8, 16)
POOL_PAD = 16
DN_ALPHA = 4.0 ** 0.25
LN_EPS = 1e-5
LANES = 128
SUBLANES = 8
MXU_TILE = 256
VMEM_LIMIT_BYTES = 56 * 1024 * 1024
TOKEN_BLOCK = 1024
NORM_ROW_GROUPS = 4
STAGE_BYTES = 1024 * 1024
STAGE_SLOTS = 4
EVEN_Q, EVEN_K, EVEN_V, EVEN_GA, EVEN_LR = 0, QK, 2 * QK, 2 * QK + BRANCH, 2 * QK + 2 * BRANCH
EVEN_U, EVEN_SV, EVEN_GB = EVEN_LR + GLA_RANK, EVEN_LR + GLA_RANK + BRANCH, EVEN_LR + GLA_RANK + 2 * BRANCH


def _dot(a, b):
    return jnp.dot(a, b, preferred_element_type=F32)


def _dot_nt(a, b):
    return lax.dot_general(a, b, (((1,), (1,)), ((), ())), preferred_element_type=F32)


def _silu(x):
    hx = 0.5 * x
    return hx + hx * jnp.tanh(hx)


def _gelu_tanh(x):
    c = np.sqrt(2.0 / np.pi)
    hx = 0.5 * x
    return hx + hx * jnp.tanh(x * (np.float32(c) + np.float32(c * 0.044715) * (x * x)))


def _log_sigmoid(z):
    return jnp.minimum(z, 0.0) - jnp.log(1.0 + jnp.exp(-jnp.abs(z)))


def _layer_norm_rows(r, g, b, eps=LN_EPS):
    mu = jnp.mean(r, axis=-1, keepdims=True)
    c = r - mu
    var = jnp.mean(c * c, axis=-1, keepdims=True)
    return c * lax.rsqrt(var + eps) * g + b


def _deepnorm(x, y_scaled, g, b):
    return _layer_norm_rows(x + y_scaled, g, b, LN_EPS / DN_ALPHA ** 2)


def _project_and_norm(x_ref, o_first, o_second, wout_ref, lng_ref, lnb_ref):
    rows_per_group = x_ref.shape[1] // NORM_ROW_GROUPS
    outs = []
    for n in range(NORM_ROW_GROUPS):
        rows = slice(n * rows_per_group, (n + 1) * rows_per_group)
        y = (_dot(o_second[rows].astype(BF16), wout_ref[BRANCH:2 * BRANCH, :])
             + _dot(o_first[rows].astype(BF16), wout_ref[0:BRANCH, :]))
        outs.append(_deepnorm(x_ref[0, rows, :], y, lng_ref[...], lnb_ref[...]))
    return jnp.concatenate(outs, axis=0)


def _head_mask(h, natural):
    lane = lax.broadcasted_iota(jnp.int32, (1, QK), 1)
    head = lane // DK if natural else (lane % LANES) // (DK // 2)
    return (head == h).astype(F32)


def _load_as_bf16(src_hbm, dst_ref, stage_ref, sem, scale=None):
    rows = src_hbm.shape[0]
    slots, per_copy = stage_ref.shape[0], stage_ref.shape[1]
    starts = list(range(0, rows, per_copy))

    def copy(k):
        n_rows = min(per_copy, rows - starts[k])
        return pltpu.make_async_copy(src_hbm.at[pl.ds(starts[k], n_rows), :],
                                     stage_ref.at[k % slots, pl.ds(0, n_rows), :], sem.at[k % slots])

    for k in range(min(slots - 1, len(starts))):
        copy(k).start()
    for k, start in enumerate(starts):
        if k + slots - 1 < len(starts):
            copy(k + slots - 1).start()
        copy(k).wait()
        n_rows = min(per_copy, rows - start)
        block = stage_ref[k % slots, 0:n_rows, :]
        if scale is not None:
            block = block * scale
        dst_ref[start:start + n_rows, :] = block.astype(BF16)


def _even_kernel(x_ref, wt_hbm, wa2_ref, ba_ref, gng_ref, slg_ref, slb_ref, ws_ref, bsg_ref, wout_hbm,
                 lng_ref, lnb_ref, o_ref, st_s, wzt_s, bs_s, wt_s, wout_s, wstage_s, ostage_s, sem):
    @pl.when((pl.program_id(0) == 0) & (pl.program_id(1) == 0))
    def _():
        _load_as_bf16(wt_hbm, wt_s, wstage_s, sem)
        _load_as_bf16(wout_hbm, wout_s, ostage_s, sem, 1.0 / DN_ALPHA)
        wa2t = jnp.concatenate([wa2_ref[...], jnp.zeros((LANES - GLA_RANK, QK), F32)], axis=0).T.astype(BF16)
        wzt_s[...] = _dot(wa2t, wt_s[EVEN_LR:EVEN_LR + LANES, :]).astype(BF16)
        for g in range(BRANCH // LANES):
            bs_s[:, g * LANES:(g + 1) * LANES] = jnp.broadcast_to(bsg_ref[g:g + 1, :], (LANES, SGU_CHUNK)).T

    @pl.when(pl.program_id(1) == 0)
    def _():
        st_s[...] = jnp.zeros_like(st_s)

    refs = (wt_s, wzt_s, ba_ref, gng_ref, slg_ref, slb_ref, ws_ref, bs_s, wout_s, lng_ref, lnb_ref)
    o_ref[0], st_s[...] = _even_block(x_ref, st_s[...], *refs)


def _even_block(x_ref, state, wt_ref, wzt_ref, ba_ref, gng_ref, slg_ref, slb_ref, ws_ref, bs_ref, wout_ref,
                lng_ref, lnb_ref):
    tb = x_ref.shape[1]
    xb = x_ref[0].astype(BF16)
    c_len = GLA_CHUNK
    t_len = SGU_CHUNK
    chunks = [slice(c * c_len, (c + 1) * c_len) for c in range(tb // c_len)]

    z = _dot_nt(xb, wzt_ref[...]) + ba_ref[...]
    q = _dot_nt(xb, wt_ref[EVEN_Q:EVEN_Q + QK, :])
    k = _dot_nt(xb, wt_ref[EVEN_K:EVEN_K + QK, :])
    la = _log_sigmoid(z) * (1.0 / GLA_TAU)
    v = _dot_nt(xb, wt_ref[EVEN_V:EVEN_V + BRANCH, :]).astype(BF16)

    row = lax.broadcasted_iota(jnp.int32, (LANES, 2 * LANES), 0)
    col = lax.broadcasted_iota(jnp.int32, (LANES, 2 * LANES), 1) % LANES
    tril2 = ((row >= col) & (row // c_len == col // c_len)).astype(BF16)
    la_hi = la.astype(BF16)
    la_lo = (la - la_hi.astype(F32)).astype(BF16)
    b = jnp.concatenate(
        [_dot(tril2, jnp.concatenate([la_hi[n * LANES:(n + 1) * LANES], la_lo[n * LANES:(n + 1) * LANES]],
                                     axis=0))
         for n in range(tb // LANES)], axis=0)

    u = _dot_nt(xb, wt_ref[EVEN_U:EVEN_U + BRANCH, :])
    q_dec = q * jnp.exp(b) * (DK ** -0.5)
    k_dec = k * jnp.exp(-b)
    masks = [_head_mask(h, True).astype(BF16) for h in range(HEADS)]
    q_dec_b = q_dec.astype(BF16)
    qms = [jnp.concatenate([q_dec_b[sl] * masks[h] for h in range(HEADS)], axis=0)
           for sl in chunks]
    sv = _dot_nt(xb, wt_ref[EVEN_SV:EVEN_SV + BRANCH, :])

    u = _gelu_tanh(u)
    kts, upds, dcols = [], [], []
    for c, sl in enumerate(chunks):
        kt = k_dec[sl].T.astype(BF16)
        kts.append(kt)
        upds.append(jnp.concatenate([_dot(kt[h * DK:(h + 1) * DK], v[sl, h * DV:(h + 1) * DV])
                                     for h in range(HEADS)], axis=0))
        b_last = b[(c + 1) * c_len - 1:(c + 1) * c_len, :]
        dcols.append(jnp.exp(jnp.broadcast_to(b_last, (LANES, QK)).T))
    g_a = _dot_nt(xb, wt_ref[EVEN_GA:EVEN_GA + BRANCH, :])
    sv = _gelu_tanh(sv)
    slg = slg_ref[...]
    slb = slb_ref[...]
    svn = jnp.concatenate(
        [_layer_norm_rows(sv[:, g * LANES:(g + 1) * LANES], slg[:, g * LANES:(g + 1) * LANES],
                          slb[:, g * LANES:(g + 1) * LANES]) for g in range(BRANCH // LANES)],
        axis=1).astype(BF16)
    g_b = _dot_nt(xb, wt_ref[EVEN_GB:EVEN_GB + BRANCH, :])

    states = []
    for upd, dcol in zip(upds, dcols):
        states.append(state.astype(BF16))
        state = dcol * (state + upd)

    trow = lax.broadcasted_iota(jnp.int32, (t_len, t_len), 0)
    tcol = lax.broadcasted_iota(jnp.int32, (t_len, t_len), 1)
    s_groups = []
    for g in range(BRANCH // LANES):
        w = jnp.where(trow >= tcol, ws_ref[g], 0.0).astype(BF16)
        cols = slice(g * LANES, (g + 1) * LANES)
        mixed = []
        for n in range(0, tb // t_len, 2):
            pair = _dot(w, jnp.concatenate([svn[n * t_len:(n + 1) * t_len, cols],
                                            svn[(n + 1) * t_len:(n + 2) * t_len, cols]], axis=1))
            mixed += [pair[:, :LANES], pair[:, LANES:]]
        s_groups.append(jnp.concatenate(mixed, axis=0))
    s = jnp.concatenate(s_groups, axis=1) + jnp.concatenate([bs_ref[...]] * (tb // t_len), axis=0)

    gng = gng_ref[...]
    crow = lax.broadcasted_iota(jnp.int32, (HEADS * c_len, c_len), 0) % c_len
    ccol = lax.broadcasted_iota(jnp.int32, (HEADS * c_len, c_len), 1)
    causal = crow >= ccol
    fill = jnp.zeros((QK, MXU_TILE - DV - c_len), BF16)
    fused = [_dot(qm, jnp.concatenate([s_b, kt, fill], axis=1)) for qm, s_b, kt in zip(qms, states, kts)]
    o_b = u * s * _silu(g_b)
    o_chunks = []
    for sl, f in zip(chunks, fused):
        o_inter = f[:, 0:DV]
        sc = jnp.where(causal, f[:, DV:DV + c_len], 0.0).astype(BF16)
        o_heads = []
        for h in range(HEADS):
            rows = slice(h * c_len, (h + 1) * c_len)
            o = _dot(sc[rows], v[sl, h * DV:(h + 1) * DV]) + o_inter[rows]
            o = o * lax.rsqrt(jnp.mean(o * o, axis=-1, keepdims=True) + LN_EPS)
            o_heads.append(o * gng[:, h * DV:(h + 1) * DV])
        o_chunks.append(jnp.concatenate(o_heads, axis=1))
    o_a = jnp.concatenate(o_chunks, axis=0) * _silu(g_a)
    return _project_and_norm(x_ref, o_a, o_b, wout_ref, lng_ref, lnb_ref), state


def _odd_kernel(run_ref, x_ref, pos_ref, w_hbm, perm_ref, inv_ref, ctab_ref, stab_ref, dmat_ref, wint_ref,
                wst_ref, dec_ref, rng_ref, wpool_ref, psc_ref, wout_hbm, lng_ref, lnb_ref, o_ref, st_s, pad_s,
                cos_s, sin_s, wqk_s, wp_ref, w_ref, wout_ref, wstage_s, ostage_s, sem):
    tb = x_ref.shape[1]
    seq = pl.program_id(0)
    i = pl.program_id(1)

    @pl.when((seq == 0) & (i == 0))
    def _():
        _load_as_bf16(w_hbm, w_ref, wstage_s, sem)
        _load_as_bf16(wout_hbm, wout_ref, ostage_s, sem, 1.0 / DN_ALPHA)
        wqk_s[...] = _dot(w_ref[:, 0:2 * QK], perm_ref[...]).astype(BF16)
        zero = jnp.zeros((LANES, LANES), F32)
        for n in range(len(POOL_WINDOWS) // 2):
            wp_ref[n] = jnp.concatenate(
                [jnp.concatenate([wpool_ref[2 * n], zero], axis=1),
                 jnp.concatenate([zero, wpool_ref[2 * n + 1]], axis=1)], axis=0).astype(BF16)

    @pl.when(i == 0)
    def _():
        st_s[...] = jnp.zeros_like(st_s)
        pad_s[0:POOL_PAD, :] = jnp.zeros((POOL_PAD, BRANCH), F32)

    c_len = RET_CHUNK
    half = DK // 2
    chunks = [slice(c * c_len, (c + 1) * c_len) for c in range(tb // c_len)]
    c_q, c_k, c_v, c_gc, c_p, c_gd = [(o, o + n) for o, n in zip(
        (0, QK, 2 * QK, 2 * QK + BRANCH, 2 * QK + 2 * BRANCH, 2 * QK + 3 * BRANCH),
        (QK, QK, BRANCH, BRANCH, BRANCH, BRANCH))]

    consecutive = run_ref[seq, i] == 1

    @pl.when(consecutive)
    def _():
        a0 = pos_ref[0, :, 0:1].astype(F32) * inv_ref[...]
        c0 = jnp.cos(a0)
        s0 = jnp.sin(a0)
        cos_s[...] = c0 * ctab_ref[...] - s0 * stab_ref[...]
        sin_s[...] = s0 * ctab_ref[...] + c0 * stab_ref[...]

    @pl.when(jnp.logical_not(consecutive))
    def _():
        pos_rows = jnp.broadcast_to(pos_ref[0].astype(F32), (LANES, tb)).T
        ang = pos_rows * inv_ref[...]
        cos_s[...] = jnp.cos(ang)
        sin_s[...] = jnp.sin(ang)

    xb = x_ref[0].astype(BF16)

    q = _dot(xb, wqk_s[:, c_q[0]:c_q[1]]) * (DK ** -0.5)
    k = _dot(xb, wqk_s[:, c_k[0]:c_k[1]])
    cos = cos_s[...]
    sin = sin_s[...]
    v = _dot(xb, w_ref[:, c_v[0]:c_v[1]]).astype(BF16)
    p = _dot(xb, w_ref[:, c_p[0]:c_p[1]])
    q1, q2 = q[:, :LANES], q[:, LANES:]
    k1, k2 = k[:, :LANES], k[:, LANES:]
    qr = jnp.concatenate([q1 * cos - q2 * sin, q1 * sin + q2 * cos], axis=1)
    kr = jnp.concatenate([k1 * cos - k2 * sin, k1 * sin + k2 * cos], axis=1)
    masks = [_head_mask(h, False).astype(BF16) for h in range(HEADS)]
    qr_b = qr.astype(BF16)
    qms = [jnp.concatenate([qr_b[sl] * masks[h] for h in range(HEADS)], axis=0)
           for sl in chunks]
    g_c = _dot(xb, w_ref[:, c_gc[0]:c_gc[1]])

    pad_s[POOL_PAD:POOL_PAD + tb, :] = p
    t_top = i * tb + lax.broadcasted_iota(jnp.int32, (POOL_PAD, LANES), 0)
    pooled = []
    for g, win in enumerate(POOL_WINDOWS):
        cols = slice(g * LANES, (g + 1) * LANES)
        acc = pad_s[:, cols]
        shift = 1
        while shift < win:
            acc = acc + pltpu.roll(acc, shift, axis=0)
            shift *= 2
        cnt_top = jnp.minimum(t_top + 1, win).astype(F32)
        mean = jnp.concatenate([acc[POOL_PAD:2 * POOL_PAD] / cnt_top, acc[2 * POOL_PAD:] * (1.0 / win)],
                               axis=0)
        pooled.append((mean - p[:, cols]).astype(BF16))
    tail = pad_s[tb:tb + POOL_PAD, :]
    pad_s[0:POOL_PAD, :] = tail
    kts, upds = [], []
    for sl in chunks:
        kt = (kr[sl] * wst_ref[...]).T.astype(BF16)
        kts.append(kt)
        upd = [_dot(jnp.concatenate([kt[h * half:(h + 1) * half],
                                     kt[LANES + h * half:LANES + (h + 1) * half]], axis=0),
                    v[sl, h * DV:(h + 1) * DV]) for h in range(HEADS)]
        upds.append(jnp.concatenate([u[:half] for u in upd] + [u[half:] for u in upd], axis=0))
    g_d = _dot(xb, w_ref[:, c_gd[0]:c_gd[1]])

    state = st_s[...]
    states = []
    for upd in upds:
        states.append(state.astype(BF16))
        state = dec_ref[...] * state + upd
    st_s[...] = state
    yd = jnp.concatenate([_dot(jnp.concatenate([pooled[2 * n], pooled[2 * n + 1]], axis=1), wp_ref[n])
                          for n in range(len(POOL_WINDOWS) // 2)], axis=1)
    fused = [_dot(qm, jnp.concatenate([s_b, kt], axis=1)) for qm, s_b, kt in zip(qms, states, kts)]
    o_d = yd * psc_ref[...] * _silu(g_d)
    rng = rng_ref[...]
    o_chunks = []
    for sl, f in zip(chunks, fused):
        o_inter = f[:, 0:DV] * wint_ref[...]
        sc = (f[:, DV:DV + c_len] * dmat_ref[...]).astype(BF16)
        o_heads = []
        for h in range(HEADS):
            rows = slice(h * c_len, (h + 1) * c_len)
            o = _dot(sc[rows], v[sl, h * DV:(h + 1) * DV]) + o_inter[rows]
            mu = jnp.mean(o, axis=-1, keepdims=True)
            oc = o - mu
            var = jnp.mean(oc * oc, axis=-1, keepdims=True)
            o_heads.append(oc * lax.rsqrt(var + LN_EPS) * rng[:, h * DV:(h + 1) * DV])
        o_chunks.append(jnp.concatenate(o_heads, axis=1))
    o_c = jnp.concatenate(o_chunks, axis=0) * _silu(g_c)
    o_ref[0] = _project_and_norm(x_ref, o_c, o_d, wout_ref, lng_ref, lnb_ref)


def _full_spec(a):
    nd = a.ndim
    return pl.BlockSpec(a.shape, lambda *_, _nd=nd: (0,) * _nd)


def _compiler_params():
    return pltpu.CompilerParams(dimension_semantics=("arbitrary", "arbitrary"),
                                vmem_limit_bytes=VMEM_LIMIT_BYTES)


def _row(a):
    return a.reshape(1, -1).astype(F32)


def _hbm_spec():
    return pl.BlockSpec(memory_space=pl.ANY)


def _stage(w):
    rows = min(w.shape[0], STAGE_BYTES // (4 * w.shape[1]) // SUBLANES * SUBLANES)
    return pltpu.VMEM((STAGE_SLOTS, rows, w.shape[1]), F32)


def _even_layer(x, w_in, w_a2, b_a, gla_norm_g, sgu_ln_g, sgu_ln_b, w_s, b_s, w_out, ln_g, ln_b):
    B, S, D = x.shape
    tb = TOKEN_BLOCK
    w_in_t = w_in.T
    params = (w_in_t, w_a2.astype(F32), _row(b_a), _row(gla_norm_g), _row(sgu_ln_g), _row(sgu_ln_b),
              w_s.astype(F32), b_s.astype(F32), w_out, _row(ln_g), _row(ln_b))
    in_hbm = (w_in_t, w_out)
    tok_spec = pl.BlockSpec((1, tb, D), lambda b, i: (b, i, 0))
    return pl.pallas_call(
        _even_kernel,
        out_shape=jax.ShapeDtypeStruct((B, S, D), F32),
        grid=(B, S // tb),
        in_specs=[tok_spec] + [_hbm_spec() if any(p is h for h in in_hbm) else _full_spec(p) for p in params],
        out_specs=tok_spec,
        scratch_shapes=[
            pltpu.VMEM((QK, DV), F32),
            pltpu.VMEM((QK, D), BF16),
            pltpu.VMEM((SGU_CHUNK, BRANCH), F32),
            pltpu.VMEM(w_in_t.shape, BF16),
            pltpu.VMEM(w_out.shape, BF16),
            _stage(w_in_t),
            _stage(w_out),
            pltpu.SemaphoreType.DMA((STAGE_SLOTS,)),
        ],
        compiler_params=_compiler_params(),
        name="even_layer_gla_sgu",
    )(x, *params)


def _retention_tables():
    c = RET_CHUNK
    log_gamma = np.log(1.0 - 2.0 ** (-5.0 - np.arange(HEADS, dtype=np.float64)))
    idx = np.arange(c, dtype=np.float64)
    rel = idx[:, None] - idx[None, :]
    dmat = np.where(rel >= 0, np.exp((idx[:, None] - (c - 1.0))[None] * log_gamma[:, None, None]), 0.0)
    w_inter = np.exp((idx + 1.0)[None] * log_gamma[:, None])
    w_state = np.exp((c - 1.0 - idx)[None] * log_gamma[:, None])
    lane_head = (np.arange(QK) % LANES) // (DK // 2)
    dmat = dmat.reshape(HEADS * c, c)
    wint = np.broadcast_to(w_inter[:, :, None], (HEADS, c, DV)).reshape(HEADS * c, DV)
    wst = w_state[lane_head, :].T
    dec = np.broadcast_to(np.exp(c * log_gamma)[lane_head][:, None], (QK, DV))
    inv = ROPE_BASE ** (-np.arange(DK // 2, dtype=np.float64) / (DK // 2))
    inv = np.tile(inv, HEADS)[None, :].astype(np.float32)
    off = np.arange(TOKEN_BLOCK, dtype=np.float64)[:, None] * inv.astype(np.float64)
    f = lambda a: jnp.asarray(np.ascontiguousarray(a), F32)
    return f(inv), f(np.cos(off)), f(np.sin(off)), f(dmat), f(wint), f(wst), f(dec)


def _odd_layer(x, positions, w_in, ret_norm_g, w_pool, pool_scale, w_out, ln_g, ln_b):
    B, S, D = x.shape
    tb = TOKEN_BLOCK
    j = np.arange(2 * QK)
    src = (j // QK) * QK + ((j % LANES) // (DK // 2)) * DK + ((j % QK) // LANES) * (DK // 2) + j % (DK // 2)
    perm = jnp.asarray(src[None, :] == np.arange(2 * QK)[:, None], BF16)
    params = (w_in, perm, *_retention_tables(), _row(ret_norm_g), w_pool.astype(F32), _row(pool_scale), w_out,
              _row(ln_g), _row(ln_b))
    in_hbm = (w_in, w_out)
    pos_blocks = positions.reshape(B, S // tb, tb)
    runs = jnp.all(pos_blocks == pos_blocks[:, :, :1] + jnp.arange(tb, dtype=positions.dtype), axis=-1)
    tok_spec = pl.BlockSpec((1, tb, D), lambda b, i: (b, i, 0))
    pos_spec = pl.BlockSpec((1, 1, tb), lambda b, i: (b, 0, i))
    return pl.pallas_call(
        _odd_kernel,
        out_shape=jax.ShapeDtypeStruct((B, S, D), F32),
        grid=(B, S // tb),
        in_specs=[pl.BlockSpec(memory_space=pltpu.SMEM), tok_spec, pos_spec]
                 + [_hbm_spec() if any(p is h for h in in_hbm) else _full_spec(p) for p in params],
        out_specs=tok_spec,
        scratch_shapes=[
            pltpu.VMEM((QK, DV), F32),
            pltpu.VMEM((POOL_PAD + tb, BRANCH), F32),
            pltpu.VMEM((tb, LANES), F32),
            pltpu.VMEM((tb, LANES), F32),
            pltpu.VMEM((D, 2 * QK), BF16),
            pltpu.VMEM((len(POOL_WINDOWS) // 2, 2 * LANES, 2 * LANES), BF16),
            pltpu.VMEM(w_in.shape, BF16),
            pltpu.VMEM(w_out.shape, BF16),
            _stage(w_in),
            _stage(w_out),
            pltpu.SemaphoreType.DMA((STAGE_SLOTS,)),
        ],
        compiler_params=_compiler_params(),
        name="odd_layer_retention_pool",
    )(runs.astype(jnp.int32), x, positions.reshape(B, 1, S), *params)


def kernel(x, positions, l0_w_in, l0_w_a2, l0_b_a, l0_gla_norm_g, l0_sgu_ln_g, l0_sgu_ln_b, l0_w_s, l0_b_s, l0_w_out, l0_ln_g, l0_ln_b, l1_w_in, l1_ret_norm_g, l1_w_pool, l1_pool_scale, l1_w_out, l1_ln_g, l1_ln_b):
    x = _even_layer(x, l0_w_in, l0_w_a2, l0_b_a, l0_gla_norm_g, l0_sgu_ln_g, l0_sgu_ln_b, l0_w_s, l0_b_s,
                    l0_w_out, l0_ln_g, l0_ln_b)
    return _odd_layer(x, positions, l1_w_in, l1_ret_norm_g, l1_w_pool, l1_pool_scale, l1_w_out, l1_ln_g, l1_ln_b)
```

```python
import jax
import jax.numpy as jnp
import numpy as np
from jax import lax
from jax.experimental import pallas as pl
from jax.experimental.pallas import tpu as pltpu

F32 = jnp.float32
BF16 = jnp.bfloat16

D_MODEL = 1024
BRANCH = 512
HEADS = 4
DK = 64
DV = 128
QK = HEADS * DK
GLA_RANK = 16
GLA_TAU = 16.0
GLA_CHUNK = 64
SGU_CHUNK = 128
RET_CHUNK = 128
ROPE_BASE = 10000.0
POOL_WINDOWS = (2, 4, 8, 16)
POOL_PAD = 16
DN_ALPHA = 4.0 ** 0.25
LN_EPS = 1e-5
LANES = 128
SUBLANES = 8
MXU_TILE = 256
VMEM_LIMIT_BYTES = 56 * 1024 * 1024
TOKEN_BLOCK = 1024
NORM_ROW_GROUPS = 4
STAGE_BYTES = 1024 * 1024
STAGE_SLOTS = 4
EVEN_Q, EVEN_K, EVEN_V, EVEN_GA, EVEN_LR = 0, QK, 2 * QK, 2 * QK + BRANCH, 2 * QK + 2 * BRANCH
EVEN_U, EVEN_SV, EVEN_GB = EVEN_LR + GLA_RANK, EVEN_LR + GLA_RANK + BRANCH, EVEN_LR + GLA_RANK + 2 * BRANCH


def _dot(a, b):
    return jnp.dot(a, b, preferred_element_type=F32)


def _dot_nt(a, b):
    return lax.dot_general(a, b, (((1,), (1,)), ((), ())), preferred_element_type=F32)


def _silu(x):
    hx = 0.5 * x
    return hx + hx * jnp.tanh(hx)


def _gelu_tanh(x):
    c = np.sqrt(2.0 / np.pi)
    hx = 0.5 * x
    return hx + hx * jnp.tanh(x * (np.float32(c) + np.float32(c * 0.044715) * (x * x)))


def _log_sigmoid(z):
    return jnp.minimum(z, 0.0) - jnp.log(1.0 + jnp.exp(-jnp.abs(z)))


def _layer_norm_rows(r, g, b, eps=LN_EPS):
    mu = jnp.mean(r, axis=-1, keepdims=True)
    c = r - mu
    var = jnp.mean(c * c, axis=-1, keepdims=True)
    return c * lax.rsqrt(var + eps) * g + b


def _deepnorm(x, y_scaled, g, b):
    return _layer_norm_rows(x + y_scaled, g, b, LN_EPS / DN_ALPHA ** 2)


def _project_and_norm(x_ref, o_first, o_second, wout_ref, lng_ref, lnb_ref):
    rows_per_group = x_ref.shape[1] // NORM_ROW_GROUPS
    outs = []
    for n in range(NORM_ROW_GROUPS):
        rows = slice(n * rows_per_group, (n + 1) * rows_per_group)
        y = (_dot(o_second[rows].astype(BF16), wout_ref[BRANCH:2 * BRANCH, :])
             + _dot(o_first[rows].astype(BF16), wout_ref[0:BRANCH, :]))
        outs.append(_deepnorm(x_ref[0, rows, :], y, lng_ref[...], lnb_ref[...]))
    return jnp.concatenate(outs, axis=0)


def _head_mask(h, natural):
    lane = lax.broadcasted_iota(jnp.int32, (1, QK), 1)
    head = lane // DK if natural else (lane % LANES) // (DK // 2)
    return (head == h).astype(F32)


def _load_as_bf16(src_hbm, dst_ref, stage_ref, sem, scale=None):
    rows = src_hbm.shape[0]
    slots, per_copy = stage_ref.shape[0], stage_ref.shape[1]
    starts = list(range(0, rows, per_copy))

    def copy(k):
        n_rows = min(per_copy, rows - starts[k])
        return pltpu.make_async_copy(src_hbm.at[pl.ds(starts[k], n_rows), :],
                                     stage_ref.at[k % slots, pl.ds(0, n_rows), :], sem.at[k % slots])

    for k in range(min(slots - 1, len(starts))):
        copy(k).start()
    for k, start in enumerate(starts):
        if k + slots - 1 < len(starts):
            copy(k + slots - 1).start()
        copy(k).wait()
        n_rows = min(per_copy, rows - start)
        block = stage_ref[k % slots, 0:n_rows, :]
        if scale is not None:
            block = block * scale
        dst_ref[start:start + n_rows, :] = block.astype(BF16)


def _even_kernel(x_ref, wt_hbm, wa2_ref, ba_ref, gng_ref, slg_ref, slb_ref, ws_ref, bsg_ref, wout_hbm,
                 lng_ref, lnb_ref, o_ref, st_s, wzt_s, bs_s, wt_s, wout_s, wstage_s, ostage_s, sem):
    @pl.when((pl.program_id(0) == 0) & (pl.program_id(1) == 0))
    def _():
        _load_as_bf16(wt_hbm, wt_s, wstage_s, sem)
        _load_as_bf16(wout_hbm, wout_s, ostage_s, sem, 1.0 / DN_ALPHA)
        wa2t = jnp.concatenate([wa2_ref[...], jnp.zeros((LANES - GLA_RANK, QK), F32)], axis=0).T.astype(BF16)
        wzt_s[...] = _dot(wa2t, wt_s[EVEN_LR:EVEN_LR + LANES, :]).astype(BF16)
        for g in range(BRANCH // LANES):
            bs_s[:, g * LANES:(g + 1) * LANES] = jnp.broadcast_to(bsg_ref[g:g + 1, :], (LANES, SGU_CHUNK)).T

    @pl.when(pl.program_id(1) == 0)
    def _():
        st_s[...] = jnp.zeros_like(st_s)

    refs = (wt_s, wzt_s, ba_ref, gng_ref, slg_ref, slb_ref, ws_ref, bs_s, wout_s, lng_ref, lnb_ref)
    o_ref[0], st_s[...] = _even_block(x_ref, st_s[...], *refs)


def _even_block(x_ref, state, wt_ref, wzt_ref, ba_ref, gng_ref, slg_ref, slb_ref, ws_ref, bs_ref, wout_ref,
                lng_ref, lnb_ref):
    tb = x_ref.shape[1]
    xb = x_ref[0].astype(BF16)
    c_len = GLA_CHUNK
    t_len = SGU_CHUNK
    chunks = [slice(c * c_len, (c + 1) * c_len) for c in range(tb // c_len)]

    z = _dot_nt(xb, wzt_ref[...]) + ba_ref[...]
    q = _dot_nt(xb, wt_ref[EVEN_Q:EVEN_Q + QK, :])
    k = _dot_nt(xb, wt_ref[EVEN_K:EVEN_K + QK, :])
    la = _log_sigmoid(z) * (1.0 / GLA_TAU)
    v = _dot_nt(xb, wt_ref[EVEN_V:EVEN_V + BRANCH, :]).astype(BF16)

    row = lax.broadcasted_iota(jnp.int32, (LANES, 2 * LANES), 0)
    col = lax.broadcasted_iota(jnp.int32, (LANES, 2 * LANES), 1) % LANES
    tril2 = ((row >= col) & (row // c_len == col // c_len)).astype(BF16)
    la_hi = la.astype(BF16)
    la_lo = (la - la_hi.astype(F32)).astype(BF16)
    b = jnp.concatenate(
        [_dot(tril2, jnp.concatenate([la_hi[n * LANES:(n + 1) * LANES], la_lo[n * LANES:(n + 1) * LANES]],
                                     axis=0))
         for n in range(tb // LANES)], axis=0)

    u = _dot_nt(xb, wt_ref[EVEN_U:EVEN_U + BRANCH, :])
    q_dec = q * jnp.exp(b) * (DK ** -0.5)
    k_dec = k * jnp.exp(-b)
    masks = [_head_mask(h, True).astype(BF16) for h in range(HEADS)]
    q_dec_b = q_dec.astype(BF16)
    qms = [jnp.concatenate([q_dec_b[sl] * masks[h] for h in range(HEADS)], axis=0)
           for sl in chunks]
    sv = _dot_nt(xb, wt_ref[EVEN_SV:EVEN_SV + BRANCH, :])

    u = _gelu_tanh(u)
    kts, upds, dcols = [], [], []
    for c, sl in enumerate(chunks):
        kt = k_dec[sl].T.astype(BF16)
        kts.append(kt)
        upds.append(jnp.concatenate([_dot(kt[h * DK:(h + 1) * DK], v[sl, h * DV:(h + 1) * DV])
                                     for h in range(HEADS)], axis=0))
        b_last = b[(c + 1) * c_len - 1:(c + 1) * c_len, :]
        dcols.append(jnp.exp(jnp.broadcast_to(b_last, (LANES, QK)).T))
    g_a = _dot_nt(xb, wt_ref[EVEN_GA:EVEN_GA + BRANCH, :])
    sv = _gelu_tanh(sv)
    slg = slg_ref[...]
    slb = slb_ref[...]
    svn = jnp.concatenate(
        [_layer_norm_rows(sv[:, g * LANES:(g + 1) * LANES], slg[:, g * LANES:(g + 1) * LANES],
                          slb[:, g * LANES:(g + 1) * LANES]) for g in range(BRANCH // LANES)],
        axis=1).astype(BF16)
    g_b = _dot_nt(xb, wt_ref[EVEN_GB:EVEN_GB + BRANCH, :])

    states = []
    for upd, dcol in zip(upds, dcols):
        states.append(state.astype(BF16))
        state = dcol * (state + upd)

    trow = lax.broadcasted_iota(jnp.int32, (t_len, t_len), 0)
    tcol = lax.broadcasted_iota(jnp.int32, (t_len, t_len), 1)
    s_groups = []
    for g in range(BRANCH // LANES):
        w = jnp.where(trow >= tcol, ws_ref[g], 0.0).astype(BF16)
        cols = slice(g * LANES, (g + 1) * LANES)
        mixed = []
        for n in range(0, tb // t_len, 2):
            pair = _dot(w, jnp.concatenate([svn[n * t_len:(n + 1) * t_len, cols],
                                            svn[(n + 1) * t_len:(n + 2) * t_len, cols]], axis=1))
            mixed += [pair[:, :LANES], pair[:, LANES:]]
        s_groups.append(jnp.concatenate(mixed, axis=0))
    s = jnp.concatenate(s_groups, axis=1) + jnp.concatenate([bs_ref[...]] * (tb // t_len), axis=0)

    gng = gng_ref[...]
    crow = lax.broadcasted_iota(jnp.int32, (HEADS * c_len, c_len), 0) % c_len
    ccol = lax.broadcasted_iota(jnp.int32, (HEADS * c_len, c_len), 1)
    causal = crow >= ccol
    fill = jnp.zeros((QK, MXU_TILE - DV - c_len), BF16)
    fused = [_dot(qm, jnp.concatenate([s_b, kt, fill], axis=1)) for qm, s_b, kt in zip(qms, states, kts)]
    o_b = u * s * _silu(g_b)
    o_chunks = []
    for sl, f in zip(chunks, fused):
        o_inter = f[:, 0:DV]
        sc = jnp.where(causal, f[:, DV:DV + c_len], 0.0).astype(BF16)
        o_heads = []
        for h in range(HEADS):
            rows = slice(h * c_len, (h + 1) * c_len)
            o = _dot(sc[rows], v[sl, h * DV:(h + 1) * DV]) + o_inter[rows]
            o = o * lax.rsqrt(jnp.mean(o * o, axis=-1, keepdims=True) + LN_EPS)
            o_heads.append(o * gng[:, h * DV:(h + 1) * DV])
        o_chunks.append(jnp.concatenate(o_heads, axis=1))
    o_a = jnp.concatenate(o_chunks, axis=0) * _silu(g_a)
    return _project_and_norm(x_ref, o_a, o_b, wout_ref, lng_ref, lnb_ref), state


def _odd_kernel(run_ref, x_ref, pos_hbm, w_hbm, perm_ref, inv_ref, ctab_ref, stab_ref, dmat_ref, wint_ref,
                wst_ref, dec_ref, rng_ref, wpool_ref, psc_ref, wout_hbm, lng_ref, lnb_ref, o_ref, st_s, pad_s,
                cos_s, sin_s, pos_s, wqk_s, wp_ref, w_ref, wout_ref, wstage_s, ostage_s, sem):
    tb = x_ref.shape[1]
    seq = pl.program_id(0)
    i = pl.program_id(1)

    @pl.when((seq == 0) & (i == 0))
    def _():
        _load_as_bf16(w_hbm, w_ref, wstage_s, sem)
        _load_as_bf16(wout_hbm, wout_ref, ostage_s, sem, 1.0 / DN_ALPHA)
        wqk_s[...] = _dot(w_ref[:, 0:2 * QK], perm_ref[...]).astype(BF16)
        zero = jnp.zeros((LANES, LANES), F32)
        for n in range(len(POOL_WINDOWS) // 2):
            wp_ref[n] = jnp.concatenate(
                [jnp.concatenate([wpool_ref[2 * n], zero], axis=1),
                 jnp.concatenate([zero, wpool_ref[2 * n + 1]], axis=1)], axis=0).astype(BF16)

    @pl.when(i == 0)
    def _():
        st_s[...] = jnp.zeros_like(st_s)
        pad_s[0:POOL_PAD, :] = jnp.zeros((POOL_PAD, BRANCH), F32)

    c_len = RET_CHUNK
    half = DK // 2
    chunks = [slice(c * c_len, (c + 1) * c_len) for c in range(tb // c_len)]
    c_q, c_k, c_v, c_gc, c_p, c_gd = [(o, o + n) for o, n in zip(
        (0, QK, 2 * QK, 2 * QK + BRANCH, 2 * QK + 2 * BRANCH, 2 * QK + 3 * BRANCH),
        (QK, QK, BRANCH, BRANCH, BRANCH, BRANCH))]

    consecutive = run_ref[1, seq, i] == 1

    @pl.when(consecutive)
    def _():
        a0 = run_ref[0, seq, i].astype(F32) * inv_ref[...]
        c0 = jnp.cos(a0)
        s0 = jnp.sin(a0)
        cos_s[...] = c0 * ctab_ref[...] - s0 * stab_ref[...]
        sin_s[...] = s0 * ctab_ref[...] + c0 * stab_ref[...]

    @pl.when(jnp.logical_not(consecutive))
    def _():
        copy = pltpu.make_async_copy(pos_hbm.at[seq, :, pl.ds(pl.multiple_of(i * tb, tb), tb)], pos_s, sem.at[0])
        copy.start()
        copy.wait()
        pos_rows = jnp.broadcast_to(pos_s[...].astype(F32), (LANES, tb)).T
        ang = pos_rows * inv_ref[...]
        cos_s[...] = jnp.cos(ang)
        sin_s[...] = jnp.sin(ang)

    xb = x_ref[0].astype(BF16)

    q = _dot(xb, wqk_s[:, c_q[0]:c_q[1]]) * (DK ** -0.5)
    k = _dot(xb, wqk_s[:, c_k[0]:c_k[1]])
    cos = cos_s[...]
    sin = sin_s[...]
    v = _dot(xb, w_ref[:, c_v[0]:c_v[1]]).astype(BF16)
    p = _dot(xb, w_ref[:, c_p[0]:c_p[1]])
    q1, q2 = q[:, :LANES], q[:, LANES:]
    k1, k2 = k[:, :LANES], k[:, LANES:]
    qr = jnp.concatenate([q1 * cos - q2 * sin, q1 * sin + q2 * cos], axis=1)
    kr = jnp.concatenate([k1 * cos - k2 * sin, k1 * sin + k2 * cos], axis=1)
    masks = [_head_mask(h, False).astype(BF16) for h in range(HEADS)]
    qr_b = qr.astype(BF16)
    qms = [jnp.concatenate([qr_b[sl] * masks[h] for h in range(HEADS)], axis=0)
           for sl in chunks]
    g_c = _dot(xb, w_ref[:, c_gc[0]:c_gc[1]])

    pad_s[POOL_PAD:POOL_PAD + tb, :] = p
    t_top = i * tb + lax.broadcasted_iota(jnp.int32, (POOL_PAD, LANES), 0)
    pooled = []
    for g, win in enumerate(POOL_WINDOWS):
        cols = slice(g * LANES, (g + 1) * LANES)
        acc = pad_s[:, cols]
        shift = 1
        while shift < win:
            acc = acc + pltpu.roll(acc, shift, axis=0)
            shift *= 2
        cnt_top = jnp.minimum(t_top + 1, win).astype(F32)
        mean = jnp.concatenate([acc[POOL_PAD:2 * POOL_PAD] / cnt_top, acc[2 * POOL_PAD:] * (1.0 / win)],
                               axis=0)
        pooled.append((mean - p[:, cols]).astype(BF16))
    tail = pad_s[tb:tb + POOL_PAD, :]
    pad_s[0:POOL_PAD, :] = tail
    kts, upds = [], []
    for sl in chunks:
        kt = (kr[sl] * wst_ref[...]).T.astype(BF16)
        kts.append(kt)
        upd = [_dot(jnp.concatenate([kt[h * half:(h + 1) * half],
                                     kt[LANES + h * half:LANES + (h + 1) * half]], axis=0),
                    v[sl, h * DV:(h + 1) * DV]) for h in range(HEADS)]
        upds.append(jnp.concatenate([u[:half] for u in upd] + [u[half:] for u in upd], axis=0))
    g_d = _dot(xb, w_ref[:, c_gd[0]:c_gd[1]])

    state = st_s[...]
    states = []
    for upd in upds:
        states.append(state.astype(BF16))
        state = dec_ref[...] * state + upd
    st_s[...] = state
    yd = jnp.concatenate([_dot(jnp.concatenate([pooled[2 * n], pooled[2 * n + 1]], axis=1), wp_ref[n])
                          for n in range(len(POOL_WINDOWS) // 2)], axis=1)
    fused = [_dot(qm, jnp.concatenate([s_b, kt], axis=1)) for qm, s_b, kt in zip(qms, states, kts)]
    o_d = yd * psc_ref[...] * _silu(g_d)
    rng = rng_ref[...]
    o_chunks = []
    for sl, f in zip(chunks, fused):
        o_inter = f[:, 0:DV] * wint_ref[...]
        sc = (f[:, DV:DV + c_len] * dmat_ref[...]).astype(BF16)
        o_heads = []
        for h in range(HEADS):
            rows = slice(h * c_len, (h + 1) * c_len)
            o = _dot(sc[rows], v[sl, h * DV:(h + 1) * DV]) + o_inter[rows]
            mu = jnp.mean(o, axis=-1, keepdims=True)
            oc = o - mu
            var = jnp.mean(oc * oc, axis=-1, keepdims=True)
            o_heads.append(oc * lax.rsqrt(var + LN_EPS) * rng[:, h * DV:(h + 1) * DV])
        o_chunks.append(jnp.concatenate(o_heads, axis=1))
    o_c = jnp.concatenate(o_chunks, axis=0) * _silu(g_c)
    o_ref[0] = _project_and_norm(x_ref, o_c, o_d, wout_ref, lng_ref, lnb_ref)


def _full_spec(a):
    nd = a.ndim
    return pl.BlockSpec(a.shape, lambda *_, _nd=nd: (0,) * _nd)


def _compiler_params():
    return pltpu.CompilerParams(dimension_semantics=("arbitrary", "arbitrary"),
                                vmem_limit_bytes=VMEM_LIMIT_BYTES)


def _row(a):
    return a.reshape(1, -1).astype(F32)


def _hbm_spec():
    return pl.BlockSpec(memory_space=pl.ANY)


def _stage(w):
    rows = min(w.shape[0], STAGE_BYTES // (4 * w.shape[1]) // SUBLANES * SUBLANES)
    return pltpu.VMEM((STAGE_SLOTS, rows, w.shape[1]), F32)


def _even_layer(x, w_in, w_a2, b_a, gla_norm_g, sgu_ln_g, sgu_ln_b, w_s, b_s, w_out, ln_g, ln_b):
    B, S, D = x.shape
    tb = TOKEN_BLOCK
    w_in_t = w_in.T
    params = (w_in_t, w_a2.astype(F32), _row(b_a), _row(gla_norm_g), _row(sgu_ln_g), _row(sgu_ln_b),
              w_s.astype(F32), b_s.astype(F32), w_out, _row(ln_g), _row(ln_b))
    in_hbm = (w_in_t, w_out)
    tok_spec = pl.BlockSpec((1, tb, D), lambda b, i: (b, i, 0))
    return pl.pallas_call(
        _even_kernel,
        out_shape=jax.ShapeDtypeStruct((B, S, D), F32),
        grid=(B, S // tb),
        in_specs=[tok_spec] + [_hbm_spec() if any(p is h for h in in_hbm) else _full_spec(p) for p in params],
        out_specs=tok_spec,
        scratch_shapes=[
            pltpu.VMEM((QK, DV), F32),
            pltpu.VMEM((QK, D), BF16),
            pltpu.VMEM((SGU_CHUNK, BRANCH), F32),
            pltpu.VMEM(w_in_t.shape, BF16),
            pltpu.VMEM(w_out.shape, BF16),
            _stage(w_in_t),
            _stage(w_out),
            pltpu.SemaphoreType.DMA((STAGE_SLOTS,)),
        ],
        compiler_params=_compiler_params(),
        name="even_layer_gla_sgu",
    )(x, *params)


def _retention_tables():
    c = RET_CHUNK
    log_gamma = np.log(1.0 - 2.0 ** (-5.0 - np.arange(HEADS, dtype=np.float64)))
    idx = np.arange(c, dtype=np.float64)
    rel = idx[:, None] - idx[None, :]
    dmat = np.where(rel >= 0, np.exp((idx[:, None] - (c - 1.0))[None] * log_gamma[:, None, None]), 0.0)
    w_inter = np.exp((idx + 1.0)[None] * log_gamma[:, None])
    w_state = np.exp((c - 1.0 - idx)[None] * log_gamma[:, None])
    lane_head = (np.arange(QK) % LANES) // (DK // 2)
    dmat = dmat.reshape(HEADS * c, c)
    wint = np.broadcast_to(w_inter[:, :, None], (HEADS, c, DV)).reshape(HEADS * c, DV)
    wst = w_state[lane_head, :].T
    dec = np.broadcast_to(np.exp(c * log_gamma)[lane_head][:, None], (QK, DV))
    inv = ROPE_BASE ** (-np.arange(DK // 2, dtype=np.float64) / (DK // 2))
    inv = np.tile(inv, HEADS)[None, :].astype(np.float32)
    off = np.arange(TOKEN_BLOCK, dtype=np.float64)[:, None] * inv.astype(np.float64)
    f = lambda a: jnp.asarray(np.ascontiguousarray(a), F32)
    return f(inv), f(np.cos(off)), f(np.sin(off)), f(dmat), f(wint), f(wst), f(dec)


def _odd_layer(x, positions, w_in, ret_norm_g, w_pool, pool_scale, w_out, ln_g, ln_b):
    B, S, D = x.shape
    tb = TOKEN_BLOCK
    j = np.arange(2 * QK)
    src = (j // QK) * QK + ((j % LANES) // (DK // 2)) * DK + ((j % QK) // LANES) * (DK // 2) + j % (DK // 2)
    perm = jnp.asarray(src[None, :] == np.arange(2 * QK)[:, None], BF16)
    params = (w_in, perm, *_retention_tables(), _row(ret_norm_g), w_pool.astype(F32), _row(pool_scale), w_out,
              _row(ln_g), _row(ln_b))
    in_hbm = (w_in, w_out)
    pos_blocks = positions.reshape(B, S // tb, tb)
    first = pos_blocks[:, :, 0]
    runs = jnp.all(pos_blocks == first[:, :, None] + jnp.arange(tb, dtype=positions.dtype), axis=-1)
    block_info = jnp.stack([first.astype(jnp.int32), runs.astype(jnp.int32)])
    tok_spec = pl.BlockSpec((1, tb, D), lambda b, i: (b, i, 0))
    return pl.pallas_call(
        _odd_kernel,
        out_shape=jax.ShapeDtypeStruct((B, S, D), F32),
        grid=(B, S // tb),
        in_specs=[pl.BlockSpec(memory_space=pltpu.SMEM), tok_spec, _hbm_spec()]
                 + [_hbm_spec() if any(p is h for h in in_hbm) else _full_spec(p) for p in params],
        out_specs=tok_spec,
        scratch_shapes=[
            pltpu.VMEM((QK, DV), F32),
            pltpu.VMEM((POOL_PAD + tb, BRANCH), F32),
            pltpu.VMEM((tb, LANES), F32),
            pltpu.VMEM((tb, LANES), F32),
            pltpu.VMEM((1, tb), jnp.int32),
            pltpu.VMEM((D, 2 * QK), BF16),
            pltpu.VMEM((len(POOL_WINDOWS) // 2, 2 * LANES, 2 * LANES), BF16),
            pltpu.VMEM(w_in.shape, BF16),
            pltpu.VMEM(w_out.shape, BF16),
            _stage(w_in),
            _stage(w_out),
            pltpu.SemaphoreType.DMA((STAGE_SLOTS,)),
        ],
        compiler_params=_compiler_params(),
        name="odd_layer_retention_pool",
    )(block_info, x, positions.reshape(B, 1, S), *params)


def kernel(x, positions, l0_w_in, l0_w_a2, l0_b_a, l0_gla_norm_g, l0_sgu_ln_g, l0_sgu_ln_b, l0_w_s, l0_b_s, l0_w_out, l0_ln_g, l0_ln_b, l1_w_in, l1_ret_norm_g, l1_w_pool, l1_pool_scale, l1_w_out, l1_ln_g, l1_ln_b):
    x = _even_layer(x, l0_w_in, l0_w_a2, l0_b_a, l0_gla_norm_g, l0_sgu_ln_g, l0_sgu_ln_b, l0_w_s, l0_b_s,
                    l0_w_out, l0_ln_g, l0_ln_b)
    return _odd_layer(x, positions, l1_w_in, l1_ret_norm_g, l1_w_pool, l1_pool_scale, l1_w_out, l1_ln_g, l1_ln_b)
```

```python
import jax
import jax.numpy as jnp
import numpy as np
from jax import lax
from jax.experimental import pallas as pl
from jax.experimental.pallas import tpu as pltpu

F32 = jnp.float32
BF16 = jnp.bfloat16

D_MODEL = 1024
BRANCH = 512
HEADS = 4
DK = 64
DV = 128
QK = HEADS * DK
GLA_RANK = 16
GLA_TAU = 16.0
GLA_CHUNK = 64
SGU_CHUNK = 128
RET_CHUNK = 128
ROPE_BASE = 10000.0
POOL_WINDOWS = (2, 4, 8, 16)
POOL_PAD = 16
DN_ALPHA = 4.0 ** 0.25
LN_EPS = 1e-5
LANES = 128
SUBLANES = 8
MXU_TILE = 256
VMEM_LIMIT_BYTES = 56 * 1024 * 1024
TOKEN_BLOCK = 1024
NORM_ROW_GROUPS = 4
STAGE_BYTES = 1024 * 1024
STAGE_SLOTS = 4
EVEN_Q, EVEN_K, EVEN_V, EVEN_GA, EVEN_LR = 0, QK, 2 * QK, 2 * QK + BRANCH, 2 * QK + 2 * BRANCH
EVEN_U, EVEN_SV, EVEN_GB = EVEN_LR + GLA_RANK, EVEN_LR + GLA_RANK + BRANCH, EVEN_LR + GLA_RANK + 2 * BRANCH


def _dot(a, b):
    return jnp.dot(a, b, preferred_element_type=F32)


def _dot_nt(a, b):
    return lax.dot_general(a, b, (((1,), (1,)), ((), ())), preferred_element_type=F32)


def _silu(x):
    hx = 0.5 * x
    return hx + hx * jnp.tanh(hx)


def _gelu_tanh(x):
    c = np.sqrt(2.0 / np.pi)
    hx = 0.5 * x
    return hx + hx * jnp.tanh(x * (np.float32(c) + np.float32(c * 0.044715) * (x * x)))


def _log_sigmoid(z):
    return jnp.minimum(z, 0.0) - jnp.log(1.0 + jnp.exp(-jnp.abs(z)))


def _layer_norm_rows(r, g, b, eps=LN_EPS):
    mu = jnp.mean(r, axis=-1, keepdims=True)
    c = r - mu
    var = jnp.mean(c * c, axis=-1, keepdims=True)
    return c * lax.rsqrt(var + eps) * g + b


def _deepnorm(x, y_scaled, g, b):
    return _layer_norm_rows(x + y_scaled, g, b, LN_EPS / DN_ALPHA ** 2)


def _project_and_norm(x_ref, o_first, o_second, wout_ref, lng_ref, lnb_ref):
    rows_per_group = x_ref.shape[1] // NORM_ROW_GROUPS
    outs = []
    for n in range(NORM_ROW_GROUPS):
        rows = slice(n * rows_per_group, (n + 1) * rows_per_group)
        y = (_dot(o_second[rows].astype(BF16), wout_ref[BRANCH:2 * BRANCH, :])
             + _dot(o_first[rows].astype(BF16), wout_ref[0:BRANCH, :]))
        outs.append(_deepnorm(x_ref[0, rows, :], y, lng_ref[...], lnb_ref[...]))
    return jnp.concatenate(outs, axis=0)


def _head_mask(h, natural):
    lane = lax.broadcasted_iota(jnp.int32, (1, QK), 1)
    head = lane // DK if natural else (lane % LANES) // (DK // 2)
    return (head == h).astype(F32)


def _load_as_bf16(src_hbm, dst_ref, stage_ref, sem, scale=None):
    rows = src_hbm.shape[0]
    slots, per_copy = stage_ref.shape[0], stage_ref.shape[1]
    starts = list(range(0, rows, per_copy))

    def copy(k):
        n_rows = min(per_copy, rows - starts[k])
        return pltpu.make_async_copy(src_hbm.at[pl.ds(starts[k], n_rows), :],
                                     stage_ref.at[k % slots, pl.ds(0, n_rows), :], sem.at[k % slots])

    for k in range(min(slots - 1, len(starts))):
        copy(k).start()
    for k, start in enumerate(starts):
        if k + slots - 1 < len(starts):
            copy(k + slots - 1).start()
        copy(k).wait()
        n_rows = min(per_copy, rows - start)
        block = stage_ref[k % slots, 0:n_rows, :]
        if scale is not None:
            block = block * scale
        dst_ref[start:start + n_rows, :] = block.astype(BF16)


def _even_kernel(x_ref, wt_hbm, wa2_ref, ba_ref, gng_ref, slg_ref, slb_ref, ws_ref, bsg_ref, wout_hbm,
                 lng_ref, lnb_ref, o_ref, st_s, wzt_s, bs_s, ws_s, wt_s, wout_s, wstage_s, ostage_s, sem):
    @pl.when((pl.program_id(0) == 0) & (pl.program_id(1) == 0))
    def _():
        _load_as_bf16(wt_hbm, wt_s, wstage_s, sem)
        _load_as_bf16(wout_hbm, wout_s, ostage_s, sem, 1.0 / DN_ALPHA)
        wa2t = jnp.concatenate([wa2_ref[...], jnp.zeros((LANES - GLA_RANK, QK), F32)], axis=0).T.astype(BF16)
        wzt_s[...] = _dot(wa2t, wt_s[EVEN_LR:EVEN_LR + LANES, :]).astype(BF16)
        trow = lax.broadcasted_iota(jnp.int32, (SGU_CHUNK, SGU_CHUNK), 0)
        tcol = lax.broadcasted_iota(jnp.int32, (SGU_CHUNK, SGU_CHUNK), 1)
        for g in range(BRANCH // LANES):
            ws_s[g] = jnp.where(trow >= tcol, ws_ref[g], 0.0).astype(BF16)
            bs_s[:, g * LANES:(g + 1) * LANES] = jnp.broadcast_to(bsg_ref[g:g + 1, :], (LANES, SGU_CHUNK)).T

    @pl.when(pl.program_id(1) == 0)
    def _():
        st_s[...] = jnp.zeros_like(st_s)

    refs = (wt_s, wzt_s, ba_ref, gng_ref, slg_ref, slb_ref, ws_s, bs_s, wout_s, lng_ref, lnb_ref)
    o_ref[0], st_s[...] = _even_block(x_ref, st_s[...], *refs)


def _even_block(x_ref, state, wt_ref, wzt_ref, ba_ref, gng_ref, slg_ref, slb_ref, ws_ref, bs_ref, wout_ref,
                lng_ref, lnb_ref):
    tb = x_ref.shape[1]
    xb = x_ref[0].astype(BF16)
    c_len = GLA_CHUNK
    t_len = SGU_CHUNK
    chunks = [slice(c * c_len, (c + 1) * c_len) for c in range(tb // c_len)]

    z = _dot_nt(xb, wzt_ref[...]) + ba_ref[...]
    q = _dot_nt(xb, wt_ref[EVEN_Q:EVEN_Q + QK, :])
    k = _dot_nt(xb, wt_ref[EVEN_K:EVEN_K + QK, :])
    la = _log_sigmoid(z) * (1.0 / GLA_TAU)
    v = _dot_nt(xb, wt_ref[EVEN_V:EVEN_V + BRANCH, :]).astype(BF16)

    row = lax.broadcasted_iota(jnp.int32, (LANES, 2 * LANES), 0)
    col = lax.broadcasted_iota(jnp.int32, (LANES, 2 * LANES), 1) % LANES
    tril2 = ((row >= col) & (row // c_len == col // c_len)).astype(BF16)
    la_hi = la.astype(BF16)
    la_lo = (la - la_hi.astype(F32)).astype(BF16)
    b = jnp.concatenate(
        [_dot(tril2, jnp.concatenate([la_hi[n * LANES:(n + 1) * LANES], la_lo[n * LANES:(n + 1) * LANES]],
                                     axis=0))
         for n in range(tb // LANES)], axis=0)

    u = _dot_nt(xb, wt_ref[EVEN_U:EVEN_U + BRANCH, :])
    q_dec = q * jnp.exp(b) * (DK ** -0.5)
    k_dec = k * jnp.exp(-b)
    masks = [_head_mask(h, True).astype(BF16) for h in range(HEADS)]
    q_dec_b = q_dec.astype(BF16)
    qms = [jnp.concatenate([q_dec_b[sl] * masks[h] for h in range(HEADS)], axis=0)
           for sl in chunks]
    sv = _dot_nt(xb, wt_ref[EVEN_SV:EVEN_SV + BRANCH, :])

    u = _gelu_tanh(u)
    kts, upds, dcols = [], [], []
    for c, sl in enumerate(chunks):
        kt = k_dec[sl].T.astype(BF16)
        kts.append(kt)
        upds.append(jnp.concatenate([_dot(kt[h * DK:(h + 1) * DK], v[sl, h * DV:(h + 1) * DV])
                                     for h in range(HEADS)], axis=0))
        b_last = b[(c + 1) * c_len - 1:(c + 1) * c_len, :]
        dcols.append(jnp.exp(jnp.broadcast_to(b_last, (LANES, QK)).T))
    g_a = _dot_nt(xb, wt_ref[EVEN_GA:EVEN_GA + BRANCH, :])
    sv = _gelu_tanh(sv)
    slg = slg_ref[...]
    slb = slb_ref[...]
    svn = jnp.concatenate(
        [_layer_norm_rows(sv[:, g * LANES:(g + 1) * LANES], slg[:, g * LANES:(g + 1) * LANES],
                          slb[:, g * LANES:(g + 1) * LANES]) for g in range(BRANCH // LANES)],
        axis=1).astype(BF16)
    g_b = _dot_nt(xb, wt_ref[EVEN_GB:EVEN_GB + BRANCH, :])

    states = []
    for upd, dcol in zip(upds, dcols):
        states.append(state.astype(BF16))
        state = dcol * (state + upd)

    s_groups = []
    for g in range(BRANCH // LANES):
        w = ws_ref[g]
        cols = slice(g * LANES, (g + 1) * LANES)
        mixed = []
        for n in range(0, tb // t_len, 2):
            pair = _dot(w, jnp.concatenate([svn[n * t_len:(n + 1) * t_len, cols],
                                            svn[(n + 1) * t_len:(n + 2) * t_len, cols]], axis=1))
            mixed += [pair[:, :LANES], pair[:, LANES:]]
        s_groups.append(jnp.concatenate(mixed, axis=0))
    s = jnp.concatenate(s_groups, axis=1) + jnp.concatenate([bs_ref[...]] * (tb // t_len), axis=0)

    gng = gng_ref[...]
    crow = lax.broadcasted_iota(jnp.int32, (HEADS * c_len, c_len), 0) % c_len
    ccol = lax.broadcasted_iota(jnp.int32, (HEADS * c_len, c_len), 1)
    causal = crow >= ccol
    fill = jnp.zeros((QK, MXU_TILE - DV - c_len), BF16)
    fused = [_dot(qm, jnp.concatenate([s_b, kt, fill], axis=1)) for qm, s_b, kt in zip(qms, states, kts)]
    o_b = u * s * _silu(g_b)
    o_chunks = []
    for sl, f in zip(chunks, fused):
        o_inter = f[:, 0:DV]
        sc = jnp.where(causal, f[:, DV:DV + c_len], 0.0).astype(BF16)
        o_heads = []
        for h in range(HEADS):
            rows = slice(h * c_len, (h + 1) * c_len)
            o = _dot(sc[rows], v[sl, h * DV:(h + 1) * DV]) + o_inter[rows]
            o = o * lax.rsqrt(jnp.mean(o * o, axis=-1, keepdims=True) + LN_EPS)
            o_heads.append(o * gng[:, h * DV:(h + 1) * DV])
        o_chunks.append(jnp.concatenate(o_heads, axis=1))
    o_a = jnp.concatenate(o_chunks, axis=0) * _silu(g_a)
    return _project_and_norm(x_ref, o_a, o_b, wout_ref, lng_ref, lnb_ref), state


def _odd_kernel(run_ref, x_ref, pos_hbm, w_hbm, perm_ref, inv_ref, ctab_ref, stab_ref, dmat_ref, wint_ref,
                wst_ref, dec_ref, rng_ref, wpool_ref, psc_ref, wout_hbm, lng_ref, lnb_ref, o_ref, st_s, pad_s,
                cos_s, sin_s, pos_s, wqk_s, wp_ref, w_ref, wout_ref, wstage_s, ostage_s, sem):
    tb = x_ref.shape[1]
    seq = pl.program_id(0)
    i = pl.program_id(1)

    @pl.when((seq == 0) & (i == 0))
    def _():
        _load_as_bf16(w_hbm, w_ref, wstage_s, sem)
        _load_as_bf16(wout_hbm, wout_ref, ostage_s, sem, 1.0 / DN_ALPHA)
        wqk_s[...] = _dot(w_ref[:, 0:2 * QK], perm_ref[...]).astype(BF16)
        zero = jnp.zeros((LANES, LANES), F32)
        for n in range(len(POOL_WINDOWS) // 2):
            wp_ref[n] = jnp.concatenate(
                [jnp.concatenate([wpool_ref[2 * n], zero], axis=1),
                 jnp.concatenate([zero, wpool_ref[2 * n + 1]], axis=1)], axis=0).astype(BF16)

    @pl.when(i == 0)
    def _():
        st_s[...] = jnp.zeros_like(st_s)
        pad_s[0:POOL_PAD, :] = jnp.zeros((POOL_PAD, BRANCH), F32)

    c_len = RET_CHUNK
    half = DK // 2
    chunks = [slice(c * c_len, (c + 1) * c_len) for c in range(tb // c_len)]
    c_q, c_k, c_v, c_gc, c_p, c_gd = [(o, o + n) for o, n in zip(
        (0, QK, 2 * QK, 2 * QK + BRANCH, 2 * QK + 2 * BRANCH, 2 * QK + 3 * BRANCH),
        (QK, QK, BRANCH, BRANCH, BRANCH, BRANCH))]

    consecutive = run_ref[1, seq, i] == 1

    @pl.when(consecutive)
    def _():
        a0 = run_ref[0, seq, i].astype(F32) * inv_ref[...]
        c0 = jnp.cos(a0)
        s0 = jnp.sin(a0)
        cos_s[...] = c0 * ctab_ref[...] - s0 * stab_ref[...]
        sin_s[...] = s0 * ctab_ref[...] + c0 * stab_ref[...]

    @pl.when(jnp.logical_not(consecutive))
    def _():
        copy = pltpu.make_async_copy(pos_hbm.at[seq, :, pl.ds(pl.multiple_of(i * tb, tb), tb)], pos_s, sem.at[0])
        copy.start()
        copy.wait()
        pos_rows = jnp.broadcast_to(pos_s[...].astype(F32), (LANES, tb)).T
        ang = pos_rows * inv_ref[...]
        cos_s[...] = jnp.cos(ang)
        sin_s[...] = jnp.sin(ang)

    xb = x_ref[0].astype(BF16)

    q = _dot(xb, wqk_s[:, c_q[0]:c_q[1]])
    k = _dot(xb, wqk_s[:, c_k[0]:c_k[1]])
    cos = cos_s[...]
    sin = sin_s[...]
    v = _dot(xb, w_ref[:, c_v[0]:c_v[1]]).astype(BF16)
    p = _dot(xb, w_ref[:, c_p[0]:c_p[1]])
    q1, q2 = q[:, :LANES], q[:, LANES:]
    k1, k2 = k[:, :LANES], k[:, LANES:]
    qr = jnp.concatenate([q1 * cos - q2 * sin, q1 * sin + q2 * cos], axis=1)
    kr = jnp.concatenate([k1 * cos - k2 * sin, k1 * sin + k2 * cos], axis=1)
    masks = [_head_mask(h, False).astype(BF16) for h in range(HEADS)]
    qr_b = qr.astype(BF16)
    qms = [jnp.concatenate([qr_b[sl] * masks[h] for h in range(HEADS)], axis=0)
           for sl in chunks]
    g_c = _dot(xb, w_ref[:, c_gc[0]:c_gc[1]])

    pad_s[POOL_PAD:POOL_PAD + tb, :] = p
    t_top = i * tb + lax.broadcasted_iota(jnp.int32, (POOL_PAD, LANES), 0)
    pooled = []
    for g, win in enumerate(POOL_WINDOWS):
        cols = slice(g * LANES, (g + 1) * LANES)
        acc = pad_s[:, cols]
        shift = 1
        while shift < win:
            acc = acc + pltpu.roll(acc, shift, axis=0)
            shift *= 2
        cnt_top = jnp.minimum(t_top + 1, win).astype(F32)
        mean = jnp.concatenate([acc[POOL_PAD:2 * POOL_PAD] / cnt_top, acc[2 * POOL_PAD:] * (1.0 / win)],
                               axis=0)
        pooled.append((mean - p[:, cols]).astype(BF16))
    tail = pad_s[tb:tb + POOL_PAD, :]
    pad_s[0:POOL_PAD, :] = tail
    kts, upds = [], []
    for sl in chunks:
        kt = (kr[sl] * wst_ref[...]).T.astype(BF16)
        kts.append(kt)
        upd = [_dot(jnp.concatenate([kt[h * half:(h + 1) * half],
                                     kt[LANES + h * half:LANES + (h + 1) * half]], axis=0),
                    v[sl, h * DV:(h + 1) * DV]) for h in range(HEADS)]
        upds.append(jnp.concatenate([u[:half] for u in upd] + [u[half:] for u in upd], axis=0))
    g_d = _dot(xb, w_ref[:, c_gd[0]:c_gd[1]])

    state = st_s[...]
    states = []
    for upd in upds:
        states.append(state.astype(BF16))
        state = dec_ref[...] * state + upd
    st_s[...] = state
    yd = jnp.concatenate([_dot(jnp.concatenate([pooled[2 * n], pooled[2 * n + 1]], axis=1), wp_ref[n])
                          for n in range(len(POOL_WINDOWS) // 2)], axis=1)
    fused = [_dot(qm, jnp.concatenate([s_b, kt], axis=1)) for qm, s_b, kt in zip(qms, states, kts)]
    o_d = yd * psc_ref[...] * _silu(g_d)
    rng = rng_ref[...]
    o_chunks = []
    for sl, f in zip(chunks, fused):
        o_inter = f[:, 0:DV] * wint_ref[...]
        sc = (f[:, DV:DV + c_len] * dmat_ref[...]).astype(BF16)
        o_heads = []
        for h in range(HEADS):
            rows = slice(h * c_len, (h + 1) * c_len)
            o = _dot(sc[rows], v[sl, h * DV:(h + 1) * DV]) + o_inter[rows]
            mu = jnp.mean(o, axis=-1, keepdims=True)
            oc = o - mu
            var = jnp.mean(oc * oc, axis=-1, keepdims=True)
            o_heads.append(oc * lax.rsqrt(var + LN_EPS) * rng[:, h * DV:(h + 1) * DV])
        o_chunks.append(jnp.concatenate(o_heads, axis=1))
    o_c = jnp.concatenate(o_chunks, axis=0) * _silu(g_c)
    o_ref[0] = _project_and_norm(x_ref, o_c, o_d, wout_ref, lng_ref, lnb_ref)


def _full_spec(a):
    nd = a.ndim
    return pl.BlockSpec(a.shape, lambda *_, _nd=nd: (0,) * _nd)


def _compiler_params():
    return pltpu.CompilerParams(dimension_semantics=("arbitrary", "arbitrary"),
                                vmem_limit_bytes=VMEM_LIMIT_BYTES)


def _row(a):
    return a.reshape(1, -1).astype(F32)


def _hbm_spec():
    return pl.BlockSpec(memory_space=pl.ANY)


def _stage(w):
    rows = min(w.shape[0], STAGE_BYTES // (4 * w.shape[1]) // SUBLANES * SUBLANES)
    return pltpu.VMEM((STAGE_SLOTS, rows, w.shape[1]), F32)


def _even_layer(x, w_in, w_a2, b_a, gla_norm_g, sgu_ln_g, sgu_ln_b, w_s, b_s, w_out, ln_g, ln_b):
    B, S, D = x.shape
    tb = TOKEN_BLOCK
    w_in_t = w_in.T
    params = (w_in_t, w_a2.astype(F32), _row(b_a), _row(gla_norm_g), _row(sgu_ln_g), _row(sgu_ln_b),
              w_s.astype(F32), b_s.astype(F32), w_out, _row(ln_g), _row(ln_b))
    in_hbm = (w_in_t, w_out)
    tok_spec = pl.BlockSpec((1, tb, D), lambda b, i: (b, i, 0))
    return pl.pallas_call(
        _even_kernel,
        out_shape=jax.ShapeDtypeStruct((B, S, D), F32),
        grid=(B, S // tb),
        in_specs=[tok_spec] + [_hbm_spec() if any(p is h for h in in_hbm) else _full_spec(p) for p in params],
        out_specs=tok_spec,
        scratch_shapes=[
            pltpu.VMEM((QK, DV), F32),
            pltpu.VMEM((QK, D), BF16),
            pltpu.VMEM((SGU_CHUNK, BRANCH), F32),
            pltpu.VMEM((BRANCH // LANES, SGU_CHUNK, SGU_CHUNK), BF16),
            pltpu.VMEM(w_in_t.shape, BF16),
            pltpu.VMEM(w_out.shape, BF16),
            _stage(w_in_t),
            _stage(w_out),
            pltpu.SemaphoreType.DMA((STAGE_SLOTS,)),
        ],
        compiler_params=_compiler_params(),
        name="even_layer_gla_sgu",
    )(x, *params)


def _retention_tables():
    c = RET_CHUNK
    log_gamma = np.log(1.0 - 2.0 ** (-5.0 - np.arange(HEADS, dtype=np.float64)))
    idx = np.arange(c, dtype=np.float64)
    rel = idx[:, None] - idx[None, :]
    dmat = np.where(rel >= 0, np.exp((idx[:, None] - (c - 1.0))[None] * log_gamma[:, None, None]), 0.0)
    w_inter = np.exp((idx + 1.0)[None] * log_gamma[:, None])
    w_state = np.exp((c - 1.0 - idx)[None] * log_gamma[:, None])
    lane_head = (np.arange(QK) % LANES) // (DK // 2)
    dmat = dmat.reshape(HEADS * c, c)
    wint = np.broadcast_to(w_inter[:, :, None], (HEADS, c, DV)).reshape(HEADS * c, DV)
    wst = w_state[lane_head, :].T
    dec = np.broadcast_to(np.exp(c * log_gamma)[lane_head][:, None], (QK, DV))
    inv = ROPE_BASE ** (-np.arange(DK // 2, dtype=np.float64) / (DK // 2))
    inv = np.tile(inv, HEADS)[None, :].astype(np.float32)
    off = np.arange(TOKEN_BLOCK, dtype=np.float64)[:, None] * inv.astype(np.float64)
    f = lambda a: jnp.asarray(np.ascontiguousarray(a), F32)
    return f(inv), f(np.cos(off)), f(np.sin(off)), f(dmat), f(wint), f(wst), f(dec)


def _odd_layer(x, positions, w_in, ret_norm_g, w_pool, pool_scale, w_out, ln_g, ln_b):
    B, S, D = x.shape
    tb = TOKEN_BLOCK
    j = np.arange(2 * QK)
    src = (j // QK) * QK + ((j % LANES) // (DK // 2)) * DK + ((j % QK) // LANES) * (DK // 2) + j % (DK // 2)
    perm = jnp.asarray((src[None, :] == np.arange(2 * QK)[:, None]) * np.where(j < QK, DK ** -0.5, 1.0)[None, :],
                       BF16)
    params = (w_in, perm, *_retention_tables(), _row(ret_norm_g), w_pool.astype(F32), _row(pool_scale), w_out,
              _row(ln_g), _row(ln_b))
    in_hbm = (w_in, w_out)
    pos_blocks = positions.reshape(B, S // tb, tb)
    first = pos_blocks[:, :, 0]
    runs = jnp.all(pos_blocks == first[:, :, None] + jnp.arange(tb, dtype=positions.dtype), axis=-1)
    block_info = jnp.stack([first.astype(jnp.int32), runs.astype(jnp.int32)])
    tok_spec = pl.BlockSpec((1, tb, D), lambda b, i: (b, i, 0))
    return pl.pallas_call(
        _odd_kernel,
        out_shape=jax.ShapeDtypeStruct((B, S, D), F32),
        grid=(B, S // tb),
        in_specs=[pl.BlockSpec(memory_space=pltpu.SMEM), tok_spec, _hbm_spec()]
                 + [_hbm_spec() if any(p is h for h in in_hbm) else _full_spec(p) for p in params],
        out_specs=tok_spec,
        scratch_shapes=[
            pltpu.VMEM((QK, DV), F32),
            pltpu.VMEM((POOL_PAD + tb, BRANCH), F32),
            pltpu.VMEM((tb, LANES), F32),
            pltpu.VMEM((tb, LANES), F32),
            pltpu.VMEM((1, tb), jnp.int32),
            pltpu.VMEM((D, 2 * QK), BF16),
            pltpu.VMEM((len(POOL_WINDOWS) // 2, 2 * LANES, 2 * LANES), BF16),
            pltpu.VMEM(w_in.shape, BF16),
            pltpu.VMEM(w_out.shape, BF16),
            _stage(w_in),
            _stage(w_out),
            pltpu.SemaphoreType.DMA((STAGE_SLOTS,)),
        ],
        compiler_params=_compiler_params(),
        name="odd_layer_retention_pool",
    )(block_info, x, positions.reshape(B, 1, S), *params)


def kernel(x, positions, l0_w_in, l0_w_a2, l0_b_a, l0_gla_norm_g, l0_sgu_ln_g, l0_sgu_ln_b, l0_w_s, l0_b_s, l0_w_out, l0_ln_g, l0_ln_b, l1_w_in, l1_ret_norm_g, l1_w_pool, l1_pool_scale, l1_w_out, l1_ln_g, l1_ln_b):
    x = _even_layer(x, l0_w_in, l0_w_a2, l0_b_a, l0_gla_norm_g, l0_sgu_ln_g, l0_sgu_ln_b, l0_w_s, l0_b_s,
                    l0_w_out, l0_ln_g, l0_ln_b)
    return _odd_layer(x, positions, l1_w_in, l1_ret_norm_g, l1_w_pool, l1_pool_scale, l1_w_out, l1_ln_g, l1_ln_b)
```

```python
import jax
import jax.numpy as jnp
import numpy as np
from jax import lax
from jax.experimental import pallas as pl
from jax.experimental.pallas import tpu as pltpu

F32 = jnp.float32
BF16 = jnp.bfloat16

D_MODEL = 1024
BRANCH = 512
HEADS = 4
DK = 64
DV = 128
QK = HEADS * DK
GLA_RANK = 16
GLA_TAU = 16.0
GLA_CHUNK = 64
SGU_CHUNK = 128
RET_CHUNK = 128
ROPE_BASE = 10000.0
POOL_WINDOWS = (2, 4, 8, 16)
POOL_PAD = 16
DN_ALPHA = 4.0 ** 0.25
LN_EPS = 1e-5
LANES = 128
SUBLANES = 8
MXU_TILE = 256
VMEM_LIMIT_BYTES = 56 * 1024 * 1024
TOKEN_BLOCK = 1024
NORM_ROW_GROUPS = 4
STAGE_BYTES = 1024 * 1024
STAGE_SLOTS = 4
EVEN_Q, EVEN_K, EVEN_V, EVEN_GA, EVEN_LR = 0, QK, 2 * QK, 2 * QK + BRANCH, 2 * QK + 2 * BRANCH
EVEN_U, EVEN_SV, EVEN_GB = EVEN_LR + GLA_RANK, EVEN_LR + GLA_RANK + BRANCH, EVEN_LR + GLA_RANK + 2 * BRANCH


def _dot(a, b):
    return jnp.dot(a, b, preferred_element_type=F32)


def _dot_nt(a, b):
    return lax.dot_general(a, b, (((1,), (1,)), ((), ())), preferred_element_type=F32)


def _silu(x):
    hx = 0.5 * x
    return hx + hx * jnp.tanh(hx)


def _gelu_tanh(x):
    c = np.sqrt(2.0 / np.pi)
    hx = 0.5 * x
    return hx + hx * jnp.tanh(x * (np.float32(c) + np.float32(c * 0.044715) * (x * x)))


def _log_sigmoid(z):
    return jnp.minimum(z, 0.0) - jnp.log(1.0 + jnp.exp(-jnp.abs(z)))


def _layer_norm_rows(r, g, b, eps=LN_EPS):
    mu = jnp.mean(r, axis=-1, keepdims=True)
    c = r - mu
    var = jnp.mean(c * c, axis=-1, keepdims=True)
    return c * lax.rsqrt(var + eps) * g + b


def _deepnorm(x, y_scaled, g, b):
    return _layer_norm_rows(x + y_scaled, g, b, LN_EPS / DN_ALPHA ** 2)


def _project_and_norm(x_ref, o_first, o_second, wout_ref, lng_ref, lnb_ref):
    rows_per_group = x_ref.shape[1] // NORM_ROW_GROUPS
    outs = []
    for n in range(NORM_ROW_GROUPS):
        rows = slice(n * rows_per_group, (n + 1) * rows_per_group)
        y = (_dot(o_second[rows].astype(BF16), wout_ref[BRANCH:2 * BRANCH, :])
             + _dot(o_first[rows].astype(BF16), wout_ref[0:BRANCH, :]))
        outs.append(_deepnorm(x_ref[0, rows, :], y, lng_ref[...], lnb_ref[...]))
    return jnp.concatenate(outs, axis=0)


def _head_mask(h, natural):
    lane = lax.broadcasted_iota(jnp.int32, (1, QK), 1)
    head = lane // DK if natural else (lane % LANES) // (DK // 2)
    return (head == h).astype(F32)


def _load_as_bf16(src_hbm, dst_ref, stage_ref, sem, scale=None):
    rows = src_hbm.shape[0]
    slots, per_copy = stage_ref.shape[0], stage_ref.shape[1]
    starts = list(range(0, rows, per_copy))

    def copy(k):
        n_rows = min(per_copy, rows - starts[k])
        return pltpu.make_async_copy(src_hbm.at[pl.ds(starts[k], n_rows), :],
                                     stage_ref.at[k % slots, pl.ds(0, n_rows), :], sem.at[k % slots])

    for k in range(min(slots - 1, len(starts))):
        copy(k).start()
    for k, start in enumerate(starts):
        if k + slots - 1 < len(starts):
            copy(k + slots - 1).start()
        copy(k).wait()
        n_rows = min(per_copy, rows - start)
        block = stage_ref[k % slots, 0:n_rows, :]
        if scale is not None:
            block = block * scale
        dst_ref[start:start + n_rows, :] = block.astype(BF16)


def _even_kernel(x_ref, wt_hbm, wa2_ref, ba_ref, gng_ref, slg_ref, slb_ref, ws_ref, bsg_ref, wout_hbm,
                 lng_ref, lnb_ref, o_ref, st_s, wzt_s, bs_s, wt_s, wout_s, wstage_s, ostage_s, sem):
    @pl.when((pl.program_id(0) == 0) & (pl.program_id(1) == 0))
    def _():
        _load_as_bf16(wt_hbm, wt_s, wstage_s, sem)
        _load_as_bf16(wout_hbm, wout_s, ostage_s, sem, 1.0 / DN_ALPHA)
        wa2t = jnp.concatenate([wa2_ref[...], jnp.zeros((LANES - GLA_RANK, QK), F32)], axis=0).T.astype(BF16)
        wzt_s[...] = _dot(wa2t, wt_s[EVEN_LR:EVEN_LR + LANES, :]).astype(BF16)
        for g in range(BRANCH // LANES):
            bs_s[:, g * LANES:(g + 1) * LANES] = jnp.broadcast_to(bsg_ref[g:g + 1, :], (LANES, SGU_CHUNK)).T

    @pl.when(pl.program_id(1) == 0)
    def _():
        st_s[...] = jnp.zeros_like(st_s)

    refs = (wt_s, wzt_s, ba_ref, gng_ref, slg_ref, slb_ref, ws_ref, bs_s, wout_s, lng_ref, lnb_ref)
    o_ref[0], st_s[...] = _even_block(x_ref, st_s[...], *refs)


def _even_block(x_ref, state, wt_ref, wzt_ref, ba_ref, gng_ref, slg_ref, slb_ref, ws_ref, bs_ref, wout_ref,
                lng_ref, lnb_ref):
    tb = x_ref.shape[1]
    xb = x_ref[0].astype(BF16)
    c_len = GLA_CHUNK
    t_len = SGU_CHUNK
    chunks = [slice(c * c_len, (c + 1) * c_len) for c in range(tb // c_len)]

    z = _dot_nt(xb, wzt_ref[...]) + ba_ref[...]
    q = _dot_nt(xb, wt_ref[EVEN_Q:EVEN_Q + QK, :])
    k = _dot_nt(xb, wt_ref[EVEN_K:EVEN_K + QK, :])
    la = _log_sigmoid(z) * (1.0 / GLA_TAU)
    v = _dot_nt(xb, wt_ref[EVEN_V:EVEN_V + BRANCH, :]).astype(BF16)

    row = lax.broadcasted_iota(jnp.int32, (LANES, 2 * LANES), 0)
    col = lax.broadcasted_iota(jnp.int32, (LANES, 2 * LANES), 1) % LANES
    tril2 = ((row >= col) & (row // c_len == col // c_len)).astype(BF16)
    la_hi = la.astype(BF16)
    la_lo = (la - la_hi.astype(F32)).astype(BF16)
    b = jnp.concatenate(
        [_dot(tril2, jnp.concatenate([la_hi[n * LANES:(n + 1) * LANES], la_lo[n * LANES:(n + 1) * LANES]],
                                     axis=0))
         for n in range(tb // LANES)], axis=0)

    u = _dot_nt(xb, wt_ref[EVEN_U:EVEN_U + BRANCH, :])
    q_dec = q * jnp.exp(b) * (DK ** -0.5)
    k_dec = k * jnp.exp(-b)
    masks = [_head_mask(h, True).astype(BF16) for h in range(HEADS)]
    q_dec_b = q_dec.astype(BF16)
    qms = [jnp.concatenate([q_dec_b[sl] * masks[h] for h in range(HEADS)], axis=0)
           for sl in chunks]
    sv = _dot_nt(xb, wt_ref[EVEN_SV:EVEN_SV + BRANCH, :])

    u = _gelu_tanh(u)
    kts, upds, dcols = [], [], []
    for c, sl in enumerate(chunks):
        kt = k_dec[sl].T.astype(BF16)
        kts.append(kt)
        upds.append(jnp.concatenate([_dot(kt[h * DK:(h + 1) * DK], v[sl, h * DV:(h + 1) * DV])
                                     for h in range(HEADS)], axis=0))
        b_last = b[(c + 1) * c_len - 1:(c + 1) * c_len, :]
        dcols.append(jnp.exp(jnp.broadcast_to(b_last, (LANES, QK)).T))
    gate_a = _silu(_dot_nt(xb, wt_ref[EVEN_GA:EVEN_GA + BRANCH, :])).astype(BF16)
    sv = _gelu_tanh(sv)
    slg = slg_ref[...]
    slb = slb_ref[...]
    svn = jnp.concatenate(
        [_layer_norm_rows(sv[:, g * LANES:(g + 1) * LANES], slg[:, g * LANES:(g + 1) * LANES],
                          slb[:, g * LANES:(g + 1) * LANES]) for g in range(BRANCH // LANES)],
        axis=1).astype(BF16)
    gate_b = _silu(_dot_nt(xb, wt_ref[EVEN_GB:EVEN_GB + BRANCH, :])).astype(BF16)

    states = []
    for upd, dcol in zip(upds, dcols):
        states.append(state.astype(BF16))
        state = dcol * (state + upd)

    trow = lax.broadcasted_iota(jnp.int32, (t_len, t_len), 0)
    tcol = lax.broadcasted_iota(jnp.int32, (t_len, t_len), 1)
    s_groups = []
    for g in range(BRANCH // LANES):
        w = jnp.where(trow >= tcol, ws_ref[g], 0.0).astype(BF16)
        cols = slice(g * LANES, (g + 1) * LANES)
        mixed = []
        for n in range(0, tb // t_len, 2):
            pair = _dot(w, jnp.concatenate([svn[n * t_len:(n + 1) * t_len, cols],
                                            svn[(n + 1) * t_len:(n + 2) * t_len, cols]], axis=1))
            mixed += [pair[:, :LANES], pair[:, LANES:]]
        s_groups.append(jnp.concatenate(mixed, axis=0))
    s = jnp.concatenate(s_groups, axis=1) + jnp.concatenate([bs_ref[...]] * (tb // t_len), axis=0)

    gng = gng_ref[...]
    crow = lax.broadcasted_iota(jnp.int32, (HEADS * c_len, c_len), 0) % c_len
    ccol = lax.broadcasted_iota(jnp.int32, (HEADS * c_len, c_len), 1)
    causal = crow >= ccol
    fill = jnp.zeros((QK, MXU_TILE - DV - c_len), BF16)
    fused = [_dot(qm, jnp.concatenate([s_b, kt, fill], axis=1)) for qm, s_b, kt in zip(qms, states, kts)]
    o_b = u * s * gate_b
    o_chunks = []
    for sl, f in zip(chunks, fused):
        o_inter = f[:, 0:DV]
        sc = jnp.where(causal, f[:, DV:DV + c_len], 0.0).astype(BF16)
        o_heads = []
        for h in range(HEADS):
            rows = slice(h * c_len, (h + 1) * c_len)
            o = _dot(sc[rows], v[sl, h * DV:(h + 1) * DV]) + o_inter[rows]
            o = o * lax.rsqrt(jnp.mean(o * o, axis=-1, keepdims=True) + LN_EPS)
            o_heads.append(o * gng[:, h * DV:(h + 1) * DV])
        o_chunks.append(jnp.concatenate(o_heads, axis=1))
    o_a = jnp.concatenate(o_chunks, axis=0) * gate_a
    return _project_and_norm(x_ref, o_a, o_b, wout_ref, lng_ref, lnb_ref), state


def _odd_kernel(run_ref, x_ref, pos_hbm, w_hbm, perm_ref, inv_ref, ctab_ref, stab_ref, dmat_ref, wint_ref,
                wst_ref, dec_ref, rng_ref, wpool_ref, psc_ref, wout_hbm, lng_ref, lnb_ref, o_ref, st_s, pad_s,
                cos_s, sin_s, pos_s, wqk_s, wp_ref, w_ref, wout_ref, wstage_s, ostage_s, sem):
    tb = x_ref.shape[1]
    seq = pl.program_id(0)
    i = pl.program_id(1)

    @pl.when((seq == 0) & (i == 0))
    def _():
        _load_as_bf16(w_hbm, w_ref, wstage_s, sem)
        _load_as_bf16(wout_hbm, wout_ref, ostage_s, sem, 1.0 / DN_ALPHA)
        wqk_s[...] = _dot(w_ref[:, 0:2 * QK], perm_ref[...]).astype(BF16)
        zero = jnp.zeros((LANES, LANES), F32)
        for n in range(len(POOL_WINDOWS) // 2):
            wp_ref[n] = jnp.concatenate(
                [jnp.concatenate([wpool_ref[2 * n], zero], axis=1),
                 jnp.concatenate([zero, wpool_ref[2 * n + 1]], axis=1)], axis=0).astype(BF16)

    @pl.when(i == 0)
    def _():
        st_s[...] = jnp.zeros_like(st_s)
        pad_s[0:POOL_PAD, :] = jnp.zeros((POOL_PAD, BRANCH), F32)

    c_len = RET_CHUNK
    half = DK // 2
    chunks = [slice(c * c_len, (c + 1) * c_len) for c in range(tb // c_len)]
    c_q, c_k, c_v, c_gc, c_p, c_gd = [(o, o + n) for o, n in zip(
        (0, QK, 2 * QK, 2 * QK + BRANCH, 2 * QK + 2 * BRANCH, 2 * QK + 3 * BRANCH),
        (QK, QK, BRANCH, BRANCH, BRANCH, BRANCH))]

    consecutive = run_ref[1, seq, i] == 1

    @pl.when(consecutive)
    def _():
        a0 = run_ref[0, seq, i].astype(F32) * inv_ref[...]
        c0 = jnp.cos(a0)
        s0 = jnp.sin(a0)
        cos_s[...] = c0 * ctab_ref[...] - s0 * stab_ref[...]
        sin_s[...] = s0 * ctab_ref[...] + c0 * stab_ref[...]

    @pl.when(jnp.logical_not(consecutive))
    def _():
        copy = pltpu.make_async_copy(pos_hbm.at[seq, :, pl.ds(pl.multiple_of(i * tb, tb), tb)], pos_s, sem.at[0])
        copy.start()
        copy.wait()
        pos_rows = jnp.broadcast_to(pos_s[...].astype(F32), (LANES, tb)).T
        ang = pos_rows * inv_ref[...]
        cos_s[...] = jnp.cos(ang)
        sin_s[...] = jnp.sin(ang)

    xb = x_ref[0].astype(BF16)

    q = _dot(xb, wqk_s[:, c_q[0]:c_q[1]]) * (DK ** -0.5)
    k = _dot(xb, wqk_s[:, c_k[0]:c_k[1]])
    cos = cos_s[...]
    sin = sin_s[...]
    v = _dot(xb, w_ref[:, c_v[0]:c_v[1]]).astype(BF16)
    p = _dot(xb, w_ref[:, c_p[0]:c_p[1]])
    q1, q2 = q[:, :LANES], q[:, LANES:]
    k1, k2 = k[:, :LANES], k[:, LANES:]
    qr = jnp.concatenate([q1 * cos - q2 * sin, q1 * sin + q2 * cos], axis=1)
    kr = jnp.concatenate([k1 * cos - k2 * sin, k1 * sin + k2 * cos], axis=1)
    masks = [_head_mask(h, False).astype(BF16) for h in range(HEADS)]
    qr_b = qr.astype(BF16)
    qms = [jnp.concatenate([qr_b[sl] * masks[h] for h in range(HEADS)], axis=0)
           for sl in chunks]
    gate_c = _silu(_dot(xb, w_ref[:, c_gc[0]:c_gc[1]])).astype(BF16)

    pad_s[POOL_PAD:POOL_PAD + tb, :] = p
    t_top = i * tb + lax.broadcasted_iota(jnp.int32, (POOL_PAD, LANES), 0)
    pooled = []
    for g, win in enumerate(POOL_WINDOWS):
        cols = slice(g * LANES, (g + 1) * LANES)
        acc = pad_s[:, cols]
        shift = 1
        while shift < win:
            acc = acc + pltpu.roll(acc, shift, axis=0)
            shift *= 2
        cnt_top = jnp.minimum(t_top + 1, win).astype(F32)
        mean = jnp.concatenate([acc[POOL_PAD:2 * POOL_PAD] / cnt_top, acc[2 * POOL_PAD:] * (1.0 / win)],
                               axis=0)
        pooled.append((mean - p[:, cols]).astype(BF16))
    tail = pad_s[tb:tb + POOL_PAD, :]
    pad_s[0:POOL_PAD, :] = tail
    kts, upds = [], []
    for sl in chunks:
        kt = (kr[sl] * wst_ref[...]).T.astype(BF16)
        kts.append(kt)
        upd = [_dot(jnp.concatenate([kt[h * half:(h + 1) * half],
                                     kt[LANES + h * half:LANES + (h + 1) * half]], axis=0),
                    v[sl, h * DV:(h + 1) * DV]) for h in range(HEADS)]
        upds.append(jnp.concatenate([u[:half] for u in upd] + [u[half:] for u in upd], axis=0))
    gate_d = _silu(_dot(xb, w_ref[:, c_gd[0]:c_gd[1]])).astype(BF16)

    state = st_s[...]
    states = []
    for upd in upds:
        states.append(state.astype(BF16))
        state = dec_ref[...] * state + upd
    st_s[...] = state
    yd = jnp.concatenate([_dot(jnp.concatenate([pooled[2 * n], pooled[2 * n + 1]], axis=1), wp_ref[n])
                          for n in range(len(POOL_WINDOWS) // 2)], axis=1)
    fused = [_dot(qm, jnp.concatenate([s_b, kt], axis=1)) for qm, s_b, kt in zip(qms, states, kts)]
    o_d = yd * psc_ref[...] * gate_d
    rng = rng_ref[...]
    o_chunks = []
    for sl, f in zip(chunks, fused):
        o_inter = f[:, 0:DV] * wint_ref[...]
        sc = (f[:, DV:DV + c_len] * dmat_ref[...]).astype(BF16)
        o_heads = []
        for h in range(HEADS):
            rows = slice(h * c_len, (h + 1) * c_len)
            o = _dot(sc[rows], v[sl, h * DV:(h + 1) * DV]) + o_inter[rows]
            mu = jnp.mean(o, axis=-1, keepdims=True)
            oc = o - mu
            var = jnp.mean(oc * oc, axis=-1, keepdims=True)
            o_heads.append(oc * lax.rsqrt(var + LN_EPS) * rng[:, h * DV:(h + 1) * DV])
        o_chunks.append(jnp.concatenate(o_heads, axis=1))
    o_c = jnp.concatenate(o_chunks, axis=0) * gate_c
    o_ref[0] = _project_and_norm(x_ref, o_c, o_d, wout_ref, lng_ref, lnb_ref)


def _full_spec(a):
    nd = a.ndim
    return pl.BlockSpec(a.shape, lambda *_, _nd=nd: (0,) * _nd)


def _compiler_params():
    return pltpu.CompilerParams(dimension_semantics=("arbitrary", "arbitrary"),
                                vmem_limit_bytes=VMEM_LIMIT_BYTES)


def _row(a):
    return a.reshape(1, -1).astype(F32)


def _hbm_spec():
    return pl.BlockSpec(memory_space=pl.ANY)


def _stage(w):
    rows = min(w.shape[0], STAGE_BYTES // (4 * w.shape[1]) // SUBLANES * SUBLANES)
    return pltpu.VMEM((STAGE_SLOTS, rows, w.shape[1]), F32)


def _even_layer(x, w_in, w_a2, b_a, gla_norm_g, sgu_ln_g, sgu_ln_b, w_s, b_s, w_out, ln_g, ln_b):
    B, S, D = x.shape
    tb = TOKEN_BLOCK
    w_in_t = w_in.T
    params = (w_in_t, w_a2.astype(F32), _row(b_a), _row(gla_norm_g), _row(sgu_ln_g), _row(sgu_ln_b),
              w_s.astype(F32), b_s.astype(F32), w_out, _row(ln_g), _row(ln_b))
    in_hbm = (w_in_t, w_out)
    tok_spec = pl.BlockSpec((1, tb, D), lambda b, i: (b, i, 0))
    return pl.pallas_call(
        _even_kernel,
        out_shape=jax.ShapeDtypeStruct((B, S, D), F32),
        grid=(B, S // tb),
        in_specs=[tok_spec] + [_hbm_spec() if any(p is h for h in in_hbm) else _full_spec(p) for p in params],
        out_specs=tok_spec,
        scratch_shapes=[
            pltpu.VMEM((QK, DV), F32),
            pltpu.VMEM((QK, D), BF16),
            pltpu.VMEM((SGU_CHUNK, BRANCH), F32),
            pltpu.VMEM(w_in_t.shape, BF16),
            pltpu.VMEM(w_out.shape, BF16),
            _stage(w_in_t),
            _stage(w_out),
            pltpu.SemaphoreType.DMA((STAGE_SLOTS,)),
        ],
        compiler_params=_compiler_params(),
        name="even_layer_gla_sgu",
    )(x, *params)


def _retention_tables():
    c = RET_CHUNK
    log_gamma = np.log(1.0 - 2.0 ** (-5.0 - np.arange(HEADS, dtype=np.float64)))
    idx = np.arange(c, dtype=np.float64)
    rel = idx[:, None] - idx[None, :]
    dmat = np.where(rel >= 0, np.exp((idx[:, None] - (c - 1.0))[None] * log_gamma[:, None, None]), 0.0)
    w_inter = np.exp((idx + 1.0)[None] * log_gamma[:, None])
    w_state = np.exp((c - 1.0 - idx)[None] * log_gamma[:, None])
    lane_head = (np.arange(QK) % LANES) // (DK // 2)
    dmat = dmat.reshape(HEADS * c, c)
    wint = np.broadcast_to(w_inter[:, :, None], (HEADS, c, DV)).reshape(HEADS * c, DV)
    wst = w_state[lane_head, :].T
    dec = np.broadcast_to(np.exp(c * log_gamma)[lane_head][:, None], (QK, DV))
    inv = ROPE_BASE ** (-np.arange(DK // 2, dtype=np.float64) / (DK // 2))
    inv = np.tile(inv, HEADS)[None, :].astype(np.float32)
    off = np.arange(TOKEN_BLOCK, dtype=np.float64)[:, None] * inv.astype(np.float64)
    f = lambda a: jnp.asarray(np.ascontiguousarray(a), F32)
    return f(inv), f(np.cos(off)), f(np.sin(off)), f(dmat), f(wint), f(wst), f(dec)


def _odd_layer(x, positions, w_in, ret_norm_g, w_pool, pool_scale, w_out, ln_g, ln_b):
    B, S, D = x.shape
    tb = TOKEN_BLOCK
    j = np.arange(2 * QK)
    src = (j // QK) * QK + ((j % LANES) // (DK // 2)) * DK + ((j % QK) // LANES) * (DK // 2) + j % (DK // 2)
    perm = jnp.asarray(src[None, :] == np.arange(2 * QK)[:, None], BF16)
    params = (w_in, perm, *_retention_tables(), _row(ret_norm_g), w_pool.astype(F32), _row(pool_scale), w_out,
              _row(ln_g), _row(ln_b))
    in_hbm = (w_in, w_out)
    pos_blocks = positions.reshape(B, S // tb, tb)
    first = pos_blocks[:, :, 0]
    runs = jnp.all(pos_blocks == first[:, :, None] + jnp.arange(tb, dtype=positions.dtype), axis=-1)
    block_info = jnp.stack([first.astype(jnp.int32), runs.astype(jnp.int32)])
    tok_spec = pl.BlockSpec((1, tb, D), lambda b, i: (b, i, 0))
    return pl.pallas_call(
        _odd_kernel,
        out_shape=jax.ShapeDtypeStruct((B, S, D), F32),
        grid=(B, S // tb),
        in_specs=[pl.BlockSpec(memory_space=pltpu.SMEM), tok_spec, _hbm_spec()]
                 + [_hbm_spec() if any(p is h for h in in_hbm) else _full_spec(p) for p in params],
        out_specs=tok_spec,
        scratch_shapes=[
            pltpu.VMEM((QK, DV), F32),
            pltpu.VMEM((POOL_PAD + tb, BRANCH), F32),
            pltpu.VMEM((tb, LANES), F32),
            pltpu.VMEM((tb, LANES), F32),
            pltpu.VMEM((1, tb), jnp.int32),
            pltpu.VMEM((D, 2 * QK), BF16),
            pltpu.VMEM((len(POOL_WINDOWS) // 2, 2 * LANES, 2 * LANES), BF16),
            pltpu.VMEM(w_in.shape, BF16),
            pltpu.VMEM(w_out.shape, BF16),
            _stage(w_in),
            _stage(w_out),
            pltpu.SemaphoreType.DMA((STAGE_SLOTS,)),
        ],
        compiler_params=_compiler_params(),
        name="odd_layer_retention_pool",
    )(block_info, x, positions.reshape(B, 1, S), *params)


def kernel(x, positions, l0_w_in, l0_w_a2, l0_b_a, l0_gla_norm_g, l0_sgu_ln_g, l0_sgu_ln_b, l0_w_s, l0_b_s, l0_w_out, l0_ln_g, l0_ln_b, l1_w_in, l1_ret_norm_g, l1_w_pool, l1_pool_scale, l1_w_out, l1_ln_g, l1_ln_b):
    x = _even_layer(x, l0_w_in, l0_w_a2, l0_b_a, l0_gla_norm_g, l0_sgu_ln_g, l0_sgu_ln_b, l0_w_s, l0_b_s,
                    l0_w_out, l0_ln_g, l0_ln_b)
    return _odd_layer(x, positions, l1_w_in, l1_ret_norm_g, l1_w_pool, l1_pool_scale, l1_w_out, l1_ln_g, l1_ln_b)
```

```python
import jax
import jax.numpy as jnp
import numpy as np
from jax import lax
from jax.experimental import pallas as pl
from jax.experimental.pallas import tpu as pltpu

F32 = jnp.float32
BF16 = jnp.bfloat16

D_MODEL = 1024
BRANCH = 512
HEADS = 4
DK = 64
DV = 128
QK = HEADS * DK
GLA_RANK = 16
GLA_TAU = 16.0
GLA_CHUNK = 64
SGU_CHUNK = 128
RET_CHUNK = 128
ROPE_BASE = 10000.0
POOL_WINDOWS = (2, 4, 8, 16)
POOL_PAD = 16
DN_ALPHA = 4.0 ** 0.25
LN_EPS = 1e-5
LANES = 128
SUBLANES = 8
MXU_TILE = 256
VMEM_LIMIT_BYTES = 56 * 1024 * 1024
TOKEN_BLOCK = 1024
NORM_ROW_GROUPS = 4
STAGE_BYTES = 1024 * 1024
STAGE_SLOTS = 4
EVEN_Q, EVEN_K, EVEN_V, EVEN_GA, EVEN_LR = 0, QK, 2 * QK, 2 * QK + BRANCH, 2 * QK + 2 * BRANCH
EVEN_U, EVEN_SV, EVEN_GB = EVEN_LR + GLA_RANK, EVEN_LR + GLA_RANK + BRANCH, EVEN_LR + GLA_RANK + 2 * BRANCH


def _dot(a, b):
    return jnp.dot(a, b, preferred_element_type=F32)


def _dot_nt(a, b):
    return lax.dot_general(a, b, (((1,), (1,)), ((), ())), preferred_element_type=F32)


def _silu(x):
    hx = 0.5 * x
    return hx + hx * jnp.tanh(hx)


def _gelu_tanh(x):
    c = np.sqrt(2.0 / np.pi)
    hx = 0.5 * x
    return hx + hx * jnp.tanh(x * (np.float32(c) + np.float32(c * 0.044715) * (x * x)))


def _log_sigmoid(z):
    return jnp.minimum(z, 0.0) - jnp.log(1.0 + jnp.exp(-jnp.abs(z)))


def _layer_norm_rows(r, g, b, eps=LN_EPS):
    mu = jnp.mean(r, axis=-1, keepdims=True)
    c = r - mu
    var = jnp.mean(c * c, axis=-1, keepdims=True)
    return c * lax.rsqrt(var + eps) * g + b


def _deepnorm(x, y_scaled, g, b):
    r = x + y_scaled
    mu = jnp.mean(r, axis=-1, keepdims=True)
    var = jnp.mean(jnp.square(r - mu), axis=-1, keepdims=True)
    a = lax.rsqrt(var + LN_EPS / DN_ALPHA ** 2)
    return (r * a - mu * a) * g + b


def _project_and_norm(x_ref, o_first, o_second, wout_ref, lng_ref, lnb_ref):
    rows_per_group = x_ref.shape[1] // NORM_ROW_GROUPS
    outs = []
    for n in range(NORM_ROW_GROUPS):
        rows = slice(n * rows_per_group, (n + 1) * rows_per_group)
        y = (_dot(o_second[rows].astype(BF16), wout_ref[BRANCH:2 * BRANCH, :])
             + _dot(o_first[rows].astype(BF16), wout_ref[0:BRANCH, :]))
        outs.append(_deepnorm(x_ref[0, rows, :], y, lng_ref[...], lnb_ref[...]))
    return jnp.concatenate(outs, axis=0)


def _head_mask(h, natural):
    lane = lax.broadcasted_iota(jnp.int32, (1, QK), 1)
    head = lane // DK if natural else (lane % LANES) // (DK // 2)
    return (head == h).astype(F32)


def _load_as_bf16(src_hbm, dst_ref, stage_ref, sem, scale=None):
    rows = src_hbm.shape[0]
    slots, per_copy = stage_ref.shape[0], stage_ref.shape[1]
    starts = list(range(0, rows, per_copy))

    def copy(k):
        n_rows = min(per_copy, rows - starts[k])
        return pltpu.make_async_copy(src_hbm.at[pl.ds(starts[k], n_rows), :],
                                     stage_ref.at[k % slots, pl.ds(0, n_rows), :], sem.at[k % slots])

    for k in range(min(slots - 1, len(starts))):
        copy(k).start()
    for k, start in enumerate(starts):
        if k + slots - 1 < len(starts):
            copy(k + slots - 1).start()
        copy(k).wait()
        n_rows = min(per_copy, rows - start)
        block = stage_ref[k % slots, 0:n_rows, :]
        if scale is not None:
            block = block * scale
        dst_ref[start:start + n_rows, :] = block.astype(BF16)


def _even_kernel(x_ref, wt_hbm, wa2_ref, ba_ref, gng_ref, slg_ref, slb_ref, ws_ref, bsg_ref, wout_hbm,
                 lng_ref, lnb_ref, o_ref, st_s, wzt_s, bs_s, wt_s, wout_s, wstage_s, ostage_s, sem):
    @pl.when((pl.program_id(0) == 0) & (pl.program_id(1) == 0))
    def _():
        _load_as_bf16(wt_hbm, wt_s, wstage_s, sem)
        _load_as_bf16(wout_hbm, wout_s, ostage_s, sem, 1.0 / DN_ALPHA)
        wa2t = jnp.concatenate([wa2_ref[...], jnp.zeros((LANES - GLA_RANK, QK), F32)], axis=0).T.astype(BF16)
        wzt_s[...] = _dot(wa2t, wt_s[EVEN_LR:EVEN_LR + LANES, :]).astype(BF16)
        for g in range(BRANCH // LANES):
            bs_s[:, g * LANES:(g + 1) * LANES] = jnp.broadcast_to(bsg_ref[g:g + 1, :], (LANES, SGU_CHUNK)).T

    @pl.when(pl.program_id(1) == 0)
    def _():
        st_s[...] = jnp.zeros_like(st_s)

    refs = (wt_s, wzt_s, ba_ref, gng_ref, slg_ref, slb_ref, ws_ref, bs_s, wout_s, lng_ref, lnb_ref)
    o_ref[0], st_s[...] = _even_block(x_ref, st_s[...], *refs)


def _even_block(x_ref, state, wt_ref, wzt_ref, ba_ref, gng_ref, slg_ref, slb_ref, ws_ref, bs_ref, wout_ref,
                lng_ref, lnb_ref):
    tb = x_ref.shape[1]
    xb = x_ref[0].astype(BF16)
    c_len = GLA_CHUNK
    t_len = SGU_CHUNK
    chunks = [slice(c * c_len, (c + 1) * c_len) for c in range(tb // c_len)]

    z = _dot_nt(xb, wzt_ref[...]) + ba_ref[...]
    q = _dot_nt(xb, wt_ref[EVEN_Q:EVEN_Q + QK, :])
    k = _dot_nt(xb, wt_ref[EVEN_K:EVEN_K + QK, :])
    la = _log_sigmoid(z) * (1.0 / GLA_TAU)
    v = _dot_nt(xb, wt_ref[EVEN_V:EVEN_V + BRANCH, :]).astype(BF16)

    row = lax.broadcasted_iota(jnp.int32, (LANES, 2 * LANES), 0)
    col = lax.broadcasted_iota(jnp.int32, (LANES, 2 * LANES), 1) % LANES
    tril2 = ((row >= col) & (row // c_len == col // c_len)).astype(BF16)
    la_hi = la.astype(BF16)
    la_lo = (la - la_hi.astype(F32)).astype(BF16)
    b = jnp.concatenate(
        [_dot(tril2, jnp.concatenate([la_hi[n * LANES:(n + 1) * LANES], la_lo[n * LANES:(n + 1) * LANES]],
                                     axis=0))
         for n in range(tb // LANES)], axis=0)

    u = _dot_nt(xb, wt_ref[EVEN_U:EVEN_U + BRANCH, :])
    q_dec = q * jnp.exp(b) * (DK ** -0.5)
    k_dec = k * jnp.exp(-b)
    masks = [_head_mask(h, True).astype(BF16) for h in range(HEADS)]
    q_dec_b = q_dec.astype(BF16)
    qms = [jnp.concatenate([q_dec_b[sl] * masks[h] for h in range(HEADS)], axis=0)
           for sl in chunks]
    sv = _dot_nt(xb, wt_ref[EVEN_SV:EVEN_SV + BRANCH, :])

    u = _gelu_tanh(u)
    kts, upds, dcols = [], [], []
    for c, sl in enumerate(chunks):
        kt = k_dec[sl].T.astype(BF16)
        kts.append(kt)
        upds.append(jnp.concatenate([_dot(kt[h * DK:(h + 1) * DK], v[sl, h * DV:(h + 1) * DV])
                                     for h in range(HEADS)], axis=0))
        b_last = b[(c + 1) * c_len - 1:(c + 1) * c_len, :]
        dcols.append(jnp.exp(jnp.broadcast_to(b_last, (LANES, QK)).T))
    gate_a = _silu(_dot_nt(xb, wt_ref[EVEN_GA:EVEN_GA + BRANCH, :])).astype(BF16)
    sv = _gelu_tanh(sv)
    slg = slg_ref[...]
    slb = slb_ref[...]
    svn = jnp.concatenate(
        [_layer_norm_rows(sv[:, g * LANES:(g + 1) * LANES], slg[:, g * LANES:(g + 1) * LANES],
                          slb[:, g * LANES:(g + 1) * LANES]) for g in range(BRANCH // LANES)],
        axis=1).astype(BF16)
    gate_b = _silu(_dot_nt(xb, wt_ref[EVEN_GB:EVEN_GB + BRANCH, :])).astype(BF16)

    states = []
    for upd, dcol in zip(upds, dcols):
        states.append(state.astype(BF16))
        state = dcol * (state + upd)

    trow = lax.broadcasted_iota(jnp.int32, (t_len, t_len), 0)
    tcol = lax.broadcasted_iota(jnp.int32, (t_len, t_len), 1)
    s_groups = []
    for g in range(BRANCH // LANES):
        w = jnp.where(trow >= tcol, ws_ref[g], 0.0).astype(BF16)
        cols = slice(g * LANES, (g + 1) * LANES)
        mixed = []
        for n in range(0, tb // t_len, 2):
            pair = _dot(w, jnp.concatenate([svn[n * t_len:(n + 1) * t_len, cols],
                                            svn[(n + 1) * t_len:(n + 2) * t_len, cols]], axis=1))
            mixed += [pair[:, :LANES], pair[:, LANES:]]
        s_groups.append(jnp.concatenate(mixed, axis=0))
    s = jnp.concatenate(s_groups, axis=1) + jnp.concatenate([bs_ref[...]] * (tb // t_len), axis=0)

    gng = gng_ref[...]
    crow = lax.broadcasted_iota(jnp.int32, (HEADS * c_len, c_len), 0) % c_len
    ccol = lax.broadcasted_iota(jnp.int32, (HEADS * c_len, c_len), 1)
    causal = crow >= ccol
    fill = jnp.zeros((QK, MXU_TILE - DV - c_len), BF16)
    fused = [_dot(qm, jnp.concatenate([s_b, kt, fill], axis=1)) for qm, s_b, kt in zip(qms, states, kts)]
    o_b = u * s * gate_b
    o_chunks = []
    for sl, f in zip(chunks, fused):
        o_inter = f[:, 0:DV]
        sc = jnp.where(causal, f[:, DV:DV + c_len], 0.0).astype(BF16)
        o_heads = []
        for h in range(HEADS):
            rows = slice(h * c_len, (h + 1) * c_len)
            o = _dot(sc[rows], v[sl, h * DV:(h + 1) * DV]) + o_inter[rows]
            o = o * lax.rsqrt(jnp.mean(o * o, axis=-1, keepdims=True) + LN_EPS)
            o_heads.append(o * gng[:, h * DV:(h + 1) * DV])
        o_chunks.append(jnp.concatenate(o_heads, axis=1))
    o_a = jnp.concatenate(o_chunks, axis=0) * gate_a
    return _project_and_norm(x_ref, o_a, o_b, wout_ref, lng_ref, lnb_ref), state


def _odd_kernel(run_ref, x_ref, pos_hbm, w_hbm, perm_ref, inv_ref, ctab_ref, stab_ref, dmat_ref, wint_ref,
                wst_ref, dec_ref, rng_ref, wpool_ref, psc_ref, wout_hbm, lng_ref, lnb_ref, o_ref, st_s, pad_s,
                cos_s, sin_s, pos_s, wqk_s, wp_ref, w_ref, wout_ref, wstage_s, ostage_s, sem):
    tb = x_ref.shape[1]
    seq = pl.program_id(0)
    i = pl.program_id(1)

    @pl.when((seq == 0) & (i == 0))
    def _():
        _load_as_bf16(w_hbm, w_ref, wstage_s, sem)
        _load_as_bf16(wout_hbm, wout_ref, ostage_s, sem, 1.0 / DN_ALPHA)
        wqk_s[...] = _dot(w_ref[:, 0:2 * QK], perm_ref[...]).astype(BF16)
        zero = jnp.zeros((LANES, LANES), F32)
        for n in range(len(POOL_WINDOWS) // 2):
            wp_ref[n] = jnp.concatenate(
                [jnp.concatenate([wpool_ref[2 * n], zero], axis=1),
                 jnp.concatenate([zero, wpool_ref[2 * n + 1]], axis=1)], axis=0).astype(BF16)

    @pl.when(i == 0)
    def _():
        st_s[...] = jnp.zeros_like(st_s)
        pad_s[0:POOL_PAD, :] = jnp.zeros((POOL_PAD, BRANCH), F32)

    c_len = RET_CHUNK
    half = DK // 2
    chunks = [slice(c * c_len, (c + 1) * c_len) for c in range(tb // c_len)]
    c_q, c_k, c_v, c_gc, c_p, c_gd = [(o, o + n) for o, n in zip(
        (0, QK, 2 * QK, 2 * QK + BRANCH, 2 * QK + 2 * BRANCH, 2 * QK + 3 * BRANCH),
        (QK, QK, BRANCH, BRANCH, BRANCH, BRANCH))]

    consecutive = run_ref[1, seq, i] == 1

    @pl.when(consecutive)
    def _():
        a0 = run_ref[0, seq, i].astype(F32) * inv_ref[...]
        c0 = jnp.cos(a0)
        s0 = jnp.sin(a0)
        cos_s[...] = c0 * ctab_ref[...] - s0 * stab_ref[...]
        sin_s[...] = s0 * ctab_ref[...] + c0 * stab_ref[...]

    @pl.when(jnp.logical_not(consecutive))
    def _():
        copy = pltpu.make_async_copy(pos_hbm.at[seq, :, pl.ds(pl.multiple_of(i * tb, tb), tb)], pos_s, sem.at[0])
        copy.start()
        copy.wait()
        pos_rows = jnp.broadcast_to(pos_s[...].astype(F32), (LANES, tb)).T
        ang = pos_rows * inv_ref[...]
        cos_s[...] = jnp.cos(ang)
        sin_s[...] = jnp.sin(ang)

    xb = x_ref[0].astype(BF16)

    q = _dot(xb, wqk_s[:, c_q[0]:c_q[1]]) * (DK ** -0.5)
    k = _dot(xb, wqk_s[:, c_k[0]:c_k[1]])
    cos = cos_s[...]
    sin = sin_s[...]
    v = _dot(xb, w_ref[:, c_v[0]:c_v[1]]).astype(BF16)
    p = _dot(xb, w_ref[:, c_p[0]:c_p[1]])
    q1, q2 = q[:, :LANES], q[:, LANES:]
    k1, k2 = k[:, :LANES], k[:, LANES:]
    qr = jnp.concatenate([q1 * cos - q2 * sin, q1 * sin + q2 * cos], axis=1)
    kr = jnp.concatenate([k1 * cos - k2 * sin, k1 * sin + k2 * cos], axis=1)
    masks = [_head_mask(h, False).astype(BF16) for h in range(HEADS)]
    qr_b = qr.astype(BF16)
    qms = [jnp.concatenate([qr_b[sl] * masks[h] for h in range(HEADS)], axis=0)
           for sl in chunks]
    gate_c = _silu(_dot(xb, w_ref[:, c_gc[0]:c_gc[1]])).astype(BF16)

    pad_s[POOL_PAD:POOL_PAD + tb, :] = p
    t_top = i * tb + lax.broadcasted_iota(jnp.int32, (POOL_PAD, LANES), 0)
    pooled = []
    for g, win in enumerate(POOL_WINDOWS):
        cols = slice(g * LANES, (g + 1) * LANES)
        acc = pad_s[:, cols]
        shift = 1
        while shift < win:
            acc = acc + pltpu.roll(acc, shift, axis=0)
            shift *= 2
        cnt_top = jnp.minimum(t_top + 1, win).astype(F32)
        mean = jnp.concatenate([acc[POOL_PAD:2 * POOL_PAD] / cnt_top, acc[2 * POOL_PAD:] * (1.0 / win)],
                               axis=0)
        pooled.append((mean - p[:, cols]).astype(BF16))
    tail = pad_s[tb:tb + POOL_PAD, :]
    pad_s[0:POOL_PAD, :] = tail
    kts, upds = [], []
    for sl in chunks:
        kt = (kr[sl] * wst_ref[...]).T.astype(BF16)
        kts.append(kt)
        upd = [_dot(jnp.concatenate([kt[h * half:(h + 1) * half],
                                     kt[LANES + h * half:LANES + (h + 1) * half]], axis=0),
                    v[sl, h * DV:(h + 1) * DV]) for h in range(HEADS)]
        upds.append(jnp.concatenate([u[:half] for u in upd] + [u[half:] for u in upd], axis=0))
    gate_d = _silu(_dot(xb, w_ref[:, c_gd[0]:c_gd[1]])).astype(BF16)

    state = st_s[...]
    states = []
    for upd in upds:
        states.append(state.astype(BF16))
        state = dec_ref[...] * state + upd
    st_s[...] = state
    yd = jnp.concatenate([_dot(jnp.concatenate([pooled[2 * n], pooled[2 * n + 1]], axis=1), wp_ref[n])
                          for n in range(len(POOL_WINDOWS) // 2)], axis=1)
    fused = [_dot(qm, jnp.concatenate([s_b, kt], axis=1)) for qm, s_b, kt in zip(qms, states, kts)]
    o_d = yd * psc_ref[...] * gate_d
    rng = rng_ref[...]
    o_chunks = []
    for sl, f in zip(chunks, fused):
        o_inter = f[:, 0:DV] * wint_ref[...]
        sc = (f[:, DV:DV + c_len] * dmat_ref[...]).astype(BF16)
        o_heads = []
        for h in range(HEADS):
            rows = slice(h * c_len, (h + 1) * c_len)
            o = _dot(sc[rows], v[sl, h * DV:(h + 1) * DV]) + o_inter[rows]
            mu = jnp.mean(o, axis=-1, keepdims=True)
            oc = o - mu
            var = jnp.mean(oc * oc, axis=-1, keepdims=True)
            o_heads.append(oc * lax.rsqrt(var + LN_EPS) * rng[:, h * DV:(h + 1) * DV])
        o_chunks.append(jnp.concatenate(o_heads, axis=1))
    o_c = jnp.concatenate(o_chunks, axis=0) * gate_c
    o_ref[0] = _project_and_norm(x_ref, o_c, o_d, wout_ref, lng_ref, lnb_ref)


def _full_spec(a):
    nd = a.ndim
    return pl.BlockSpec(a.shape, lambda *_, _nd=nd: (0,) * _nd)


def _compiler_params():
    return pltpu.CompilerParams(dimension_semantics=("arbitrary", "arbitrary"),
                                vmem_limit_bytes=VMEM_LIMIT_BYTES)


def _row(a):
    return a.reshape(1, -1).astype(F32)


def _hbm_spec():
    return pl.BlockSpec(memory_space=pl.ANY)


def _stage(w):
    rows = min(w.shape[0], STAGE_BYTES // (4 * w.shape[1]) // SUBLANES * SUBLANES)
    return pltpu.VMEM((STAGE_SLOTS, rows, w.shape[1]), F32)


def _even_layer(x, w_in, w_a2, b_a, gla_norm_g, sgu_ln_g, sgu_ln_b, w_s, b_s, w_out, ln_g, ln_b):
    B, S, D = x.shape
    tb = TOKEN_BLOCK
    w_in_t = w_in.T
    params = (w_in_t, w_a2.astype(F32), _row(b_a), _row(gla_norm_g), _row(sgu_ln_g), _row(sgu_ln_b),
              w_s.astype(F32), b_s.astype(F32), w_out, _row(ln_g), _row(ln_b))
    in_hbm = (w_in_t, w_out)
    tok_spec = pl.BlockSpec((1, tb, D), lambda b, i: (b, i, 0))
    return pl.pallas_call(
        _even_kernel,
        out_shape=jax.ShapeDtypeStruct((B, S, D), F32),
        grid=(B, S // tb),
        in_specs=[tok_spec] + [_hbm_spec() if any(p is h for h in in_hbm) else _full_spec(p) for p in params],
        out_specs=tok_spec,
        scratch_shapes=[
            pltpu.VMEM((QK, DV), F32),
            pltpu.VMEM((QK, D), BF16),
            pltpu.VMEM((SGU_CHUNK, BRANCH), F32),
            pltpu.VMEM(w_in_t.shape, BF16),
            pltpu.VMEM(w_out.shape, BF16),
            _stage(w_in_t),
            _stage(w_out),
            pltpu.SemaphoreType.DMA((STAGE_SLOTS,)),
        ],
        compiler_params=_compiler_params(),
        name="even_layer_gla_sgu",
    )(x, *params)


def _retention_tables():
    c = RET_CHUNK
    log_gamma = np.log(1.0 - 2.0 ** (-5.0 - np.arange(HEADS, dtype=np.float64)))
    idx = np.arange(c, dtype=np.float64)
    rel = idx[:, None] - idx[None, :]
    dmat = np.where(rel >= 0, np.exp((idx[:, None] - (c - 1.0))[None] * log_gamma[:, None, None]), 0.0)
    w_inter = np.exp((idx + 1.0)[None] * log_gamma[:, None])
    w_state = np.exp((c - 1.0 - idx)[None] * log_gamma[:, None])
    lane_head = (np.arange(QK) % LANES) // (DK // 2)
    dmat = dmat.reshape(HEADS * c, c)
    wint = np.broadcast_to(w_inter[:, :, None], (HEADS, c, DV)).reshape(HEADS * c, DV)
    wst = w_state[lane_head, :].T
    dec = np.broadcast_to(np.exp(c * log_gamma)[lane_head][:, None], (QK, DV))
    inv = ROPE_BASE ** (-np.arange(DK // 2, dtype=np.float64) / (DK // 2))
    inv = np.tile(inv, HEADS)[None, :].astype(np.float32)
    off = np.arange(TOKEN_BLOCK, dtype=np.float64)[:, None] * inv.astype(np.float64)
    f = lambda a: jnp.asarray(np.ascontiguousarray(a), F32)
    return f(inv), f(np.cos(off)), f(np.sin(off)), f(dmat), f(wint), f(wst), f(dec)


def _odd_layer(x, positions, w_in, ret_norm_g, w_pool, pool_scale, w_out, ln_g, ln_b):
    B, S, D = x.shape
    tb = TOKEN_BLOCK
    j = np.arange(2 * QK)
    src = (j // QK) * QK + ((j % LANES) // (DK // 2)) * DK + ((j % QK) // LANES) * (DK // 2) + j % (DK // 2)
    perm = jnp.asarray(src[None, :] == np.arange(2 * QK)[:, None], BF16)
    params = (w_in, perm, *_retention_tables(), _row(ret_norm_g), w_pool.astype(F32), _row(pool_scale), w_out,
              _row(ln_g), _row(ln_b))
    in_hbm = (w_in, w_out)
    pos_blocks = positions.reshape(B, S // tb, tb)
    first = pos_blocks[:, :, 0]
    runs = jnp.all(pos_blocks == first[:, :, None] + jnp.arange(tb, dtype=positions.dtype), axis=-1)
    block_info = jnp.stack([first.astype(jnp.int32), runs.astype(jnp.int32)])
    tok_spec = pl.BlockSpec((1, tb, D), lambda b, i: (b, i, 0))
    return pl.pallas_call(
        _odd_kernel,
        out_shape=jax.ShapeDtypeStruct((B, S, D), F32),
        grid=(B, S // tb),
        in_specs=[pl.BlockSpec(memory_space=pltpu.SMEM), tok_spec, _hbm_spec()]
                 + [_hbm_spec() if any(p is h for h in in_hbm) else _full_spec(p) for p in params],
        out_specs=tok_spec,
        scratch_shapes=[
            pltpu.VMEM((QK, DV), F32),
            pltpu.VMEM((POOL_PAD + tb, BRANCH), F32),
            pltpu.VMEM((tb, LANES), F32),
            pltpu.VMEM((tb, LANES), F32),
            pltpu.VMEM((1, tb), jnp.int32),
            pltpu.VMEM((D, 2 * QK), BF16),
            pltpu.VMEM((len(POOL_WINDOWS) // 2, 2 * LANES, 2 * LANES), BF16),
            pltpu.VMEM(w_in.shape, BF16),
            pltpu.VMEM(w_out.shape, BF16),
            _stage(w_in),
            _stage(w_out),
            pltpu.SemaphoreType.DMA((STAGE_SLOTS,)),
        ],
        compiler_params=_compiler_params(),
        name="odd_layer_retention_pool",
    )(block_info, x, positions.reshape(B, 1, S), *params)


def kernel(x, positions, l0_w_in, l0_w_a2, l0_b_a, l0_gla_norm_g, l0_sgu_ln_g, l0_sgu_ln_b, l0_w_s, l0_b_s, l0_w_out, l0_ln_g, l0_ln_b, l1_w_in, l1_ret_norm_g, l1_w_pool, l1_pool_scale, l1_w_out, l1_ln_g, l1_ln_b):
    x = _even_layer(x, l0_w_in, l0_w_a2, l0_b_a, l0_gla_norm_g, l0_sgu_ln_g, l0_sgu_ln_b, l0_w_s, l0_b_s,
                    l0_w_out, l0_ln_g, l0_ln_b)
    return _odd_layer(x, positions, l1_w_in, l1_ret_norm_g, l1_w_pool, l1_pool_scale, l1_w_out, l1_ln_g, l1_ln_b)
```

```python
import jax
import jax.numpy as jnp
import numpy as np
from jax import lax
from jax.experimental import pallas as pl
from jax.experimental.pallas import tpu as pltpu

F32 = jnp.float32
BF16 = jnp.bfloat16

D_MODEL = 1024
BRANCH = 512
HEADS = 4
DK = 64
DV = 128
QK = HEADS * DK
GLA_RANK = 16
GLA_TAU = 16.0
GLA_CHUNK = 64
SGU_CHUNK = 128
RET_CHUNK = 128
ROPE_BASE = 10000.0
POOL_WINDOWS = (2, 4, 8, 16)
POOL_PAD = 16
DN_ALPHA = 4.0 ** 0.25
LN_EPS = 1e-5
LANES = 128
SUBLANES = 8
MXU_TILE = 256
VMEM_LIMIT_BYTES = 56 * 1024 * 1024
TOKEN_BLOCK = 1024
NORM_ROW_GROUPS = 4
STAGE_BYTES = 1024 * 1024
STAGE_SLOTS = 4
EVEN_Q, EVEN_K, EVEN_V, EVEN_GA, EVEN_LR = 0, QK, 2 * QK, 2 * QK + BRANCH, 2 * QK + 2 * BRANCH
EVEN_U, EVEN_SV, EVEN_GB = EVEN_LR + GLA_RANK, EVEN_LR + GLA_RANK + BRANCH, EVEN_LR + GLA_RANK + 2 * BRANCH


def _dot(a, b):
    return jnp.dot(a, b, preferred_element_type=F32)


def _dot_nt(a, b):
    return lax.dot_general(a, b, (((1,), (1,)), ((), ())), preferred_element_type=F32)


def _silu(x):
    hx = 0.5 * x
    return hx + hx * jnp.tanh(hx)


def _gelu_tanh(x):
    c = np.sqrt(2.0 / np.pi)
    hx = 0.5 * x
    return hx + hx * jnp.tanh(x * (np.float32(c) + np.float32(c * 0.044715) * (x * x)))


def _log_sigmoid(z):
    return jnp.minimum(z, 0.0) - jnp.log(1.0 + jnp.exp(-jnp.abs(z)))


def _layer_norm_rows(r, g, b, eps=LN_EPS):
    mu = jnp.mean(r, axis=-1, keepdims=True)
    c = r - mu
    var = jnp.mean(c * c, axis=-1, keepdims=True)
    return c * lax.rsqrt(var + eps) * g + b


def _deepnorm(x, y_scaled, g, b):
    return _layer_norm_rows(x + y_scaled, g, b, LN_EPS / DN_ALPHA ** 2)


def _project_and_norm(x_ref, o_first, o_second, wout_ref, lng_ref, lnb_ref):
    rows_per_group = x_ref.shape[1] // NORM_ROW_GROUPS
    outs = []
    for n in range(NORM_ROW_GROUPS):
        rows = slice(n * rows_per_group, (n + 1) * rows_per_group)
        y = (_dot(o_second[rows].astype(BF16), wout_ref[BRANCH:2 * BRANCH, :])
             + _dot(o_first[rows].astype(BF16), wout_ref[0:BRANCH, :]))
        outs.append(_deepnorm(x_ref[0, rows, :], y, lng_ref[...], lnb_ref[...]))
    return jnp.concatenate(outs, axis=0)


def _head_mask(h, natural):
    lane = lax.broadcasted_iota(jnp.int32, (1, QK), 1)
    head = lane // DK if natural else (lane % LANES) // (DK // 2)
    return (head == h).astype(F32)


def _load_as_bf16(src_hbm, dst_ref, stage_ref, sem, scale=None):
    rows = src_hbm.shape[0]
    slots, per_copy = stage_ref.shape[0], stage_ref.shape[1]
    starts = list(range(0, rows, per_copy))

    def copy(k):
        n_rows = min(per_copy, rows - starts[k])
        return pltpu.make_async_copy(src_hbm.at[pl.ds(starts[k], n_rows), :],
                                     stage_ref.at[k % slots, pl.ds(0, n_rows), :], sem.at[k % slots])

    for k in range(min(slots - 1, len(starts))):
        copy(k).start()
    for k, start in enumerate(starts):
        if k + slots - 1 < len(starts):
            copy(k + slots - 1).start()
        copy(k).wait()
        n_rows = min(per_copy, rows - start)
        block = stage_ref[k % slots, 0:n_rows, :]
        if scale is not None:
            block = block * scale
        dst_ref[start:start + n_rows, :] = block.astype(BF16)


def _even_kernel(x_ref, wt_hbm, wa2_ref, ba_ref, gng_ref, slg_ref, slb_ref, ws_ref, bsg_ref, wout_hbm,
                 lng_ref, lnb_ref, o_ref, st_s, wzt_s, bs_s, wt_s, wout_s, wstage_s, ostage_s, sem):
    @pl.when((pl.program_id(0) == 0) & (pl.program_id(1) == 0))
    def _():
        _load_as_bf16(wt_hbm, wt_s, wstage_s, sem)
        _load_as_bf16(wout_hbm, wout_s, ostage_s, sem, 1.0 / DN_ALPHA)
        wa2t = jnp.concatenate([wa2_ref[...], jnp.zeros((LANES - GLA_RANK, QK), F32)], axis=0).T.astype(BF16)
        wzt_s[...] = _dot(wa2t, wt_s[EVEN_LR:EVEN_LR + LANES, :]).astype(BF16)
        for g in range(BRANCH // LANES):
            bs_s[:, g * LANES:(g + 1) * LANES] = jnp.broadcast_to(bsg_ref[g:g + 1, :], (LANES, SGU_CHUNK)).T

    @pl.when(pl.program_id(1) == 0)
    def _():
        st_s[...] = jnp.zeros_like(st_s)

    refs = (wt_s, wzt_s, ba_ref, gng_ref, slg_ref, slb_ref, ws_ref, bs_s, wout_s, lng_ref, lnb_ref)
    o_ref[0], st_s[...] = _even_block(x_ref, st_s[...], *refs)


def _even_block(x_ref, state, wt_ref, wzt_ref, ba_ref, gng_ref, slg_ref, slb_ref, ws_ref, bs_ref, wout_ref,
                lng_ref, lnb_ref):
    tb = x_ref.shape[1]
    xb = x_ref[0].astype(BF16)
    c_len = GLA_CHUNK
    t_len = SGU_CHUNK
    chunks = [slice(c * c_len, (c + 1) * c_len) for c in range(tb // c_len)]

    z = _dot_nt(xb, wzt_ref[...]) + ba_ref[...]
    q = _dot_nt(xb, wt_ref[EVEN_Q:EVEN_Q + QK, :])
    k = _dot_nt(xb, wt_ref[EVEN_K:EVEN_K + QK, :])
    la = _log_sigmoid(z) * (1.0 / GLA_TAU)
    v = _dot_nt(xb, wt_ref[EVEN_V:EVEN_V + BRANCH, :]).astype(BF16)

    row = lax.broadcasted_iota(jnp.int32, (LANES, 2 * LANES), 0)
    col = lax.broadcasted_iota(jnp.int32, (LANES, 2 * LANES), 1) % LANES
    tril2 = ((row >= col) & (row // c_len == col // c_len)).astype(BF16)
    la_hi = la.astype(BF16)
    la_lo = (la - la_hi.astype(F32)).astype(BF16)
    b = jnp.concatenate(
        [_dot(tril2, jnp.concatenate([la_hi[n * LANES:(n + 1) * LANES], la_lo[n * LANES:(n + 1) * LANES]],
                                     axis=0))
         for n in range(tb // LANES)], axis=0)

    u = _dot_nt(xb, wt_ref[EVEN_U:EVEN_U + BRANCH, :])
    q_dec = q * jnp.exp(b) * (DK ** -0.5)
    k_dec = k * jnp.exp(-b)
    masks = [_head_mask(h, True).astype(BF16) for h in range(HEADS)]
    q_dec_b = q_dec.astype(BF16)
    qms = [jnp.concatenate([q_dec_b[sl] * masks[h] for h in range(HEADS)], axis=0)
           for sl in chunks]
    sv = _dot_nt(xb, wt_ref[EVEN_SV:EVEN_SV + BRANCH, :])

    u = _gelu_tanh(u)
    kts, upds, dcols = [], [], []
    for c, sl in enumerate(chunks):
        kt = k_dec[sl].T.astype(BF16)
        kts.append(kt)
        upds.append(jnp.concatenate([_dot(kt[h * DK:(h + 1) * DK], v[sl, h * DV:(h + 1) * DV])
                                     for h in range(HEADS)], axis=0))
        b_last = b[(c + 1) * c_len - 1:(c + 1) * c_len, :]
        dcols.append(jnp.exp(jnp.broadcast_to(b_last, (LANES, QK)).T))
    gate_a = _silu(_dot_nt(xb, wt_ref[EVEN_GA:EVEN_GA + BRANCH, :])).astype(BF16)
    sv = _gelu_tanh(sv)
    slg = slg_ref[...]
    slb = slb_ref[...]
    svn = jnp.concatenate(
        [_layer_norm_rows(sv[:, g * LANES:(g + 1) * LANES], slg[:, g * LANES:(g + 1) * LANES],
                          slb[:, g * LANES:(g + 1) * LANES]) for g in range(BRANCH // LANES)],
        axis=1).astype(BF16)
    gate_b = _silu(_dot_nt(xb, wt_ref[EVEN_GB:EVEN_GB + BRANCH, :])).astype(BF16)

    states = []
    for upd, dcol in zip(upds, dcols):
        states.append(state.astype(BF16))
        state = dcol * (state + upd)

    trow = lax.broadcasted_iota(jnp.int32, (t_len, t_len), 0)
    tcol = lax.broadcasted_iota(jnp.int32, (t_len, t_len), 1)
    s_groups = []
    for g in range(BRANCH // LANES):
        w = jnp.where(trow >= tcol, ws_ref[g], 0.0).astype(BF16)
        cols = slice(g * LANES, (g + 1) * LANES)
        mixed = []
        for n in range(0, tb // t_len, 2):
            pair = _dot(w, jnp.concatenate([svn[n * t_len:(n + 1) * t_len, cols],
                                            svn[(n + 1) * t_len:(n + 2) * t_len, cols]], axis=1))
            mixed += [pair[:, :LANES], pair[:, LANES:]]
        s_groups.append(jnp.concatenate(mixed, axis=0))
    s = jnp.concatenate(s_groups, axis=1) + jnp.concatenate([bs_ref[...]] * (tb // t_len), axis=0)

    gng = gng_ref[...]
    crow = lax.broadcasted_iota(jnp.int32, (HEADS * c_len, c_len), 0) % c_len
    ccol = lax.broadcasted_iota(jnp.int32, (HEADS * c_len, c_len), 1)
    causal = crow >= ccol
    fill = jnp.zeros((QK, MXU_TILE - DV - c_len), BF16)
    fused = [_dot(qm, jnp.concatenate([s_b, kt, fill], axis=1)) for qm, s_b, kt in zip(qms, states, kts)]
    o_b = u * s * gate_b
    o_chunks = []
    for sl, f in zip(chunks, fused):
        o_inter = f[:, 0:DV]
        sc = jnp.where(causal, f[:, DV:DV + c_len], 0.0).astype(BF16)
        o_heads = []
        for h in range(HEADS):
            rows = slice(h * c_len, (h + 1) * c_len)
            o = _dot(sc[rows], v[sl, h * DV:(h + 1) * DV]) + o_inter[rows]
            o = o * lax.rsqrt(jnp.mean(o * o, axis=-1, keepdims=True) + LN_EPS)
            o_heads.append(o * gng[:, h * DV:(h + 1) * DV])
        o_chunks.append(jnp.concatenate(o_heads, axis=1))
    o_a = jnp.concatenate(o_chunks, axis=0) * gate_a
    return _project_and_norm(x_ref, o_a, o_b, wout_ref, lng_ref, lnb_ref), state


def _odd_kernel(run_ref, x_ref, pos_hbm, w_hbm, perm_ref, inv_ref, ctab_ref, stab_ref, dmat_ref, wint_ref,
                wst_ref, dec_ref, rng_ref, wpool_ref, psc_ref, wout_hbm, lng_ref, lnb_ref, o_ref, st_s, pad_s,
                cos_s, sin_s, pos_s, wqk_s, wp_ref, w_ref, wout_ref, wstage_s, ostage_s, sem):
    tb = x_ref.shape[1]
    seq = pl.program_id(0)
    i = pl.program_id(1)

    @pl.when((seq == 0) & (i == 0))
    def _():
        _load_as_bf16(w_hbm, w_ref, wstage_s, sem)
        _load_as_bf16(wout_hbm, wout_ref, ostage_s, sem, 1.0 / DN_ALPHA)
        wqk_s[...] = _dot(w_ref[:, 0:2 * QK], perm_ref[...]).astype(BF16)
        zero = jnp.zeros((LANES, LANES), F32)
        for n in range(len(POOL_WINDOWS) // 2):
            wp_ref[n] = jnp.concatenate(
                [jnp.concatenate([wpool_ref[2 * n], zero], axis=1),
                 jnp.concatenate([zero, wpool_ref[2 * n + 1]], axis=1)], axis=0).astype(BF16)

    @pl.when(i == 0)
    def _():
        st_s[...] = jnp.zeros_like(st_s)
        pad_s[...] = jnp.zeros_like(pad_s)

    c_len = RET_CHUNK
    half = DK // 2
    chunks = [slice(c * c_len, (c + 1) * c_len) for c in range(tb // c_len)]
    c_q, c_k, c_v, c_gc, c_p, c_gd = [(o, o + n) for o, n in zip(
        (0, QK, 2 * QK, 2 * QK + BRANCH, 2 * QK + 2 * BRANCH, 2 * QK + 3 * BRANCH),
        (QK, QK, BRANCH, BRANCH, BRANCH, BRANCH))]

    consecutive = run_ref[1, seq, i] == 1

    @pl.when(consecutive)
    def _():
        a0 = run_ref[0, seq, i].astype(F32) * inv_ref[...]
        c0 = jnp.cos(a0)
        s0 = jnp.sin(a0)
        cos_s[...] = c0 * ctab_ref[...] - s0 * stab_ref[...]
        sin_s[...] = s0 * ctab_ref[...] + c0 * stab_ref[...]

    @pl.when(jnp.logical_not(consecutive))
    def _():
        copy = pltpu.make_async_copy(pos_hbm.at[seq, :, pl.ds(pl.multiple_of(i * tb, tb), tb)], pos_s, sem.at[0])
        copy.start()
        copy.wait()
        pos_rows = jnp.broadcast_to(pos_s[...].astype(F32), (LANES, tb)).T
        ang = pos_rows * inv_ref[...]
        cos_s[...] = jnp.cos(ang)
        sin_s[...] = jnp.sin(ang)

    xb = x_ref[0].astype(BF16)

    q = _dot(xb, wqk_s[:, c_q[0]:c_q[1]]) * (DK ** -0.5)
    k = _dot(xb, wqk_s[:, c_k[0]:c_k[1]])
    cos = cos_s[...]
    sin = sin_s[...]
    v = _dot(xb, w_ref[:, c_v[0]:c_v[1]]).astype(BF16)
    p = _dot(xb, w_ref[:, c_p[0]:c_p[1]])
    q1, q2 = q[:, :LANES], q[:, LANES:]
    k1, k2 = k[:, :LANES], k[:, LANES:]
    qr = jnp.concatenate([q1 * cos - q2 * sin, q1 * sin + q2 * cos], axis=1)
    kr = jnp.concatenate([k1 * cos - k2 * sin, k1 * sin + k2 * cos], axis=1)
    masks = [_head_mask(h, False).astype(BF16) for h in range(HEADS)]
    qr_b = qr.astype(BF16)
    qms = [jnp.concatenate([qr_b[sl] * masks[h] for h in range(HEADS)], axis=0)
           for sl in chunks]
    gate_c = _silu(_dot(xb, w_ref[:, c_gc[0]:c_gc[1]])).astype(BF16)

    history = pad_s[...]
    t_top = i * tb + lax.broadcasted_iota(jnp.int32, (POOL_PAD, LANES), 0)
    pooled = []
    for g, win in enumerate(POOL_WINDOWS):
        cols = slice(g * LANES, (g + 1) * LANES)
        acc = jnp.concatenate([history[:, cols], p[:, cols]], axis=0)
        shift = 1
        while shift < win:
            acc = acc + pltpu.roll(acc, shift, axis=0)
            shift *= 2
        cnt_top = jnp.minimum(t_top + 1, win).astype(F32)
        mean = jnp.concatenate([acc[POOL_PAD:2 * POOL_PAD] / cnt_top, acc[2 * POOL_PAD:] * (1.0 / win)],
                               axis=0)
        pooled.append((mean - p[:, cols]).astype(BF16))
    pad_s[...] = p[tb - POOL_PAD:, :]
    kts, upds = [], []
    for sl in chunks:
        kt = (kr[sl] * wst_ref[...]).T.astype(BF16)
        kts.append(kt)
        upd = [_dot(jnp.concatenate([kt[h * half:(h + 1) * half],
                                     kt[LANES + h * half:LANES + (h + 1) * half]], axis=0),
                    v[sl, h * DV:(h + 1) * DV]) for h in range(HEADS)]
        upds.append(jnp.concatenate([u[:half] for u in upd] + [u[half:] for u in upd], axis=0))
    gate_d = _silu(_dot(xb, w_ref[:, c_gd[0]:c_gd[1]])).astype(BF16)

    state = st_s[...]
    states = []
    for upd in upds:
        states.append(state.astype(BF16))
        state = dec_ref[...] * state + upd
    st_s[...] = state
    yd = jnp.concatenate([_dot(jnp.concatenate([pooled[2 * n], pooled[2 * n + 1]], axis=1), wp_ref[n])
                          for n in range(len(POOL_WINDOWS) // 2)], axis=1)
    fused = [_dot(qm, jnp.concatenate([s_b, kt], axis=1)) for qm, s_b, kt in zip(qms, states, kts)]
    o_d = yd * psc_ref[...] * gate_d
    rng = rng_ref[...]
    o_chunks = []
    for sl, f in zip(chunks, fused):
        o_inter = f[:, 0:DV] * wint_ref[...]
        sc = (f[:, DV:DV + c_len] * dmat_ref[...]).astype(BF16)
        o_heads = []
        for h in range(HEADS):
            rows = slice(h * c_len, (h + 1) * c_len)
            o = _dot(sc[rows], v[sl, h * DV:(h + 1) * DV]) + o_inter[rows]
            mu = jnp.mean(o, axis=-1, keepdims=True)
            oc = o - mu
            var = jnp.mean(oc * oc, axis=-1, keepdims=True)
            o_heads.append(oc * lax.rsqrt(var + LN_EPS) * rng[:, h * DV:(h + 1) * DV])
        o_chunks.append(jnp.concatenate(o_heads, axis=1))
    o_c = jnp.concatenate(o_chunks, axis=0) * gate_c
    o_ref[0] = _project_and_norm(x_ref, o_c, o_d, wout_ref, lng_ref, lnb_ref)


def _full_spec(a):
    nd = a.ndim
    return pl.BlockSpec(a.shape, lambda *_, _nd=nd: (0,) * _nd)


def _compiler_params():
    return pltpu.CompilerParams(dimension_semantics=("arbitrary", "arbitrary"),
                                vmem_limit_bytes=VMEM_LIMIT_BYTES)


def _row(a):
    return a.reshape(1, -1).astype(F32)


def _hbm_spec():
    return pl.BlockSpec(memory_space=pl.ANY)


def _stage(w):
    rows = min(w.shape[0], STAGE_BYTES // (4 * w.shape[1]) // SUBLANES * SUBLANES)
    return pltpu.VMEM((STAGE_SLOTS, rows, w.shape[1]), F32)


def _even_layer(x, w_in, w_a2, b_a, gla_norm_g, sgu_ln_g, sgu_ln_b, w_s, b_s, w_out, ln_g, ln_b):
    B, S, D = x.shape
    tb = TOKEN_BLOCK
    w_in_t = w_in.T
    params = (w_in_t, w_a2.astype(F32), _row(b_a), _row(gla_norm_g), _row(sgu_ln_g), _row(sgu_ln_b),
              w_s.astype(F32), b_s.astype(F32), w_out, _row(ln_g), _row(ln_b))
    in_hbm = (w_in_t, w_out)
    tok_spec = pl.BlockSpec((1, tb, D), lambda b, i: (b, i, 0))
    return pl.pallas_call(
        _even_kernel,
        out_shape=jax.ShapeDtypeStruct((B, S, D), F32),
        grid=(B, S // tb),
        in_specs=[tok_spec] + [_hbm_spec() if any(p is h for h in in_hbm) else _full_spec(p) for p in params],
        out_specs=tok_spec,
        scratch_shapes=[
            pltpu.VMEM((QK, DV), F32),
            pltpu.VMEM((QK, D), BF16),
            pltpu.VMEM((SGU_CHUNK, BRANCH), F32),
            pltpu.VMEM(w_in_t.shape, BF16),
            pltpu.VMEM(w_out.shape, BF16),
            _stage(w_in_t),
            _stage(w_out),
            pltpu.SemaphoreType.DMA((STAGE_SLOTS,)),
        ],
        compiler_params=_compiler_params(),
        name="even_layer_gla_sgu",
    )(x, *params)


def _retention_tables():
    c = RET_CHUNK
    log_gamma = np.log(1.0 - 2.0 ** (-5.0 - np.arange(HEADS, dtype=np.float64)))
    idx = np.arange(c, dtype=np.float64)
    rel = idx[:, None] - idx[None, :]
    dmat = np.where(rel >= 0, np.exp((idx[:, None] - (c - 1.0))[None] * log_gamma[:, None, None]), 0.0)
    w_inter = np.exp((idx + 1.0)[None] * log_gamma[:, None])
    w_state = np.exp((c - 1.0 - idx)[None] * log_gamma[:, None])
    lane_head = (np.arange(QK) % LANES) // (DK // 2)
    dmat = dmat.reshape(HEADS * c, c)
    wint = np.broadcast_to(w_inter[:, :, None], (HEADS, c, DV)).reshape(HEADS * c, DV)
    wst = w_state[lane_head, :].T
    dec = np.broadcast_to(np.exp(c * log_gamma)[lane_head][:, None], (QK, DV))
    inv = ROPE_BASE ** (-np.arange(DK // 2, dtype=np.float64) / (DK // 2))
    inv = np.tile(inv, HEADS)[None, :].astype(np.float32)
    off = np.arange(TOKEN_BLOCK, dtype=np.float64)[:, None] * inv.astype(np.float64)
    f = lambda a: jnp.asarray(np.ascontiguousarray(a), F32)
    return f(inv), f(np.cos(off)), f(np.sin(off)), f(dmat), f(wint), f(wst), f(dec)


def _odd_layer(x, positions, w_in, ret_norm_g, w_pool, pool_scale, w_out, ln_g, ln_b):
    B, S, D = x.shape
    tb = TOKEN_BLOCK
    j = np.arange(2 * QK)
    src = (j // QK) * QK + ((j % LANES) // (DK // 2)) * DK + ((j % QK) // LANES) * (DK // 2) + j % (DK // 2)
    perm = jnp.asarray(src[None, :] == np.arange(2 * QK)[:, None], BF16)
    params = (w_in, perm, *_retention_tables(), _row(ret_norm_g), w_pool.astype(F32), _row(pool_scale), w_out,
              _row(ln_g), _row(ln_b))
    in_hbm = (w_in, w_out)
    pos_blocks = positions.reshape(B, S // tb, tb)
    first = pos_blocks[:, :, 0]
    runs = jnp.all(pos_blocks == first[:, :, None] + jnp.arange(tb, dtype=positions.dtype), axis=-1)
    block_info = jnp.stack([first.astype(jnp.int32), runs.astype(jnp.int32)])
    tok_spec = pl.BlockSpec((1, tb, D), lambda b, i: (b, i, 0))
    return pl.pallas_call(
        _odd_kernel,
        out_shape=jax.ShapeDtypeStruct((B, S, D), F32),
        grid=(B, S // tb),
        in_specs=[pl.BlockSpec(memory_space=pltpu.SMEM), tok_spec, _hbm_spec()]
                 + [_hbm_spec() if any(p is h for h in in_hbm) else _full_spec(p) for p in params],
        out_specs=tok_spec,
        scratch_shapes=[
            pltpu.VMEM((QK, DV), F32),
            pltpu.VMEM((POOL_PAD, BRANCH), F32),
            pltpu.VMEM((tb, LANES), F32),
            pltpu.VMEM((tb, LANES), F32),
            pltpu.VMEM((1, tb), jnp.int32),
            pltpu.VMEM((D, 2 * QK), BF16),
            pltpu.VMEM((len(POOL_WINDOWS) // 2, 2 * LANES, 2 * LANES), BF16),
            pltpu.VMEM(w_in.shape, BF16),
            pltpu.VMEM(w_out.shape, BF16),
            _stage(w_in),
            _stage(w_out),
            pltpu.SemaphoreType.DMA((STAGE_SLOTS,)),
        ],
        compiler_params=_compiler_params(),
        name="odd_layer_retention_pool",
    )(block_info, x, positions.reshape(B, 1, S), *params)


def kernel(x, positions, l0_w_in, l0_w_a2, l0_b_a, l0_gla_norm_g, l0_sgu_ln_g, l0_sgu_ln_b, l0_w_s, l0_b_s, l0_w_out, l0_ln_g, l0_ln_b, l1_w_in, l1_ret_norm_g, l1_w_pool, l1_pool_scale, l1_w_out, l1_ln_g, l1_ln_b):
    x = _even_layer(x, l0_w_in, l0_w_a2, l0_b_a, l0_gla_norm_g, l0_sgu_ln_g, l0_sgu_ln_b, l0_w_s, l0_b_s,
                    l0_w_out, l0_ln_g, l0_ln_b)
    return _odd_layer(x, positions, l1_w_in, l1_ret_norm_g, l1_w_pool, l1_pool_scale, l1_w_out, l1_ln_g, l1_ln_b)
```

```python
import jax
import jax.numpy as jnp
import numpy as np
from jax import lax
from jax.experimental import pallas as pl
from jax.experimental.pallas import tpu as pltpu

F32 = jnp.float32
BF16 = jnp.bfloat16

D_MODEL = 1024
BRANCH = 512
HEADS = 4
DK = 64
DV = 128
QK = HEADS * DK
GLA_RANK = 16
GLA_TAU = 16.0
GLA_CHUNK = 64
SGU_CHUNK = 128
RET_CHUNK = 128
ROPE_BASE = 10000.0
POOL_WINDOWS = (2, 4, 8, 16)
POOL_PAD = 16
DN_ALPHA = 4.0 ** 0.25
LN_EPS = 1e-5
LANES = 128
SUBLANES = 8
MXU_TILE = 256
VMEM_LIMIT_BYTES = 56 * 1024 * 1024
TOKEN_BLOCK = 1024
NORM_ROW_GROUPS = 4
STAGE_BYTES = 1024 * 1024
STAGE_SLOTS = 4
EVEN_Q, EVEN_K, EVEN_V, EVEN_GA, EVEN_LR = 0, QK, 2 * QK, 2 * QK + BRANCH, 2 * QK + 2 * BRANCH
EVEN_U, EVEN_SV, EVEN_GB = EVEN_LR + GLA_RANK, EVEN_LR + GLA_RANK + BRANCH, EVEN_LR + GLA_RANK + 2 * BRANCH


def _dot(a, b):
    return jnp.dot(a, b, preferred_element_type=F32)


def _dot_nt(a, b):
    return lax.dot_general(a, b, (((1,), (1,)), ((), ())), preferred_element_type=F32)


def _silu(x):
    hx = 0.5 * x
    return hx + hx * jnp.tanh(hx)


def _gelu_tanh(x):
    c = np.sqrt(2.0 / np.pi)
    hx = 0.5 * x
    return hx + hx * jnp.tanh(x * (np.float32(c) + np.float32(c * 0.044715) * (x * x)))


def _log_sigmoid(z):
    return jnp.minimum(z, 0.0) - jnp.log(1.0 + jnp.exp(-jnp.abs(z)))


def _layer_norm_rows(r, g, b, eps=LN_EPS):
    mu = jnp.mean(r, axis=-1, keepdims=True)
    c = r - mu
    var = jnp.mean(c * c, axis=-1, keepdims=True)
    return c * lax.rsqrt(var + eps) * g + b


def _deepnorm(x, y_scaled, g, b):
    return _layer_norm_rows(x + y_scaled, g, b, LN_EPS / DN_ALPHA ** 2)


def _project_and_norm(x_ref, o_first, o_second, wout_ref, lng_ref, lnb_ref):
    rows_per_group = x_ref.shape[1] // NORM_ROW_GROUPS
    outs = []
    for n in range(NORM_ROW_GROUPS):
        rows = slice(n * rows_per_group, (n + 1) * rows_per_group)
        y = (_dot(o_second[rows].astype(BF16), wout_ref[BRANCH:2 * BRANCH, :])
             + _dot(o_first[rows].astype(BF16), wout_ref[0:BRANCH, :]))
        outs.append(_deepnorm(x_ref[0, rows, :], y, lng_ref[...], lnb_ref[...]))
    return jnp.concatenate(outs, axis=0)


def _head_mask(h, natural):
    lane = lax.broadcasted_iota(jnp.int32, (1, QK), 1)
    head = lane // DK if natural else (lane % LANES) // (DK // 2)
    return (head == h).astype(F32)


def _load_as_bf16(src_hbm, dst_ref, stage_ref, sem, scale=None):
    rows = src_hbm.shape[0]
    slots, per_copy = stage_ref.shape[0], stage_ref.shape[1]
    starts = list(range(0, rows, per_copy))

    def copy(k):
        n_rows = min(per_copy, rows - starts[k])
        return pltpu.make_async_copy(src_hbm.at[pl.ds(starts[k], n_rows), :],
                                     stage_ref.at[k % slots, pl.ds(0, n_rows), :], sem.at[k % slots])

    for k in range(min(slots - 1, len(starts))):
        copy(k).start()
    for k, start in enumerate(starts):
        if k + slots - 1 < len(starts):
            copy(k + slots - 1).start()
        copy(k).wait()
        n_rows = min(per_copy, rows - start)
        block = stage_ref[k % slots, 0:n_rows, :]
        if scale is not None:
            block = block * scale
        dst_ref[start:start + n_rows, :] = block.astype(BF16)


def _even_kernel(x_ref, wt_hbm, wa2_ref, ba_ref, gng_ref, slg_ref, slb_ref, ws_ref, bsg_ref, wout_hbm,
                 lng_ref, lnb_ref, o_ref, st_s, wzt_s, bs_s, wt_s, wout_s, wstage_s, ostage_s, sem):
    @pl.when((pl.program_id(0) == 0) & (pl.program_id(1) == 0))
    def _():
        _load_as_bf16(wt_hbm, wt_s, wstage_s, sem)
        _load_as_bf16(wout_hbm, wout_s, ostage_s, sem, 1.0 / DN_ALPHA)
        wa2t = jnp.concatenate([wa2_ref[...], jnp.zeros((LANES - GLA_RANK, QK), F32)], axis=0).T.astype(BF16)
        wzt_s[...] = _dot(wa2t, wt_s[EVEN_LR:EVEN_LR + LANES, :]).astype(BF16)
        for g in range(BRANCH // LANES):
            bs_s[:, g * LANES:(g + 1) * LANES] = jnp.broadcast_to(bsg_ref[g:g + 1, :], (LANES, SGU_CHUNK)).T

    @pl.when(pl.program_id(1) == 0)
    def _():
        st_s[...] = jnp.zeros_like(st_s)

    refs = (wt_s, wzt_s, ba_ref, gng_ref, slg_ref, slb_ref, ws_ref, bs_s, wout_s, lng_ref, lnb_ref)
    o_ref[0], st_s[...] = _even_block(x_ref, st_s[...], *refs)


def _even_block(x_ref, state, wt_ref, wzt_ref, ba_ref, gng_ref, slg_ref, slb_ref, ws_ref, bs_ref, wout_ref,
                lng_ref, lnb_ref):
    tb = x_ref.shape[1]
    xb = x_ref[0].astype(BF16)
    c_len = GLA_CHUNK
    t_len = SGU_CHUNK
    chunks = [slice(c * c_len, (c + 1) * c_len) for c in range(tb // c_len)]

    z = _dot_nt(xb, wzt_ref[...]) + ba_ref[...]
    q = _dot_nt(xb, wt_ref[EVEN_Q:EVEN_Q + QK, :])
    k = _dot_nt(xb, wt_ref[EVEN_K:EVEN_K + QK, :])
    la = _log_sigmoid(z) * (1.0 / GLA_TAU)
    v = _dot_nt(xb, wt_ref[EVEN_V:EVEN_V + BRANCH, :]).astype(BF16)

    row = lax.broadcasted_iota(jnp.int32, (LANES, 2 * LANES), 0)
    col = lax.broadcasted_iota(jnp.int32, (LANES, 2 * LANES), 1) % LANES
    tril2 = ((row >= col) & (row // c_len == col // c_len)).astype(BF16)
    la_hi = la.astype(BF16)
    la_lo = (la - la_hi.astype(F32)).astype(BF16)
    b = jnp.concatenate(
        [_dot(tril2, jnp.concatenate([la_hi[n * LANES:(n + 1) * LANES], la_lo[n * LANES:(n + 1) * LANES]],
                                     axis=0))
         for n in range(tb // LANES)], axis=0)

    u = _dot_nt(xb, wt_ref[EVEN_U:EVEN_U + BRANCH, :])
    q_dec = q * jnp.exp(b) * (DK ** -0.5)
    k_dec = k * jnp.exp(-b)
    masks = [_head_mask(h, True).astype(BF16) for h in range(HEADS)]
    q_dec_b = q_dec.astype(BF16)
    qms = [jnp.concatenate([q_dec_b[sl] * masks[h] for h in range(HEADS)], axis=0)
           for sl in chunks]
    sv = _dot_nt(xb, wt_ref[EVEN_SV:EVEN_SV + BRANCH, :])

    u = _gelu_tanh(u)
    kts, upds, dcols = [], [], []
    for c, sl in enumerate(chunks):
        kt = k_dec[sl].T.astype(BF16)
        kts.append(kt)
        upds.append(jnp.concatenate([_dot(kt[h * DK:(h + 1) * DK], v[sl, h * DV:(h + 1) * DV])
                                     for h in range(HEADS)], axis=0))
        b_last = b[(c + 1) * c_len - 1:(c + 1) * c_len, :]
        dcols.append(jnp.exp(jnp.broadcast_to(b_last, (LANES, QK)).T))
    gate_a = _silu(_dot_nt(xb, wt_ref[EVEN_GA:EVEN_GA + BRANCH, :])).astype(BF16)
    sv = _gelu_tanh(sv)
    slg = slg_ref[...]
    slb = slb_ref[...]
    svn = jnp.concatenate(
        [_layer_norm_rows(sv[:, g * LANES:(g + 1) * LANES], slg[:, g * LANES:(g + 1) * LANES],
                          slb[:, g * LANES:(g + 1) * LANES]) for g in range(BRANCH // LANES)],
        axis=1).astype(BF16)
    gate_b = _silu(_dot_nt(xb, wt_ref[EVEN_GB:EVEN_GB + BRANCH, :])).astype(BF16)

    states = []
    for upd, dcol in zip(upds, dcols):
        states.append(state.astype(BF16))
        state = dcol * (state + upd)

    trow = lax.broadcasted_iota(jnp.int32, (t_len, t_len), 0)
    tcol = lax.broadcasted_iota(jnp.int32, (t_len, t_len), 1)
    s_groups = []
    for g in range(BRANCH // LANES):
        w = jnp.where(trow >= tcol, ws_ref[g], 0.0).astype(BF16)
        cols = slice(g * LANES, (g + 1) * LANES)
        mixed = []
        for n in range(0, tb // t_len, 2):
            pair = _dot(w, jnp.concatenate([svn[n * t_len:(n + 1) * t_len, cols],
                                            svn[(n + 1) * t_len:(n + 2) * t_len, cols]], axis=1))
            mixed += [pair[:, :LANES], pair[:, LANES:]]
        s_groups.append(jnp.concatenate(mixed, axis=0))
    s = jnp.concatenate(s_groups, axis=1) + jnp.concatenate([bs_ref[...]] * (tb // t_len), axis=0)

    gng = gng_ref[...]
    crow = lax.broadcasted_iota(jnp.int32, (HEADS * c_len, c_len), 0) % c_len
    ccol = lax.broadcasted_iota(jnp.int32, (HEADS * c_len, c_len), 1)
    causal = crow >= ccol
    fill = jnp.zeros((QK, MXU_TILE - DV - c_len), BF16)
    fused = [_dot(qm, jnp.concatenate([s_b, kt, fill], axis=1)) for qm, s_b, kt in zip(qms, states, kts)]
    o_b = u * s * gate_b
    o_chunks = []
    for sl, f in zip(chunks, fused):
        o_inter = f[:, 0:DV]
        sc = jnp.where(causal, f[:, DV:DV + c_len], 0.0).astype(BF16)
        o_heads = []
        for h in range(HEADS):
            rows = slice(h * c_len, (h + 1) * c_len)
            o = _dot(sc[rows], v[sl, h * DV:(h + 1) * DV]) + o_inter[rows]
            o = o * lax.rsqrt(jnp.mean(o * o, axis=-1, keepdims=True) + LN_EPS)
            o_heads.append(o * gng[:, h * DV:(h + 1) * DV])
        o_chunks.append(jnp.concatenate(o_heads, axis=1))
    o_a = jnp.concatenate(o_chunks, axis=0) * gate_a
    return _project_and_norm(x_ref, o_a, o_b, wout_ref, lng_ref, lnb_ref), state


def _odd_kernel(run_ref, x_ref, pos_hbm, w_hbm, perm_ref, inv_ref, ctab_ref, stab_ref, dmat_ref, wint_ref,
                wst_ref, dec_ref, rng_ref, wpool_ref, psc_ref, wout_hbm, lng_ref, lnb_ref, o_ref, st_s, pad_s,
                tab_s, rot_s, pos_s, wqk_s, wp_ref, w_ref, wout_ref, wstage_s, ostage_s, sem):
    tb = x_ref.shape[1]
    seq = pl.program_id(0)
    i = pl.program_id(1)

    @pl.when((seq == 0) & (i == 0))
    def _():
        _load_as_bf16(w_hbm, w_ref, wstage_s, sem)
        _load_as_bf16(wout_hbm, wout_ref, ostage_s, sem, 1.0 / DN_ALPHA)
        wqk_s[...] = _dot(w_ref[:, 0:2 * QK], perm_ref[...]).astype(BF16)
        tab_s[0, 0] = ctab_ref[...]
        tab_s[0, 1] = stab_ref[...]
        tab_s[1] = jnp.zeros(tab_s.shape[1:], F32)
        zero = jnp.zeros((LANES, LANES), F32)
        for n in range(len(POOL_WINDOWS) // 2):
            wp_ref[n] = jnp.concatenate(
                [jnp.concatenate([wpool_ref[2 * n], zero], axis=1),
                 jnp.concatenate([zero, wpool_ref[2 * n + 1]], axis=1)], axis=0).astype(BF16)

    @pl.when(i == 0)
    def _():
        st_s[...] = jnp.zeros_like(st_s)
        pad_s[0:POOL_PAD, :] = jnp.zeros((POOL_PAD, BRANCH), F32)

    c_len = RET_CHUNK
    half = DK // 2
    chunks = [slice(c * c_len, (c + 1) * c_len) for c in range(tb // c_len)]
    c_q, c_k, c_v, c_gc, c_p, c_gd = [(o, o + n) for o, n in zip(
        (0, QK, 2 * QK, 2 * QK + BRANCH, 2 * QK + 2 * BRANCH, 2 * QK + 3 * BRANCH),
        (QK, QK, BRANCH, BRANCH, BRANCH, BRANCH))]

    consecutive = run_ref[1, seq, i] == 1

    @pl.when(consecutive)
    def _():
        a0 = run_ref[0, seq, i].astype(F32) * inv_ref[...]
        rot_s[0:1, :] = jnp.cos(a0)
        rot_s[1:2, :] = jnp.sin(a0)

    @pl.when(jnp.logical_not(consecutive))
    def _():
        copy = pltpu.make_async_copy(pos_hbm.at[seq, :, pl.ds(pl.multiple_of(i * tb, tb), tb)], pos_s, sem.at[0])
        copy.start()
        copy.wait()
        pos_rows = jnp.broadcast_to(pos_s[...].astype(F32), (LANES, tb)).T
        ang = pos_rows * inv_ref[...]
        tab_s[1, 0] = jnp.cos(ang)
        tab_s[1, 1] = jnp.sin(ang)
        rot_s[0:1, :] = jnp.ones((1, LANES), F32)
        rot_s[1:2, :] = jnp.zeros((1, LANES), F32)

    xb = x_ref[0].astype(BF16)

    q = _dot(xb, wqk_s[:, c_q[0]:c_q[1]]) * (DK ** -0.5)
    k = _dot(xb, wqk_s[:, c_k[0]:c_k[1]])
    slot = jnp.where(consecutive, 0, 1)
    c_tab, s_tab = tab_s[slot, 0], tab_s[slot, 1]
    c_rot, s_rot = rot_s[0:1, :], rot_s[1:2, :]
    cos = c_rot * c_tab - s_rot * s_tab
    sin = s_rot * c_tab + c_rot * s_tab
    v = _dot(xb, w_ref[:, c_v[0]:c_v[1]]).astype(BF16)
    p = _dot(xb, w_ref[:, c_p[0]:c_p[1]])
    q1, q2 = q[:, :LANES], q[:, LANES:]
    k1, k2 = k[:, :LANES], k[:, LANES:]
    qr = jnp.concatenate([q1 * cos - q2 * sin, q1 * sin + q2 * cos], axis=1)
    kr = jnp.concatenate([k1 * cos - k2 * sin, k1 * sin + k2 * cos], axis=1)
    masks = [_head_mask(h, False).astype(BF16) for h in range(HEADS)]
    qr_b = qr.astype(BF16)
    qms = [jnp.concatenate([qr_b[sl] * masks[h] for h in range(HEADS)], axis=0)
           for sl in chunks]
    gate_c = _silu(_dot(xb, w_ref[:, c_gc[0]:c_gc[1]])).astype(BF16)

    pad_s[POOL_PAD:POOL_PAD + tb, :] = p
    t_top = i * tb + lax.broadcasted_iota(jnp.int32, (POOL_PAD, LANES), 0)
    pooled = []
    for g, win in enumerate(POOL_WINDOWS):
        cols = slice(g * LANES, (g + 1) * LANES)
        acc = pad_s[:, cols]
        shift = 1
        while shift < win:
            acc = acc + pltpu.roll(acc, shift, axis=0)
            shift *= 2
        cnt_top = jnp.minimum(t_top + 1, win).astype(F32)
        mean = jnp.concatenate([acc[POOL_PAD:2 * POOL_PAD] / cnt_top, acc[2 * POOL_PAD:] * (1.0 / win)],
                               axis=0)
        pooled.append((mean - p[:, cols]).astype(BF16))
    tail = pad_s[tb:tb + POOL_PAD, :]
    pad_s[0:POOL_PAD, :] = tail
    kts, upds = [], []
    for sl in chunks:
        kt = (kr[sl] * wst_ref[...]).T.astype(BF16)
        kts.append(kt)
        upd = [_dot(jnp.concatenate([kt[h * half:(h + 1) * half],
                                     kt[LANES + h * half:LANES + (h + 1) * half]], axis=0),
                    v[sl, h * DV:(h + 1) * DV]) for h in range(HEADS)]
        upds.append(jnp.concatenate([u[:half] for u in upd] + [u[half:] for u in upd], axis=0))
    gate_d = _silu(_dot(xb, w_ref[:, c_gd[0]:c_gd[1]])).astype(BF16)

    state = st_s[...]
    states = []
    for upd in upds:
        states.append(state.astype(BF16))
        state = dec_ref[...] * state + upd
    st_s[...] = state
    yd = jnp.concatenate([_dot(jnp.concatenate([pooled[2 * n], pooled[2 * n + 1]], axis=1), wp_ref[n])
                          for n in range(len(POOL_WINDOWS) // 2)], axis=1)
    fused = [_dot(qm, jnp.concatenate([s_b, kt], axis=1)) for qm, s_b, kt in zip(qms, states, kts)]
    o_d = yd * psc_ref[...] * gate_d
    rng = rng_ref[...]
    o_chunks = []
    for sl, f in zip(chunks, fused):
        o_inter = f[:, 0:DV] * wint_ref[...]
        sc = (f[:, DV:DV + c_len] * dmat_ref[...]).astype(BF16)
        o_heads = []
        for h in range(HEADS):
            rows = slice(h * c_len, (h + 1) * c_len)
            o = _dot(sc[rows], v[sl, h * DV:(h + 1) * DV]) + o_inter[rows]
            mu = jnp.mean(o, axis=-1, keepdims=True)
            oc = o - mu
            var = jnp.mean(oc * oc, axis=-1, keepdims=True)
            o_heads.append(oc * lax.rsqrt(var + LN_EPS) * rng[:, h * DV:(h + 1) * DV])
        o_chunks.append(jnp.concatenate(o_heads, axis=1))
    o_c = jnp.concatenate(o_chunks, axis=0) * gate_c
    o_ref[0] = _project_and_norm(x_ref, o_c, o_d, wout_ref, lng_ref, lnb_ref)


def _full_spec(a):
    nd = a.ndim
    return pl.BlockSpec(a.shape, lambda *_, _nd=nd: (0,) * _nd)


def _compiler_params():
    return pltpu.CompilerParams(dimension_semantics=("arbitrary", "arbitrary"),
                                vmem_limit_bytes=VMEM_LIMIT_BYTES)


def _row(a):
    return a.reshape(1, -1).astype(F32)


def _hbm_spec():
    return pl.BlockSpec(memory_space=pl.ANY)


def _stage(w):
    rows = min(w.shape[0], STAGE_BYTES // (4 * w.shape[1]) // SUBLANES * SUBLANES)
    return pltpu.VMEM((STAGE_SLOTS, rows, w.shape[1]), F32)


def _even_layer(x, w_in, w_a2, b_a, gla_norm_g, sgu_ln_g, sgu_ln_b, w_s, b_s, w_out, ln_g, ln_b):
    B, S, D = x.shape
    tb = TOKEN_BLOCK
    w_in_t = w_in.T
    params = (w_in_t, w_a2.astype(F32), _row(b_a), _row(gla_norm_g), _row(sgu_ln_g), _row(sgu_ln_b),
              w_s.astype(F32), b_s.astype(F32), w_out, _row(ln_g), _row(ln_b))
    in_hbm = (w_in_t, w_out)
    tok_spec = pl.BlockSpec((1, tb, D), lambda b, i: (b, i, 0))
    return pl.pallas_call(
        _even_kernel,
        out_shape=jax.ShapeDtypeStruct((B, S, D), F32),
        grid=(B, S // tb),
        in_specs=[tok_spec] + [_hbm_spec() if any(p is h for h in in_hbm) else _full_spec(p) for p in params],
        out_specs=tok_spec,
        scratch_shapes=[
            pltpu.VMEM((QK, DV), F32),
            pltpu.VMEM((QK, D), BF16),
            pltpu.VMEM((SGU_CHUNK, BRANCH), F32),
            pltpu.VMEM(w_in_t.shape, BF16),
            pltpu.VMEM(w_out.shape, BF16),
            _stage(w_in_t),
            _stage(w_out),
            pltpu.SemaphoreType.DMA((STAGE_SLOTS,)),
        ],
        compiler_params=_compiler_params(),
        name="even_layer_gla_sgu",
    )(x, *params)


def _retention_tables():
    c = RET_CHUNK
    log_gamma = np.log(1.0 - 2.0 ** (-5.0 - np.arange(HEADS, dtype=np.float64)))
    idx = np.arange(c, dtype=np.float64)
    rel = idx[:, None] - idx[None, :]
    dmat = np.where(rel >= 0, np.exp((idx[:, None] - (c - 1.0))[None] * log_gamma[:, None, None]), 0.0)
    w_inter = np.exp((idx + 1.0)[None] * log_gamma[:, None])
    w_state = np.exp((c - 1.0 - idx)[None] * log_gamma[:, None])
    lane_head = (np.arange(QK) % LANES) // (DK // 2)
    dmat = dmat.reshape(HEADS * c, c)
    wint = np.broadcast_to(w_inter[:, :, None], (HEADS, c, DV)).reshape(HEADS * c, DV)
    wst = w_state[lane_head, :].T
    dec = np.broadcast_to(np.exp(c * log_gamma)[lane_head][:, None], (QK, DV))
    inv = ROPE_BASE ** (-np.arange(DK // 2, dtype=np.float64) / (DK // 2))
    inv = np.tile(inv, HEADS)[None, :].astype(np.float32)
    off = np.arange(TOKEN_BLOCK, dtype=np.float64)[:, None] * inv.astype(np.float64)
    f = lambda a: jnp.asarray(np.ascontiguousarray(a), F32)
    return f(inv), f(np.cos(off)), f(np.sin(off)), f(dmat), f(wint), f(wst), f(dec)


def _odd_layer(x, positions, w_in, ret_norm_g, w_pool, pool_scale, w_out, ln_g, ln_b):
    B, S, D = x.shape
    tb = TOKEN_BLOCK
    j = np.arange(2 * QK)
    src = (j // QK) * QK + ((j % LANES) // (DK // 2)) * DK + ((j % QK) // LANES) * (DK // 2) + j % (DK // 2)
    perm = jnp.asarray(src[None, :] == np.arange(2 * QK)[:, None], BF16)
    params = (w_in, perm, *_retention_tables(), _row(ret_norm_g), w_pool.astype(F32), _row(pool_scale), w_out,
              _row(ln_g), _row(ln_b))
    in_hbm = (w_in, w_out)
    pos_blocks = positions.reshape(B, S // tb, tb)
    first = pos_blocks[:, :, 0]
    runs = jnp.all(pos_blocks == first[:, :, None] + jnp.arange(tb, dtype=positions.dtype), axis=-1)
    block_info = jnp.stack([first.astype(jnp.int32), runs.astype(jnp.int32)])
    tok_spec = pl.BlockSpec((1, tb, D), lambda b, i: (b, i, 0))
    return pl.pallas_call(
        _odd_kernel,
        out_shape=jax.ShapeDtypeStruct((B, S, D), F32),
        grid=(B, S // tb),
        in_specs=[pl.BlockSpec(memory_space=pltpu.SMEM), tok_spec, _hbm_spec()]
                 + [_hbm_spec() if any(p is h for h in in_hbm) else _full_spec(p) for p in params],
        out_specs=tok_spec,
        scratch_shapes=[
            pltpu.VMEM((QK, DV), F32),
            pltpu.VMEM((POOL_PAD + tb, BRANCH), F32),
            pltpu.VMEM((2, 2, tb, LANES), F32),
            pltpu.VMEM((2, LANES), F32),
            pltpu.VMEM((1, tb), jnp.int32),
            pltpu.VMEM((D, 2 * QK), BF16),
            pltpu.VMEM((len(POOL_WINDOWS) // 2, 2 * LANES, 2 * LANES), BF16),
            pltpu.VMEM(w_in.shape, BF16),
            pltpu.VMEM(w_out.shape, BF16),
            _stage(w_in),
            _stage(w_out),
            pltpu.SemaphoreType.DMA((STAGE_SLOTS,)),
        ],
        compiler_params=_compiler_params(),
        name="odd_layer_retention_pool",
    )(block_info, x, positions.reshape(B, 1, S), *params)


def kernel(x, positions, l0_w_in, l0_w_a2, l0_b_a, l0_gla_norm_g, l0_sgu_ln_g, l0_sgu_ln_b, l0_w_s, l0_b_s, l0_w_out, l0_ln_g, l0_ln_b, l1_w_in, l1_ret_norm_g, l1_w_pool, l1_pool_scale, l1_w_out, l1_ln_g, l1_ln_b):
    x = _even_layer(x, l0_w_in, l0_w_a2, l0_b_a, l0_gla_norm_g, l0_sgu_ln_g, l0_sgu_ln_b, l0_w_s, l0_b_s,
                    l0_w_out, l0_ln_g, l0_ln_b)
    return _odd_layer(x, positions, l1_w_in, l1_ret_norm_g, l1_w_pool, l1_pool_scale, l1_w_out, l1_ln_g, l1_ln_b)
```

```python
import jax
import jax.numpy as jnp
import numpy as np
from jax import lax
from jax.experimental import pallas as pl
from jax.experimental.pallas import tpu as pltpu

F32 = jnp.float32
BF16 = jnp.bfloat16

D_MODEL = 1024
BRANCH = 512
HEADS = 4
DK = 64
DV = 128
QK = HEADS * DK
GLA_RANK = 16
GLA_TAU = 16.0
GLA_CHUNK = 64
SGU_CHUNK = 128
RET_CHUNK = 128
ROPE_BASE = 10000.0
POOL_WINDOWS = (2, 4, 8, 16)
POOL_PAD = 16
DN_ALPHA = 4.0 ** 0.25
LN_EPS = 1e-5
LANES = 128
SUBLANES = 8
MXU_TILE = 256
VMEM_LIMIT_BYTES = 56 * 1024 * 1024
TOKEN_BLOCK = 512
NORM_ROW_GROUPS = 4
STAGE_BYTES = 1024 * 1024
STAGE_SLOTS = 4
EVEN_Q, EVEN_K, EVEN_V, EVEN_GA, EVEN_LR = 0, QK, 2 * QK, 2 * QK + BRANCH, 2 * QK + 2 * BRANCH
EVEN_U, EVEN_SV, EVEN_GB = EVEN_LR + GLA_RANK, EVEN_LR + GLA_RANK + BRANCH, EVEN_LR + GLA_RANK + 2 * BRANCH


def _dot(a, b):
    return jnp.dot(a, b, preferred_element_type=F32)


def _dot_nt(a, b):
    return lax.dot_general(a, b, (((1,), (1,)), ((), ())), preferred_element_type=F32)


def _silu(x):
    hx = 0.5 * x
    return hx + hx * jnp.tanh(hx)


def _gelu_tanh(x):
    c = np.sqrt(2.0 / np.pi)
    hx = 0.5 * x
    return hx + hx * jnp.tanh(x * (np.float32(c) + np.float32(c * 0.044715) * (x * x)))


def _log_sigmoid(z):
    return jnp.minimum(z, 0.0) - jnp.log(1.0 + jnp.exp(-jnp.abs(z)))


def _layer_norm_rows(r, g, b, eps=LN_EPS):
    mu = jnp.mean(r, axis=-1, keepdims=True)
    c = r - mu
    var = jnp.mean(c * c, axis=-1, keepdims=True)
    return c * lax.rsqrt(var + eps) * g + b


def _deepnorm(x, y_scaled, g, b):
    return _layer_norm_rows(x + y_scaled, g, b, LN_EPS / DN_ALPHA ** 2)


def _project_and_norm(x_ref, o_first, o_second, wout_ref, lng_ref, lnb_ref):
    rows_per_group = x_ref.shape[1] // NORM_ROW_GROUPS
    outs = []
    for n in range(NORM_ROW_GROUPS):
        rows = slice(n * rows_per_group, (n + 1) * rows_per_group)
        y = (_dot(o_second[rows].astype(BF16), wout_ref[BRANCH:2 * BRANCH, :])
             + _dot(o_first[rows].astype(BF16), wout_ref[0:BRANCH, :]))
        outs.append(_deepnorm(x_ref[0, rows, :], y, lng_ref[...], lnb_ref[...]))
    return jnp.concatenate(outs, axis=0)


def _head_mask(h, natural):
    lane = lax.broadcasted_iota(jnp.int32, (1, QK), 1)
    head = lane // DK if natural else (lane % LANES) // (DK // 2)
    return (head == h).astype(F32)


def _load_as_bf16(src_hbm, dst_ref, stage_ref, sem, scale=None):
    rows = src_hbm.shape[0]
    slots, per_copy = stage_ref.shape[0], stage_ref.shape[1]
    starts = list(range(0, rows, per_copy))

    def copy(k):
        n_rows = min(per_copy, rows - starts[k])
        return pltpu.make_async_copy(src_hbm.at[pl.ds(starts[k], n_rows), :],
                                     stage_ref.at[k % slots, pl.ds(0, n_rows), :], sem.at[k % slots])

    for k in range(min(slots - 1, len(starts))):
        copy(k).start()
    for k, start in enumerate(starts):
        if k + slots - 1 < len(starts):
            copy(k + slots - 1).start()
        copy(k).wait()
        n_rows = min(per_copy, rows - start)
        block = stage_ref[k % slots, 0:n_rows, :]
        if scale is not None:
            block = block * scale
        dst_ref[start:start + n_rows, :] = block.astype(BF16)


def _when_for(part):
    return pl.when if part == "branches" else (lambda cond: (lambda block: None))


def _fused_kernel(run_ref, x_ref, pos_hbm, *refs):
    even_in, odd_in, o_ref = refs[0:11], refs[11:26], refs[26]
    even_s, (odd_st, pad_s, tab_s, rot_s, pos_s, wqk_s, wp_s, w1_s, wout1_s, wstage1_s), mid_s = (
        refs[27:35], refs[35:45], refs[45])
    ostage_s, sem = even_s[6], even_s[7]
    odd_s = (odd_st, pad_s, tab_s, rot_s, pos_s, wqk_s, wp_s, w1_s, wout1_s, wstage1_s, ostage_s, sem)
    for part in ("branches", "main"):
        if part == "branches":
            _even_kernel(x_ref, *even_in, mid_s, *even_s, part=part)
            _odd_kernel(run_ref, mid_s, pos_hbm, *odd_in, o_ref, *odd_s, part=part)
        else:
            _even_kernel(x_ref, *even_in, mid_s, *even_s, part=part)
            _odd_kernel(run_ref, mid_s, pos_hbm, *odd_in, o_ref, *odd_s, part=part)


def _even_kernel(x_ref, wt_hbm, wa2_ref, ba_ref, gng_ref, slg_ref, slb_ref, ws_ref, bsg_ref, wout_hbm,
                 lng_ref, lnb_ref, o_ref, st_s, wzt_s, bs_s, wt_s, wout_s, wstage_s, ostage_s, sem, *, part):
    when = _when_for(part)

    @when((pl.program_id(0) == 0) & (pl.program_id(1) == 0))
    def _():
        _load_as_bf16(wt_hbm, wt_s, wstage_s, sem)
        _load_as_bf16(wout_hbm, wout_s, ostage_s, sem, 1.0 / DN_ALPHA)
        wa2t = jnp.concatenate([wa2_ref[...], jnp.zeros((LANES - GLA_RANK, QK), F32)], axis=0).T.astype(BF16)
        wzt_s[...] = _dot(wa2t, wt_s[EVEN_LR:EVEN_LR + LANES, :]).astype(BF16)
        for g in range(BRANCH // LANES):
            bs_s[:, g * LANES:(g + 1) * LANES] = jnp.broadcast_to(bsg_ref[g:g + 1, :], (LANES, SGU_CHUNK)).T

    @when(pl.program_id(1) == 0)
    def _():
        st_s[...] = jnp.zeros_like(st_s)

    if part == "branches":
        return
    refs = (wt_s, wzt_s, ba_ref, gng_ref, slg_ref, slb_ref, ws_ref, bs_s, wout_s, lng_ref, lnb_ref)
    o_ref[0], st_s[...] = _even_block(x_ref, st_s[...], *refs)


def _even_block(x_ref, state, wt_ref, wzt_ref, ba_ref, gng_ref, slg_ref, slb_ref, ws_ref, bs_ref, wout_ref,
                lng_ref, lnb_ref):
    tb = x_ref.shape[1]
    xb = x_ref[0].astype(BF16)
    c_len = GLA_CHUNK
    t_len = SGU_CHUNK
    chunks = [slice(c * c_len, (c + 1) * c_len) for c in range(tb // c_len)]

    z = _dot_nt(xb, wzt_ref[...]) + ba_ref[...]
    q = _dot_nt(xb, wt_ref[EVEN_Q:EVEN_Q + QK, :])
    k = _dot_nt(xb, wt_ref[EVEN_K:EVEN_K + QK, :])
    la = _log_sigmoid(z) * (1.0 / GLA_TAU)
    v = _dot_nt(xb, wt_ref[EVEN_V:EVEN_V + BRANCH, :]).astype(BF16)

    row = lax.broadcasted_iota(jnp.int32, (LANES, 2 * LANES), 0)
    col = lax.broadcasted_iota(jnp.int32, (LANES, 2 * LANES), 1) % LANES
    tril2 = ((row >= col) & (row // c_len == col // c_len)).astype(BF16)
    la_hi = la.astype(BF16)
    la_lo = (la - la_hi.astype(F32)).astype(BF16)
    b = jnp.concatenate(
        [_dot(tril2, jnp.concatenate([la_hi[n * LANES:(n + 1) * LANES], la_lo[n * LANES:(n + 1) * LANES]],
                                     axis=0))
         for n in range(tb // LANES)], axis=0)

    u = _dot_nt(xb, wt_ref[EVEN_U:EVEN_U + BRANCH, :])
    q_dec = q * jnp.exp(b) * (DK ** -0.5)
    k_dec = k * jnp.exp(-b)
    masks = [_head_mask(h, True).astype(BF16) for h in range(HEADS)]
    q_dec_b = q_dec.astype(BF16)
    qms = [jnp.concatenate([q_dec_b[sl] * masks[h] for h in range(HEADS)], axis=0)
           for sl in chunks]
    sv = _dot_nt(xb, wt_ref[EVEN_SV:EVEN_SV + BRANCH, :])

    u = _gelu_tanh(u)
    kts, upds, dcols = [], [], []
    for c, sl in enumerate(chunks):
        kt = k_dec[sl].T.astype(BF16)
        kts.append(kt)
        upds.append(jnp.concatenate([_dot(kt[h * DK:(h + 1) * DK], v[sl, h * DV:(h + 1) * DV])
                                     for h in range(HEADS)], axis=0))
        b_last = b[(c + 1) * c_len - 1:(c + 1) * c_len, :]
        dcols.append(jnp.exp(jnp.broadcast_to(b_last, (LANES, QK)).T))
    gate_a = _silu(_dot_nt(xb, wt_ref[EVEN_GA:EVEN_GA + BRANCH, :])).astype(BF16)
    sv = _gelu_tanh(sv)
    slg = slg_ref[...]
    slb = slb_ref[...]
    svn = jnp.concatenate(
        [_layer_norm_rows(sv[:, g * LANES:(g + 1) * LANES], slg[:, g * LANES:(g + 1) * LANES],
                          slb[:, g * LANES:(g + 1) * LANES]) for g in range(BRANCH // LANES)],
        axis=1).astype(BF16)
    gate_b = _silu(_dot_nt(xb, wt_ref[EVEN_GB:EVEN_GB + BRANCH, :])).astype(BF16)

    states = []
    for upd, dcol in zip(upds, dcols):
        states.append(state.astype(BF16))
        state = dcol * (state + upd)

    trow = lax.broadcasted_iota(jnp.int32, (t_len, t_len), 0)
    tcol = lax.broadcasted_iota(jnp.int32, (t_len, t_len), 1)
    s_groups = []
    for g in range(BRANCH // LANES):
        w = jnp.where(trow >= tcol, ws_ref[g], 0.0).astype(BF16)
        cols = slice(g * LANES, (g + 1) * LANES)
        mixed = []
        for n in range(0, tb // t_len, 2):
            pair = _dot(w, jnp.concatenate([svn[n * t_len:(n + 1) * t_len, cols],
                                            svn[(n + 1) * t_len:(n + 2) * t_len, cols]], axis=1))
            mixed += [pair[:, :LANES], pair[:, LANES:]]
        s_groups.append(jnp.concatenate(mixed, axis=0))
    s = jnp.concatenate(s_groups, axis=1) + jnp.concatenate([bs_ref[...]] * (tb // t_len), axis=0)

    gng = gng_ref[...]
    crow = lax.broadcasted_iota(jnp.int32, (HEADS * c_len, c_len), 0) % c_len
    ccol = lax.broadcasted_iota(jnp.int32, (HEADS * c_len, c_len), 1)
    causal = crow >= ccol
    fill = jnp.zeros((QK, MXU_TILE - DV - c_len), BF16)
    fused = [_dot(qm, jnp.concatenate([s_b, kt, fill], axis=1)) for qm, s_b, kt in zip(qms, states, kts)]
    o_b = u * s * gate_b
    o_chunks = []
    for sl, f in zip(chunks, fused):
        o_inter = f[:, 0:DV]
        sc = jnp.where(causal, f[:, DV:DV + c_len], 0.0).astype(BF16)
        o_heads = []
        for h in range(HEADS):
            rows = slice(h * c_len, (h + 1) * c_len)
            o = _dot(sc[rows], v[sl, h * DV:(h + 1) * DV]) + o_inter[rows]
            o = o * lax.rsqrt(jnp.mean(o * o, axis=-1, keepdims=True) + LN_EPS)
            o_heads.append(o * gng[:, h * DV:(h + 1) * DV])
        o_chunks.append(jnp.concatenate(o_heads, axis=1))
    o_a = jnp.concatenate(o_chunks, axis=0) * gate_a
    return _project_and_norm(x_ref, o_a, o_b, wout_ref, lng_ref, lnb_ref), state


def _odd_kernel(run_ref, x_ref, pos_hbm, w_hbm, perm_ref, inv_ref, ctab_ref, stab_ref, dmat_ref, wint_ref,
                wst_ref, dec_ref, rng_ref, wpool_ref, psc_ref, wout_hbm, lng_ref, lnb_ref, o_ref, st_s, pad_s,
                tab_s, rot_s, pos_s, wqk_s, wp_ref, w_ref, wout_ref, wstage_s, ostage_s, sem, *, part):
    when = _when_for(part)
    tb = x_ref.shape[1]
    seq = pl.program_id(0)
    i = pl.program_id(1)

    @when((seq == 0) & (i == 0))
    def _():
        _load_as_bf16(w_hbm, w_ref, wstage_s, sem)
        _load_as_bf16(wout_hbm, wout_ref, ostage_s, sem, 1.0 / DN_ALPHA)
        wqk_s[...] = _dot(w_ref[:, 0:2 * QK], perm_ref[...]).astype(BF16)
        tab_s[0, 0] = ctab_ref[...]
        tab_s[0, 1] = stab_ref[...]
        tab_s[1] = jnp.zeros(tab_s.shape[1:], F32)
        zero = jnp.zeros((LANES, LANES), F32)
        for n in range(len(POOL_WINDOWS) // 2):
            wp_ref[n] = jnp.concatenate(
                [jnp.concatenate([wpool_ref[2 * n], zero], axis=1),
                 jnp.concatenate([zero, wpool_ref[2 * n + 1]], axis=1)], axis=0).astype(BF16)

    @when(i == 0)
    def _():
        st_s[...] = jnp.zeros_like(st_s)
        pad_s[0:POOL_PAD, :] = jnp.zeros((POOL_PAD, BRANCH), F32)

    c_len = RET_CHUNK
    half = DK // 2
    chunks = [slice(c * c_len, (c + 1) * c_len) for c in range(tb // c_len)]
    c_q, c_k, c_v, c_gc, c_p, c_gd = [(o, o + n) for o, n in zip(
        (0, QK, 2 * QK, 2 * QK + BRANCH, 2 * QK + 2 * BRANCH, 2 * QK + 3 * BRANCH),
        (QK, QK, BRANCH, BRANCH, BRANCH, BRANCH))]

    consecutive = run_ref[1, seq, i] == 1

    @when(consecutive)
    def _():
        a0 = run_ref[0, seq, i].astype(F32) * inv_ref[...]
        rot_s[0:1, :] = jnp.cos(a0)
        rot_s[1:2, :] = jnp.sin(a0)

    @when(jnp.logical_not(consecutive))
    def _():
        copy = pltpu.make_async_copy(pos_hbm.at[seq, :, pl.ds(pl.multiple_of(i * tb, tb), tb)], pos_s, sem.at[0])
        copy.start()
        copy.wait()
        pos_rows = jnp.broadcast_to(pos_s[...].astype(F32), (LANES, tb)).T
        ang = pos_rows * inv_ref[...]
        tab_s[1, 0] = jnp.cos(ang)
        tab_s[1, 1] = jnp.sin(ang)
        rot_s[0:1, :] = jnp.ones((1, LANES), F32)
        rot_s[1:2, :] = jnp.zeros((1, LANES), F32)

    if part == "branches":
        return
    xb = x_ref[0].astype(BF16)

    q = _dot(xb, wqk_s[:, c_q[0]:c_q[1]]) * (DK ** -0.5)
    k = _dot(xb, wqk_s[:, c_k[0]:c_k[1]])
    slot = jnp.where(consecutive, 0, 1)
    c_tab, s_tab = tab_s[slot, 0], tab_s[slot, 1]
    c_rot, s_rot = rot_s[0:1, :], rot_s[1:2, :]
    cos = c_rot * c_tab - s_rot * s_tab
    sin = s_rot * c_tab + c_rot * s_tab
    v = _dot(xb, w_ref[:, c_v[0]:c_v[1]]).astype(BF16)
    p = _dot(xb, w_ref[:, c_p[0]:c_p[1]])
    q1, q2 = q[:, :LANES], q[:, LANES:]
    k1, k2 = k[:, :LANES], k[:, LANES:]
    qr = jnp.concatenate([q1 * cos - q2 * sin, q1 * sin + q2 * cos], axis=1)
    kr = jnp.concatenate([k1 * cos - k2 * sin, k1 * sin + k2 * cos], axis=1)
    masks = [_head_mask(h, False).astype(BF16) for h in range(HEADS)]
    qr_b = qr.astype(BF16)
    qms = [jnp.concatenate([qr_b[sl] * masks[h] for h in range(HEADS)], axis=0)
           for sl in chunks]
    gate_c = _silu(_dot(xb, w_ref[:, c_gc[0]:c_gc[1]])).astype(BF16)

    pad_s[POOL_PAD:POOL_PAD + tb, :] = p
    t_top = i * tb + lax.broadcasted_iota(jnp.int32, (POOL_PAD, LANES), 0)
    pooled = []
    for g, win in enumerate(POOL_WINDOWS):
        cols = slice(g * LANES, (g + 1) * LANES)
        acc = pad_s[:, cols]
        shift = 1
        while shift < win:
            acc = acc + pltpu.roll(acc, shift, axis=0)
            shift *= 2
        cnt_top = jnp.minimum(t_top + 1, win).astype(F32)
        mean = jnp.concatenate([acc[POOL_PAD:2 * POOL_PAD] / cnt_top, acc[2 * POOL_PAD:] * (1.0 / win)],
                               axis=0)
        pooled.append((mean - p[:, cols]).astype(BF16))
    tail = pad_s[tb:tb + POOL_PAD, :]
    pad_s[0:POOL_PAD, :] = tail
    kts, upds = [], []
    for sl in chunks:
        kt = (kr[sl] * wst_ref[...]).T.astype(BF16)
        kts.append(kt)
        upd = [_dot(jnp.concatenate([kt[h * half:(h + 1) * half],
                                     kt[LANES + h * half:LANES + (h + 1) * half]], axis=0),
                    v[sl, h * DV:(h + 1) * DV]) for h in range(HEADS)]
        upds.append(jnp.concatenate([u[:half] for u in upd] + [u[half:] for u in upd], axis=0))
    gate_d = _silu(_dot(xb, w_ref[:, c_gd[0]:c_gd[1]])).astype(BF16)

    state = st_s[...]
    states = []
    for upd in upds:
        states.append(state.astype(BF16))
        state = dec_ref[...] * state + upd
    st_s[...] = state
    yd = jnp.concatenate([_dot(jnp.concatenate([pooled[2 * n], pooled[2 * n + 1]], axis=1), wp_ref[n])
                          for n in range(len(POOL_WINDOWS) // 2)], axis=1)
    fused = [_dot(qm, jnp.concatenate([s_b, kt], axis=1)) for qm, s_b, kt in zip(qms, states, kts)]
    o_d = yd * psc_ref[...] * gate_d
    rng = rng_ref[...]
    o_chunks = []
    for sl, f in zip(chunks, fused):
        o_inter = f[:, 0:DV] * wint_ref[...]
        sc = (f[:, DV:DV + c_len] * dmat_ref[...]).astype(BF16)
        o_heads = []
        for h in range(HEADS):
            rows = slice(h * c_len, (h + 1) * c_len)
            o = _dot(sc[rows], v[sl, h * DV:(h + 1) * DV]) + o_inter[rows]
            mu = jnp.mean(o, axis=-1, keepdims=True)
            oc = o - mu
            var = jnp.mean(oc * oc, axis=-1, keepdims=True)
            o_heads.append(oc * lax.rsqrt(var + LN_EPS) * rng[:, h * DV:(h + 1) * DV])
        o_chunks.append(jnp.concatenate(o_heads, axis=1))
    o_c = jnp.concatenate(o_chunks, axis=0) * gate_c
    o_ref[0] = _project_and_norm(x_ref, o_c, o_d, wout_ref, lng_ref, lnb_ref)


def _full_spec(a):
    nd = a.ndim
    return pl.BlockSpec(a.shape, lambda *_, _nd=nd: (0,) * _nd)


def _compiler_params():
    return pltpu.CompilerParams(dimension_semantics=("arbitrary", "arbitrary"),
                                vmem_limit_bytes=VMEM_LIMIT_BYTES)


def _row(a):
    return a.reshape(1, -1).astype(F32)


def _hbm_spec():
    return pl.BlockSpec(memory_space=pl.ANY)


def _stage(w):
    rows = min(w.shape[0], STAGE_BYTES // (4 * w.shape[1]) // SUBLANES * SUBLANES)
    return pltpu.VMEM((STAGE_SLOTS, rows, w.shape[1]), F32)


def _even_layer(x, w_in, w_a2, b_a, gla_norm_g, sgu_ln_g, sgu_ln_b, w_s, b_s, w_out, ln_g, ln_b):
    B, S, D = x.shape
    tb = TOKEN_BLOCK
    w_in_t = w_in.T
    params = (w_in_t, w_a2.astype(F32), _row(b_a), _row(gla_norm_g), _row(sgu_ln_g), _row(sgu_ln_b),
              w_s.astype(F32), b_s.astype(F32), w_out, _row(ln_g), _row(ln_b))
    in_hbm = (w_in_t, w_out)
    tok_spec = pl.BlockSpec((1, tb, D), lambda b, i: (b, i, 0))
    return pl.pallas_call(
        _even_kernel,
        out_shape=jax.ShapeDtypeStruct((B, S, D), F32),
        grid=(B, S // tb),
        in_specs=[tok_spec] + [_hbm_spec() if any(p is h for h in in_hbm) else _full_spec(p) for p in params],
        out_specs=tok_spec,
        scratch_shapes=[
            pltpu.VMEM((QK, DV), F32),
            pltpu.VMEM((QK, D), BF16),
            pltpu.VMEM((SGU_CHUNK, BRANCH), F32),
            pltpu.VMEM(w_in_t.shape, BF16),
            pltpu.VMEM(w_out.shape, BF16),
            _stage(w_in_t),
            _stage(w_out),
            pltpu.SemaphoreType.DMA((STAGE_SLOTS,)),
        ],
        compiler_params=_compiler_params(),
        name="even_layer_gla_sgu",
    )(x, *params)


def _retention_tables():
    c = RET_CHUNK
    log_gamma = np.log(1.0 - 2.0 ** (-5.0 - np.arange(HEADS, dtype=np.float64)))
    idx = np.arange(c, dtype=np.float64)
    rel = idx[:, None] - idx[None, :]
    dmat = np.where(rel >= 0, np.exp((idx[:, None] - (c - 1.0))[None] * log_gamma[:, None, None]), 0.0)
    w_inter = np.exp((idx + 1.0)[None] * log_gamma[:, None])
    w_state = np.exp((c - 1.0 - idx)[None] * log_gamma[:, None])
    lane_head = (np.arange(QK) % LANES) // (DK // 2)
    dmat = dmat.reshape(HEADS * c, c)
    wint = np.broadcast_to(w_inter[:, :, None], (HEADS, c, DV)).reshape(HEADS * c, DV)
    wst = w_state[lane_head, :].T
    dec = np.broadcast_to(np.exp(c * log_gamma)[lane_head][:, None], (QK, DV))
    inv = ROPE_BASE ** (-np.arange(DK // 2, dtype=np.float64) / (DK // 2))
    inv = np.tile(inv, HEADS)[None, :].astype(np.float32)
    off = np.arange(TOKEN_BLOCK, dtype=np.float64)[:, None] * inv.astype(np.float64)
    f = lambda a: jnp.asarray(np.ascontiguousarray(a), F32)
    return f(inv), f(np.cos(off)), f(np.sin(off)), f(dmat), f(wint), f(wst), f(dec)


def _odd_layer(x, positions, w_in, ret_norm_g, w_pool, pool_scale, w_out, ln_g, ln_b):
    B, S, D = x.shape
    tb = TOKEN_BLOCK
    j = np.arange(2 * QK)
    src = (j // QK) * QK + ((j % LANES) // (DK // 2)) * DK + ((j % QK) // LANES) * (DK // 2) + j % (DK // 2)
    perm = jnp.asarray(src[None, :] == np.arange(2 * QK)[:, None], BF16)
    params = (w_in, perm, *_retention_tables(), _row(ret_norm_g), w_pool.astype(F32), _row(pool_scale), w_out,
              _row(ln_g), _row(ln_b))
    in_hbm = (w_in, w_out)
    pos_blocks = positions.reshape(B, S // tb, tb)
    first = pos_blocks[:, :, 0]
    runs = jnp.all(pos_blocks == first[:, :, None] + jnp.arange(tb, dtype=positions.dtype), axis=-1)
    block_info = jnp.stack([first.astype(jnp.int32), runs.astype(jnp.int32)])
    tok_spec = pl.BlockSpec((1, tb, D), lambda b, i: (b, i, 0))
    return pl.pallas_call(
        _odd_kernel,
        out_shape=jax.ShapeDtypeStruct((B, S, D), F32),
        grid=(B, S // tb),
        in_specs=[pl.BlockSpec(memory_space=pltpu.SMEM), tok_spec, _hbm_spec()]
                 + [_hbm_spec() if any(p is h for h in in_hbm) else _full_spec(p) for p in params],
        out_specs=tok_spec,
        scratch_shapes=[
            pltpu.VMEM((QK, DV), F32),
            pltpu.VMEM((POOL_PAD + tb, BRANCH), F32),
            pltpu.VMEM((2, 2, tb, LANES), F32),
            pltpu.VMEM((2, LANES), F32),
            pltpu.VMEM((1, tb), jnp.int32),
            pltpu.VMEM((D, 2 * QK), BF16),
            pltpu.VMEM((len(POOL_WINDOWS) // 2, 2 * LANES, 2 * LANES), BF16),
            pltpu.VMEM(w_in.shape, BF16),
            pltpu.VMEM(w_out.shape, BF16),
            _stage(w_in),
            _stage(w_out),
            pltpu.SemaphoreType.DMA((STAGE_SLOTS,)),
        ],
        compiler_params=_compiler_params(),
        name="odd_layer_retention_pool",
    )(block_info, x, positions.reshape(B, 1, S), *params)


def kernel(x, positions, l0_w_in, l0_w_a2, l0_b_a, l0_gla_norm_g, l0_sgu_ln_g, l0_sgu_ln_b, l0_w_s, l0_b_s, l0_w_out, l0_ln_g, l0_ln_b, l1_w_in, l1_ret_norm_g, l1_w_pool, l1_pool_scale, l1_w_out, l1_ln_g, l1_ln_b):
    B, S, D = x.shape
    tb = TOKEN_BLOCK
    w0_t = l0_w_in.T
    even = (w0_t, l0_w_a2.astype(F32), _row(l0_b_a), _row(l0_gla_norm_g), _row(l0_sgu_ln_g), _row(l0_sgu_ln_b),
            l0_w_s.astype(F32), l0_b_s.astype(F32), l0_w_out, _row(l0_ln_g), _row(l0_ln_b))
    j = np.arange(2 * QK)
    src = (j // QK) * QK + ((j % LANES) // (DK // 2)) * DK + ((j % QK) // LANES) * (DK // 2) + j % (DK // 2)
    perm = jnp.asarray(src[None, :] == np.arange(2 * QK)[:, None], BF16)
    odd = (l1_w_in, perm, *_retention_tables(), _row(l1_ret_norm_g), l1_w_pool.astype(F32), _row(l1_pool_scale),
           l1_w_out, _row(l1_ln_g), _row(l1_ln_b))
    in_hbm = (w0_t, l0_w_out, l1_w_in, l1_w_out)
    pos_blocks = positions.reshape(B, S // tb, tb)
    first = pos_blocks[:, :, 0]
    runs = jnp.all(pos_blocks == first[:, :, None] + jnp.arange(tb, dtype=positions.dtype), axis=-1)
    block_info = jnp.stack([first.astype(jnp.int32), runs.astype(jnp.int32)])
    tok_spec = pl.BlockSpec((1, tb, D), lambda b, i: (b, i, 0))
    params = even + odd
    return pl.pallas_call(
        _fused_kernel,
        out_shape=jax.ShapeDtypeStruct((B, S, D), F32),
        grid=(B, S // tb),
        in_specs=[pl.BlockSpec(memory_space=pltpu.SMEM), tok_spec, _hbm_spec()]
                 + [_hbm_spec() if any(p is h for h in in_hbm) else _full_spec(p) for p in params],
        out_specs=tok_spec,
        scratch_shapes=[
            pltpu.VMEM((QK, DV), F32),
            pltpu.VMEM((QK, D), BF16),
            pltpu.VMEM((SGU_CHUNK, BRANCH), F32),
            pltpu.VMEM(w0_t.shape, BF16),
            pltpu.VMEM(l0_w_out.shape, BF16),
            _stage(w0_t),
            _stage(l0_w_out),
            pltpu.SemaphoreType.DMA((STAGE_SLOTS,)),
            pltpu.VMEM((QK, DV), F32),
            pltpu.VMEM((POOL_PAD + tb, BRANCH), F32),
            pltpu.VMEM((2, 2, tb, LANES), F32),
            pltpu.VMEM((2, LANES), F32),
            pltpu.VMEM((1, tb), jnp.int32),
            pltpu.VMEM((D, 2 * QK), BF16),
            pltpu.VMEM((len(POOL_WINDOWS) // 2, 2 * LANES, 2 * LANES), BF16),
            pltpu.VMEM(l1_w_in.shape, BF16),
            pltpu.VMEM(l1_w_out.shape, BF16),
            _stage(l1_w_in),
            pltpu.VMEM((1, tb, D), F32),
        ],
        compiler_params=_compiler_params(),
        name="hybrid_trunk_two_layers",
    )(block_info, x, positions.reshape(B, 1, S), *params)
```

```python
import jax
import jax.numpy as jnp
import numpy as np
from jax import lax
from jax.experimental import pallas as pl
from jax.experimental.pallas import tpu as pltpu

F32 = jnp.float32
BF16 = jnp.bfloat16

D_MODEL = 1024
BRANCH = 512
HEADS = 4
DK = 64
DV = 128
QK = HEADS * DK
GLA_RANK = 16
GLA_TAU = 16.0
GLA_CHUNK = 64
SGU_CHUNK = 128
RET_CHUNK = 128
ROPE_BASE = 10000.0
POOL_WINDOWS = (2, 4, 8, 16)
POOL_PAD = 16
DN_ALPHA = 4.0 ** 0.25
LN_EPS = 1e-5
LANES = 128
SUBLANES = 8
MXU_TILE = 256
VMEM_LIMIT_BYTES = 62 * 1024 * 1024
TOKEN_BLOCK = 1024
NORM_ROW_GROUPS = 4
STAGE_BYTES = 512 * 1024
STAGE_SLOTS = 3
EVEN_Q, EVEN_K, EVEN_V, EVEN_GA, EVEN_LR = 0, QK, 2 * QK, 2 * QK + BRANCH, 2 * QK + 2 * BRANCH
EVEN_U, EVEN_SV, EVEN_GB = EVEN_LR + GLA_RANK, EVEN_LR + GLA_RANK + BRANCH, EVEN_LR + GLA_RANK + 2 * BRANCH


def _dot(a, b):
    return jnp.dot(a, b, preferred_element_type=F32)


def _dot_nt(a, b):
    return lax.dot_general(a, b, (((1,), (1,)), ((), ())), preferred_element_type=F32)


def _silu(x):
    hx = 0.5 * x
    return hx + hx * jnp.tanh(hx)


def _gelu_tanh(x):
    c = np.sqrt(2.0 / np.pi)
    hx = 0.5 * x
    return hx + hx * jnp.tanh(x * (np.float32(c) + np.float32(c * 0.044715) * (x * x)))


def _log_sigmoid(z):
    return jnp.minimum(z, 0.0) - jnp.log(1.0 + jnp.exp(-jnp.abs(z)))


def _layer_norm_rows(r, g, b, eps=LN_EPS):
    mu = jnp.mean(r, axis=-1, keepdims=True)
    c = r - mu
    var = jnp.mean(c * c, axis=-1, keepdims=True)
    return c * lax.rsqrt(var + eps) * g + b


def _deepnorm(x, y_scaled, g, b):
    return _layer_norm_rows(x + y_scaled, g, b, LN_EPS / DN_ALPHA ** 2)


def _project_and_norm(x_ref, o_first, o_second, wout_ref, lng_ref, lnb_ref):
    rows_per_group = x_ref.shape[1] // NORM_ROW_GROUPS
    outs = []
    for n in range(NORM_ROW_GROUPS):
        rows = slice(n * rows_per_group, (n + 1) * rows_per_group)
        y = (_dot(o_second[rows].astype(BF16), wout_ref[BRANCH:2 * BRANCH, :])
             + _dot(o_first[rows].astype(BF16), wout_ref[0:BRANCH, :]))
        outs.append(_deepnorm(x_ref[0, rows, :], y, lng_ref[...], lnb_ref[...]))
    return jnp.concatenate(outs, axis=0)


def _head_mask(h, natural):
    lane = lax.broadcasted_iota(jnp.int32, (1, QK), 1)
    head = lane // DK if natural else (lane % LANES) // (DK // 2)
    return (head == h).astype(F32)


def _load_as_bf16(src_hbm, dst_ref, stage_ref, sem, scale=None):
    rows = src_hbm.shape[0]
    slots, per_copy = stage_ref.shape[0], stage_ref.shape[1]
    starts = list(range(0, rows, per_copy))

    def copy(k):
        n_rows = min(per_copy, rows - starts[k])
        return pltpu.make_async_copy(src_hbm.at[pl.ds(starts[k], n_rows), :],
                                     stage_ref.at[k % slots, pl.ds(0, n_rows), :], sem.at[k % slots])

    for k in range(min(slots - 1, len(starts))):
        copy(k).start()
    for k, start in enumerate(starts):
        if k + slots - 1 < len(starts):
            copy(k + slots - 1).start()
        copy(k).wait()
        n_rows = min(per_copy, rows - start)
        block = stage_ref[k % slots, 0:n_rows, :]
        if scale is not None:
            block = block * scale
        dst_ref[start:start + n_rows, :] = block.astype(BF16)


def _when_for(part):
    return pl.when if part == "branches" else (lambda cond: (lambda block: None))


def _fused_kernel(run_ref, x_ref, pos_hbm, *refs):
    even_in, odd_in, o_ref = refs[0:11], refs[11:26], refs[26]
    even_s, (odd_st, pad_s, tab_s, rot_s, pos_s, wqk_s, wp_s, w1_s, wout1_s, wstage1_s), mid_s = (
        refs[27:35], refs[35:45], refs[45])
    ostage_s, sem = even_s[6], even_s[7]
    odd_s = (odd_st, pad_s, tab_s, rot_s, pos_s, wqk_s, wp_s, w1_s, wout1_s, wstage1_s, ostage_s, sem)
    for part in ("branches", "main"):
        if part == "branches":
            _even_kernel(x_ref, *even_in, mid_s, *even_s, part=part)
            _odd_kernel(run_ref, mid_s, pos_hbm, *odd_in, o_ref, *odd_s, part=part)
        else:
            _even_kernel(x_ref, *even_in, mid_s, *even_s, part=part)
            _odd_kernel(run_ref, mid_s, pos_hbm, *odd_in, o_ref, *odd_s, part=part)


def _even_kernel(x_ref, wt_hbm, wa2_ref, ba_ref, gng_ref, slg_ref, slb_ref, ws_ref, bsg_ref, wout_hbm,
                 lng_ref, lnb_ref, o_ref, st_s, wzt_s, bs_s, wt_s, wout_s, wstage_s, ostage_s, sem, *, part):
    when = _when_for(part)

    @when((pl.program_id(0) == 0) & (pl.program_id(1) == 0))
    def _():
        _load_as_bf16(wt_hbm, wt_s, wstage_s, sem)
        _load_as_bf16(wout_hbm, wout_s, ostage_s, sem, 1.0 / DN_ALPHA)
        wa2t = jnp.concatenate([wa2_ref[...], jnp.zeros((LANES - GLA_RANK, QK), F32)], axis=0).T.astype(BF16)
        wzt_s[...] = _dot(wa2t, wt_s[EVEN_LR:EVEN_LR + LANES, :]).astype(BF16)
        for g in range(BRANCH // LANES):
            bs_s[:, g * LANES:(g + 1) * LANES] = jnp.broadcast_to(bsg_ref[g:g + 1, :], (LANES, SGU_CHUNK)).T

    @when(pl.program_id(1) == 0)
    def _():
        st_s[...] = jnp.zeros_like(st_s)

    if part == "branches":
        return
    refs = (wt_s, wzt_s, ba_ref, gng_ref, slg_ref, slb_ref, ws_ref, bs_s, wout_s, lng_ref, lnb_ref)
    o_ref[0], st_s[...] = _even_block(x_ref, st_s[...], *refs)


def _even_block(x_ref, state, wt_ref, wzt_ref, ba_ref, gng_ref, slg_ref, slb_ref, ws_ref, bs_ref, wout_ref,
                lng_ref, lnb_ref):
    tb = x_ref.shape[1]
    xb = x_ref[0].astype(BF16)
    c_len = GLA_CHUNK
    t_len = SGU_CHUNK
    chunks = [slice(c * c_len, (c + 1) * c_len) for c in range(tb // c_len)]

    z = _dot_nt(xb, wzt_ref[...]) + ba_ref[...]
    q = _dot_nt(xb, wt_ref[EVEN_Q:EVEN_Q + QK, :])
    k = _dot_nt(xb, wt_ref[EVEN_K:EVEN_K + QK, :])
    la = _log_sigmoid(z) * (1.0 / GLA_TAU)
    v = _dot_nt(xb, wt_ref[EVEN_V:EVEN_V + BRANCH, :]).astype(BF16)

    row = lax.broadcasted_iota(jnp.int32, (LANES, 2 * LANES), 0)
    col = lax.broadcasted_iota(jnp.int32, (LANES, 2 * LANES), 1) % LANES
    tril2 = ((row >= col) & (row // c_len == col // c_len)).astype(BF16)
    la_hi = la.astype(BF16)
    la_lo = (la - la_hi.astype(F32)).astype(BF16)
    b = jnp.concatenate(
        [_dot(tril2, jnp.concatenate([la_hi[n * LANES:(n + 1) * LANES], la_lo[n * LANES:(n + 1) * LANES]],
                                     axis=0))
         for n in range(tb // LANES)], axis=0)

    u = _dot_nt(xb, wt_ref[EVEN_U:EVEN_U + BRANCH, :])
    q_dec = q * jnp.exp(b) * (DK ** -0.5)
    k_dec = k * jnp.exp(-b)
    masks = [_head_mask(h, True).astype(BF16) for h in range(HEADS)]
    q_dec_b = q_dec.astype(BF16)
    qms = [jnp.concatenate([q_dec_b[sl] * masks[h] for h in range(HEADS)], axis=0)
           for sl in chunks]
    sv = _dot_nt(xb, wt_ref[EVEN_SV:EVEN_SV + BRANCH, :])

    u = _gelu_tanh(u)
    kts, upds, dcols = [], [], []
    for c, sl in enumerate(chunks):
        kt = k_dec[sl].T.astype(BF16)
        kts.append(kt)
        upds.append(jnp.concatenate([_dot(kt[h * DK:(h + 1) * DK], v[sl, h * DV:(h + 1) * DV])
                                     for h in range(HEADS)], axis=0))
        b_last = b[(c + 1) * c_len - 1:(c + 1) * c_len, :]
        dcols.append(jnp.exp(jnp.broadcast_to(b_last, (LANES, QK)).T))
    gate_a = _silu(_dot_nt(xb, wt_ref[EVEN_GA:EVEN_GA + BRANCH, :])).astype(BF16)
    sv = _gelu_tanh(sv)
    slg = slg_ref[...]
    slb = slb_ref[...]
    svn = jnp.concatenate(
        [_layer_norm_rows(sv[:, g * LANES:(g + 1) * LANES], slg[:, g * LANES:(g + 1) * LANES],
                          slb[:, g * LANES:(g + 1) * LANES]) for g in range(BRANCH // LANES)],
        axis=1).astype(BF16)
    gate_b = _silu(_dot_nt(xb, wt_ref[EVEN_GB:EVEN_GB + BRANCH, :])).astype(BF16)

    states = []
    for upd, dcol in zip(upds, dcols):
        states.append(state.astype(BF16))
        state = dcol * (state + upd)

    trow = lax.broadcasted_iota(jnp.int32, (t_len, t_len), 0)
    tcol = lax.broadcasted_iota(jnp.int32, (t_len, t_len), 1)
    s_groups = []
    for g in range(BRANCH // LANES):
        w = jnp.where(trow >= tcol, ws_ref[g], 0.0).astype(BF16)
        cols = slice(g * LANES, (g + 1) * LANES)
        mixed = []
        for n in range(0, tb // t_len, 2):
            pair = _dot(w, jnp.concatenate([svn[n * t_len:(n + 1) * t_len, cols],
                                            svn[(n + 1) * t_len:(n + 2) * t_len, cols]], axis=1))
            mixed += [pair[:, :LANES], pair[:, LANES:]]
        s_groups.append(jnp.concatenate(mixed, axis=0))
    s = jnp.concatenate(s_groups, axis=1) + jnp.concatenate([bs_ref[...]] * (tb // t_len), axis=0)

    gng = gng_ref[...]
    crow = lax.broadcasted_iota(jnp.int32, (HEADS * c_len, c_len), 0) % c_len
    ccol = lax.broadcasted_iota(jnp.int32, (HEADS * c_len, c_len), 1)
    causal = crow >= ccol
    fill = jnp.zeros((QK, MXU_TILE - DV - c_len), BF16)
    fused = [_dot(qm, jnp.concatenate([s_b, kt, fill], axis=1)) for qm, s_b, kt in zip(qms, states, kts)]
    o_b = u * s * gate_b
    o_chunks = []
    for sl, f in zip(chunks, fused):
        o_inter = f[:, 0:DV]
        sc = jnp.where(causal, f[:, DV:DV + c_len], 0.0).astype(BF16)
        o_heads = []
        for h in range(HEADS):
            rows = slice(h * c_len, (h + 1) * c_len)
            o = _dot(sc[rows], v[sl, h * DV:(h + 1) * DV]) + o_inter[rows]
            o = o * lax.rsqrt(jnp.mean(o * o, axis=-1, keepdims=True) + LN_EPS)
            o_heads.append(o * gng[:, h * DV:(h + 1) * DV])
        o_chunks.append(jnp.concatenate(o_heads, axis=1))
    o_a = jnp.concatenate(o_chunks, axis=0) * gate_a
    return _project_and_norm(x_ref, o_a, o_b, wout_ref, lng_ref, lnb_ref), state


def _odd_kernel(run_ref, x_ref, pos_hbm, w_hbm, perm_ref, inv_ref, ctab_ref, stab_ref, dmat_ref, wint_ref,
                wst_ref, dec_ref, rng_ref, wpool_ref, psc_ref, wout_hbm, lng_ref, lnb_ref, o_ref, st_s, pad_s,
                tab_s, rot_s, pos_s, wqk_s, wp_ref, w_ref, wout_ref, wstage_s, ostage_s, sem, *, part):
    when = _when_for(part)
    tb = x_ref.shape[1]
    seq = pl.program_id(0)
    i = pl.program_id(1)

    @when((seq == 0) & (i == 0))
    def _():
        _load_as_bf16(w_hbm, w_ref, wstage_s, sem)
        _load_as_bf16(wout_hbm, wout_ref, ostage_s, sem, 1.0 / DN_ALPHA)
        wqk_s[...] = _dot(w_ref[:, 0:2 * QK], perm_ref[...]).astype(BF16)
        tab_s[0, 0] = ctab_ref[...]
        tab_s[0, 1] = stab_ref[...]
        tab_s[1] = jnp.zeros(tab_s.shape[1:], F32)
        zero = jnp.zeros((LANES, LANES), F32)
        for n in range(len(POOL_WINDOWS) // 2):
            wp_ref[n] = jnp.concatenate(
                [jnp.concatenate([wpool_ref[2 * n], zero], axis=1),
                 jnp.concatenate([zero, wpool_ref[2 * n + 1]], axis=1)], axis=0).astype(BF16)

    @when(i == 0)
    def _():
        st_s[...] = jnp.zeros_like(st_s)
        pad_s[0:POOL_PAD, :] = jnp.zeros((POOL_PAD, BRANCH), F32)

    c_len = RET_CHUNK
    half = DK // 2
    chunks = [slice(c * c_len, (c + 1) * c_len) for c in range(tb // c_len)]
    c_q, c_k, c_v, c_gc, c_p, c_gd = [(o, o + n) for o, n in zip(
        (0, QK, 2 * QK, 2 * QK + BRANCH, 2 * QK + 2 * BRANCH, 2 * QK + 3 * BRANCH),
        (QK, QK, BRANCH, BRANCH, BRANCH, BRANCH))]

    consecutive = run_ref[1, seq, i] == 1

    @when(consecutive)
    def _():
        a0 = run_ref[0, seq, i].astype(F32) * inv_ref[...]
        rot_s[0:1, :] = jnp.cos(a0)
        rot_s[1:2, :] = jnp.sin(a0)

    @when(jnp.logical_not(consecutive))
    def _():
        copy = pltpu.make_async_copy(pos_hbm.at[seq, :, pl.ds(pl.multiple_of(i * tb, tb), tb)], pos_s, sem.at[0])
        copy.start()
        copy.wait()
        pos_rows = jnp.broadcast_to(pos_s[...].astype(F32), (LANES, tb)).T
        ang = pos_rows * inv_ref[...]
        tab_s[1, 0] = jnp.cos(ang)
        tab_s[1, 1] = jnp.sin(ang)
        rot_s[0:1, :] = jnp.ones((1, LANES), F32)
        rot_s[1:2, :] = jnp.zeros((1, LANES), F32)

    if part == "branches":
        return
    xb = x_ref[0].astype(BF16)

    q = _dot(xb, wqk_s[:, c_q[0]:c_q[1]]) * (DK ** -0.5)
    k = _dot(xb, wqk_s[:, c_k[0]:c_k[1]])
    slot = jnp.where(consecutive, 0, 1)
    c_tab, s_tab = tab_s[slot, 0], tab_s[slot, 1]
    c_rot, s_rot = rot_s[0:1, :], rot_s[1:2, :]
    cos = c_rot * c_tab - s_rot * s_tab
    sin = s_rot * c_tab + c_rot * s_tab
    v = _dot(xb, w_ref[:, c_v[0]:c_v[1]]).astype(BF16)
    p = _dot(xb, w_ref[:, c_p[0]:c_p[1]])
    q1, q2 = q[:, :LANES], q[:, LANES:]
    k1, k2 = k[:, :LANES], k[:, LANES:]
    qr = jnp.concatenate([q1 * cos - q2 * sin, q1 * sin + q2 * cos], axis=1)
    kr = jnp.concatenate([k1 * cos - k2 * sin, k1 * sin + k2 * cos], axis=1)
    masks = [_head_mask(h, False).astype(BF16) for h in range(HEADS)]
    qr_b = qr.astype(BF16)
    qms = [jnp.concatenate([qr_b[sl] * masks[h] for h in range(HEADS)], axis=0)
           for sl in chunks]
    gate_c = _silu(_dot(xb, w_ref[:, c_gc[0]:c_gc[1]])).astype(BF16)

    pad_s[POOL_PAD:POOL_PAD + tb, :] = p
    t_top = i * tb + lax.broadcasted_iota(jnp.int32, (POOL_PAD, LANES), 0)
    pooled = []
    for g, win in enumerate(POOL_WINDOWS):
        cols = slice(g * LANES, (g + 1) * LANES)
        acc = pad_s[:, cols]
        shift = 1
        while shift < win:
            acc = acc + pltpu.roll(acc, shift, axis=0)
            shift *= 2
        cnt_top = jnp.minimum(t_top + 1, win).astype(F32)
        mean = jnp.concatenate([acc[POOL_PAD:2 * POOL_PAD] / cnt_top, acc[2 * POOL_PAD:] * (1.0 / win)],
                               axis=0)
        pooled.append((mean - p[:, cols]).astype(BF16))
    tail = pad_s[tb:tb + POOL_PAD, :]
    pad_s[0:POOL_PAD, :] = tail
    kts, upds = [], []
    for sl in chunks:
        kt = (kr[sl] * wst_ref[...]).T.astype(BF16)
        kts.append(kt)
        upd = [_dot(jnp.concatenate([kt[h * half:(h + 1) * half],
                                     kt[LANES + h * half:LANES + (h + 1) * half]], axis=0),
                    v[sl, h * DV:(h + 1) * DV]) for h in range(HEADS)]
        upds.append(jnp.concatenate([u[:half] for u in upd] + [u[half:] for u in upd], axis=0))
    gate_d = _silu(_dot(xb, w_ref[:, c_gd[0]:c_gd[1]])).astype(BF16)

    state = st_s[...]
    states = []
    for upd in upds:
        states.append(state.astype(BF16))
        state = dec_ref[...] * state + upd
    st_s[...] = state
    yd = jnp.concatenate([_dot(jnp.concatenate([pooled[2 * n], pooled[2 * n + 1]], axis=1), wp_ref[n])
                          for n in range(len(POOL_WINDOWS) // 2)], axis=1)
    fused = [_dot(qm, jnp.concatenate([s_b, kt], axis=1)) for qm, s_b, kt in zip(qms, states, kts)]
    o_d = yd * psc_ref[...] * gate_d
    rng = rng_ref[...]
    o_chunks = []
    for sl, f in zip(chunks, fused):
        o_inter = f[:, 0:DV] * wint_ref[...]
        sc = (f[:, DV:DV + c_len] * dmat_ref[...]).astype(BF16)
        o_heads = []
        for h in range(HEADS):
            rows = slice(h * c_len, (h + 1) * c_len)
            o = _dot(sc[rows], v[sl, h * DV:(h + 1) * DV]) + o_inter[rows]
            mu = jnp.mean(o, axis=-1, keepdims=True)
            oc = o - mu
            var = jnp.mean(oc * oc, axis=-1, keepdims=True)
            o_heads.append(oc * lax.rsqrt(var + LN_EPS) * rng[:, h * DV:(h + 1) * DV])
        o_chunks.append(jnp.concatenate(o_heads, axis=1))
    o_c = jnp.concatenate(o_chunks, axis=0) * gate_c
    o_ref[0] = _project_and_norm(x_ref, o_c, o_d, wout_ref, lng_ref, lnb_ref)


def _full_spec(a):
    nd = a.ndim
    return pl.BlockSpec(a.shape, lambda *_, _nd=nd: (0,) * _nd)


def _compiler_params():
    return pltpu.CompilerParams(dimension_semantics=("arbitrary", "arbitrary"),
                                vmem_limit_bytes=VMEM_LIMIT_BYTES)


def _row(a):
    return a.reshape(1, -1).astype(F32)


def _hbm_spec():
    return pl.BlockSpec(memory_space=pl.ANY)


def _stage(w):
    rows = min(w.shape[0], STAGE_BYTES // (4 * w.shape[1]) // SUBLANES * SUBLANES)
    return pltpu.VMEM((STAGE_SLOTS, rows, w.shape[1]), F32)


def _even_layer(x, w_in, w_a2, b_a, gla_norm_g, sgu_ln_g, sgu_ln_b, w_s, b_s, w_out, ln_g, ln_b):
    B, S, D = x.shape
    tb = TOKEN_BLOCK
    w_in_t = w_in.T
    params = (w_in_t, w_a2.astype(F32), _row(b_a), _row(gla_norm_g), _row(sgu_ln_g), _row(sgu_ln_b),
              w_s.astype(F32), b_s.astype(F32), w_out, _row(ln_g), _row(ln_b))
    in_hbm = (w_in_t, w_out)
    tok_spec = pl.BlockSpec((1, tb, D), lambda b, i: (b, i, 0))
    return pl.pallas_call(
        _even_kernel,
        out_shape=jax.ShapeDtypeStruct((B, S, D), F32),
        grid=(B, S // tb),
        in_specs=[tok_spec] + [_hbm_spec() if any(p is h for h in in_hbm) else _full_spec(p) for p in params],
        out_specs=tok_spec,
        scratch_shapes=[
            pltpu.VMEM((QK, DV), F32),
            pltpu.VMEM((QK, D), BF16),
            pltpu.VMEM((SGU_CHUNK, BRANCH), F32),
            pltpu.VMEM(w_in_t.shape, BF16),
            pltpu.VMEM(w_out.shape, BF16),
            _stage(w_in_t),
            _stage(w_out),
            pltpu.SemaphoreType.DMA((STAGE_SLOTS,)),
        ],
        compiler_params=_compiler_params(),
        name="even_layer_gla_sgu",
    )(x, *params)


def _retention_tables():
    c = RET_CHUNK
    log_gamma = np.log(1.0 - 2.0 ** (-5.0 - np.arange(HEADS, dtype=np.float64)))
    idx = np.arange(c, dtype=np.float64)
    rel = idx[:, None] - idx[None, :]
    dmat = np.where(rel >= 0, np.exp((idx[:, None] - (c - 1.0))[None] * log_gamma[:, None, None]), 0.0)
    w_inter = np.exp((idx + 1.0)[None] * log_gamma[:, None])
    w_state = np.exp((c - 1.0 - idx)[None] * log_gamma[:, None])
    lane_head = (np.arange(QK) % LANES) // (DK // 2)
    dmat = dmat.reshape(HEADS * c, c)
    wint = np.broadcast_to(w_inter[:, :, None], (HEADS, c, DV)).reshape(HEADS * c, DV)
    wst = w_state[lane_head, :].T
    dec = np.broadcast_to(np.exp(c * log_gamma)[lane_head][:, None], (QK, DV))
    inv = ROPE_BASE ** (-np.arange(DK // 2, dtype=np.float64) / (DK // 2))
    inv = np.tile(inv, HEADS)[None, :].astype(np.float32)
    off = np.arange(TOKEN_BLOCK, dtype=np.float64)[:, None] * inv.astype(np.float64)
    f = lambda a: jnp.asarray(np.ascontiguousarray(a), F32)
    return f(inv), f(np.cos(off)), f(np.sin(off)), f(dmat), f(wint), f(wst), f(dec)


def _odd_layer(x, positions, w_in, ret_norm_g, w_pool, pool_scale, w_out, ln_g, ln_b):
    B, S, D = x.shape
    tb = TOKEN_BLOCK
    j = np.arange(2 * QK)
    src = (j // QK) * QK + ((j % LANES) // (DK // 2)) * DK + ((j % QK) // LANES) * (DK // 2) + j % (DK // 2)
    perm = jnp.asarray(src[None, :] == np.arange(2 * QK)[:, None], BF16)
    params = (w_in, perm, *_retention_tables(), _row(ret_norm_g), w_pool.astype(F32), _row(pool_scale), w_out,
              _row(ln_g), _row(ln_b))
    in_hbm = (w_in, w_out)
    pos_blocks = positions.reshape(B, S // tb, tb)
    first = pos_blocks[:, :, 0]
    runs = jnp.all(pos_blocks == first[:, :, None] + jnp.arange(tb, dtype=positions.dtype), axis=-1)
    block_info = jnp.stack([first.astype(jnp.int32), runs.astype(jnp.int32)])
    tok_spec = pl.BlockSpec((1, tb, D), lambda b, i: (b, i, 0))
    return pl.pallas_call(
        _odd_kernel,
        out_shape=jax.ShapeDtypeStruct((B, S, D), F32),
        grid=(B, S // tb),
        in_specs=[pl.BlockSpec(memory_space=pltpu.SMEM), tok_spec, _hbm_spec()]
                 + [_hbm_spec() if any(p is h for h in in_hbm) else _full_spec(p) for p in params],
        out_specs=tok_spec,
        scratch_shapes=[
            pltpu.VMEM((QK, DV), F32),
            pltpu.VMEM((POOL_PAD + tb, BRANCH), F32),
            pltpu.VMEM((2, 2, tb, LANES), F32),
            pltpu.VMEM((2, LANES), F32),
            pltpu.VMEM((1, tb), jnp.int32),
            pltpu.VMEM((D, 2 * QK), BF16),
            pltpu.VMEM((len(POOL_WINDOWS) // 2, 2 * LANES, 2 * LANES), BF16),
            pltpu.VMEM(w_in.shape, BF16),
            pltpu.VMEM(w_out.shape, BF16),
            _stage(w_in),
            _stage(w_out),
            pltpu.SemaphoreType.DMA((STAGE_SLOTS,)),
        ],
        compiler_params=_compiler_params(),
        name="odd_layer_retention_pool",
    )(block_info, x, positions.reshape(B, 1, S), *params)


def kernel(x, positions, l0_w_in, l0_w_a2, l0_b_a, l0_gla_norm_g, l0_sgu_ln_g, l0_sgu_ln_b, l0_w_s, l0_b_s, l0_w_out, l0_ln_g, l0_ln_b, l1_w_in, l1_ret_norm_g, l1_w_pool, l1_pool_scale, l1_w_out, l1_ln_g, l1_ln_b):
    B, S, D = x.shape
    tb = TOKEN_BLOCK
    w0_t = l0_w_in.T
    even = (w0_t, l0_w_a2.astype(F32), _row(l0_b_a), _row(l0_gla_norm_g), _row(l0_sgu_ln_g), _row(l0_sgu_ln_b),
            l0_w_s.astype(F32), l0_b_s.astype(F32), l0_w_out, _row(l0_ln_g), _row(l0_ln_b))
    j = np.arange(2 * QK)
    src = (j // QK) * QK + ((j % LANES) // (DK // 2)) * DK + ((j % QK) // LANES) * (DK // 2) + j % (DK // 2)
    perm = jnp.asarray(src[None, :] == np.arange(2 * QK)[:, None], BF16)
    odd = (l1_w_in, perm, *_retention_tables(), _row(l1_ret_norm_g), l1_w_pool.astype(F32), _row(l1_pool_scale),
           l1_w_out, _row(l1_ln_g), _row(l1_ln_b))
    in_hbm = (w0_t, l0_w_out, l1_w_in, l1_w_out)
    pos_blocks = positions.reshape(B, S // tb, tb)
    first = pos_blocks[:, :, 0]
    runs = jnp.all(pos_blocks == first[:, :, None] + jnp.arange(tb, dtype=positions.dtype), axis=-1)
    block_info = jnp.stack([first.astype(jnp.int32), runs.astype(jnp.int32)])
    tok_spec = pl.BlockSpec((1, tb, D), lambda b, i: (b, i, 0))
    params = even + odd
    return pl.pallas_call(
        _fused_kernel,
        out_shape=jax.ShapeDtypeStruct((B, S, D), F32),
        grid=(B, S // tb),
        in_specs=[pl.BlockSpec(memory_space=pltpu.SMEM), tok_spec, _hbm_spec()]
                 + [_hbm_spec() if any(p is h for h in in_hbm) else _full_spec(p) for p in params],
        out_specs=tok_spec,
        scratch_shapes=[
            pltpu.VMEM((QK, DV), F32),
            pltpu.VMEM((QK, D), BF16),
            pltpu.VMEM((SGU_CHUNK, BRANCH), F32),
            pltpu.VMEM(w0_t.shape, BF16),
            pltpu.VMEM(l0_w_out.shape, BF16),
            _stage(w0_t),
            _stage(l0_w_out),
            pltpu.SemaphoreType.DMA((STAGE_SLOTS,)),
            pltpu.VMEM((QK, DV), F32),
            pltpu.VMEM((POOL_PAD + tb, BRANCH), F32),
            pltpu.VMEM((2, 2, tb, LANES), F32),
            pltpu.VMEM((2, LANES), F32),
            pltpu.VMEM((1, tb), jnp.int32),
            pltpu.VMEM((D, 2 * QK), BF16),
            pltpu.VMEM((len(POOL_WINDOWS) // 2, 2 * LANES, 2 * LANES), BF16),
            pltpu.VMEM(l1_w_in.shape, BF16),
            pltpu.VMEM(l1_w_out.shape, BF16),
            _stage(l1_w_in),
            pltpu.VMEM((1, tb, D), F32),
        ],
        compiler_params=_compiler_params(),
        name="hybrid_trunk_two_layers",
    )(block_info, x, positions.reshape(B, 1, S), *params)
```

```python
import jax
import jax.numpy as jnp
import numpy as np
from jax import lax
from jax.experimental import pallas as pl
from jax.experimental.pallas import tpu as pltpu

F32 = jnp.float32
BF16 = jnp.bfloat16

D_MODEL = 1024
BRANCH = 512
HEADS = 4
DK = 64
DV = 128
QK = HEADS * DK
GLA_RANK = 16
GLA_TAU = 16.0
GLA_CHUNK = 64
SGU_CHUNK = 128
RET_CHUNK = 128
ROPE_BASE = 10000.0
POOL_WINDOWS = (2, 4, 8, 16)
POOL_PAD = 16
DN_ALPHA = 4.0 ** 0.25
LN_EPS = 1e-5
LANES = 128
SUBLANES = 8
MXU_TILE = 256
VMEM_LIMIT_BYTES = 56 * 1024 * 1024
TOKEN_BLOCK = 1024
NORM_ROW_GROUPS = 4
STAGE_BYTES = 1024 * 1024
STAGE_SLOTS = 4
EVEN_Q, EVEN_K, EVEN_V, EVEN_GA, EVEN_LR = 0, QK, 2 * QK, 2 * QK + BRANCH, 2 * QK + 2 * BRANCH
EVEN_U, EVEN_SV, EVEN_GB = EVEN_LR + GLA_RANK, EVEN_LR + GLA_RANK + BRANCH, EVEN_LR + GLA_RANK + 2 * BRANCH


def _dot(a, b):
    return jnp.dot(a, b, preferred_element_type=F32)


def _dot_nt(a, b):
    return lax.dot_general(a, b, (((1,), (1,)), ((), ())), preferred_element_type=F32)


def _silu(x):
    hx = 0.5 * x
    return hx + hx * jnp.tanh(hx)


def _gelu_tanh(x):
    c = np.sqrt(2.0 / np.pi)
    hx = 0.5 * x
    return hx + hx * jnp.tanh(x * (np.float32(c) + np.float32(c * 0.044715) * (x * x)))


def _log_sigmoid(z):
    return jnp.minimum(z, 0.0) - jnp.log(1.0 + jnp.exp(-jnp.abs(z)))


def _layer_norm_rows(r, g, b, eps=LN_EPS):
    mu = jnp.mean(r, axis=-1, keepdims=True)
    c = r - mu
    var = jnp.mean(c * c, axis=-1, keepdims=True)
    return c * lax.rsqrt(var + eps) * g + b


def _deepnorm(x, y_scaled, g, b):
    return _layer_norm_rows(x + y_scaled, g, b, LN_EPS / DN_ALPHA ** 2)


def _project_and_norm(x_ref, o_first, o_second, wout_ref, lng_ref, lnb_ref):
    rows_per_group = x_ref.shape[1] // NORM_ROW_GROUPS
    outs = []
    for n in range(NORM_ROW_GROUPS):
        rows = slice(n * rows_per_group, (n + 1) * rows_per_group)
        y = (_dot(o_second[rows].astype(BF16), wout_ref[BRANCH:2 * BRANCH, :])
             + _dot(o_first[rows].astype(BF16), wout_ref[0:BRANCH, :]))
        outs.append(_deepnorm(x_ref[0, rows, :], y, lng_ref[...], lnb_ref[...]))
    return jnp.concatenate(outs, axis=0)


def _head_mask(h, natural):
    lane = lax.broadcasted_iota(jnp.int32, (1, QK), 1)
    head = lane // DK if natural else (lane % LANES) // (DK // 2)
    return (head == h).astype(F32)


def _load_as_bf16(src_hbm, dst_ref, stage_ref, sem, scale=None):
    rows = src_hbm.shape[0]
    slots, per_copy = stage_ref.shape[0], stage_ref.shape[1]
    starts = list(range(0, rows, per_copy))

    def copy(k):
        n_rows = min(per_copy, rows - starts[k])
        return pltpu.make_async_copy(src_hbm.at[pl.ds(starts[k], n_rows), :],
                                     stage_ref.at[k % slots, pl.ds(0, n_rows), :], sem.at[k % slots])

    for k in range(min(slots - 1, len(starts))):
        copy(k).start(priority=k % 2)
    for k, start in enumerate(starts):
        if k + slots - 1 < len(starts):
            copy(k + slots - 1).start(priority=(k + slots - 1) % 2)
        copy(k).wait()
        n_rows = min(per_copy, rows - start)
        block = stage_ref[k % slots, 0:n_rows, :]
        if scale is not None:
            block = block * scale
        dst_ref[start:start + n_rows, :] = block.astype(BF16)


def _even_kernel(x_ref, wt_hbm, wa2_ref, ba_ref, gng_ref, slg_ref, slb_ref, ws_ref, bsg_ref, wout_hbm,
                 lng_ref, lnb_ref, o_ref, st_s, wzt_s, bs_s, wt_s, wout_s, wstage_s, ostage_s, sem):
    @pl.when((pl.program_id(0) == 0) & (pl.program_id(1) == 0))
    def _():
        _load_as_bf16(wt_hbm, wt_s, wstage_s, sem)
        _load_as_bf16(wout_hbm, wout_s, ostage_s, sem, 1.0 / DN_ALPHA)
        wa2t = jnp.concatenate([wa2_ref[...], jnp.zeros((LANES - GLA_RANK, QK), F32)], axis=0).T.astype(BF16)
        wzt_s[...] = _dot(wa2t, wt_s[EVEN_LR:EVEN_LR + LANES, :]).astype(BF16)
        for g in range(BRANCH // LANES):
            bs_s[:, g * LANES:(g + 1) * LANES] = jnp.broadcast_to(bsg_ref[g:g + 1, :], (LANES, SGU_CHUNK)).T

    @pl.when(pl.program_id(1) == 0)
    def _():
        st_s[...] = jnp.zeros_like(st_s)

    refs = (wt_s, wzt_s, ba_ref, gng_ref, slg_ref, slb_ref, ws_ref, bs_s, wout_s, lng_ref, lnb_ref)
    o_ref[0], st_s[...] = _even_block(x_ref, st_s[...], *refs)


def _even_block(x_ref, state, wt_ref, wzt_ref, ba_ref, gng_ref, slg_ref, slb_ref, ws_ref, bs_ref, wout_ref,
                lng_ref, lnb_ref):
    tb = x_ref.shape[1]
    xb = x_ref[0].astype(BF16)
    c_len = GLA_CHUNK
    t_len = SGU_CHUNK
    chunks = [slice(c * c_len, (c + 1) * c_len) for c in range(tb // c_len)]

    z = _dot_nt(xb, wzt_ref[...]) + ba_ref[...]
    q = _dot_nt(xb, wt_ref[EVEN_Q:EVEN_Q + QK, :])
    k = _dot_nt(xb, wt_ref[EVEN_K:EVEN_K + QK, :])
    la = _log_sigmoid(z) * (1.0 / GLA_TAU)
    v = _dot_nt(xb, wt_ref[EVEN_V:EVEN_V + BRANCH, :]).astype(BF16)

    row = lax.broadcasted_iota(jnp.int32, (LANES, 2 * LANES), 0)
    col = lax.broadcasted_iota(jnp.int32, (LANES, 2 * LANES), 1) % LANES
    tril2 = ((row >= col) & (row // c_len == col // c_len)).astype(BF16)
    la_hi = la.astype(BF16)
    la_lo = (la - la_hi.astype(F32)).astype(BF16)
    b = jnp.concatenate(
        [_dot(tril2, jnp.concatenate([la_hi[n * LANES:(n + 1) * LANES], la_lo[n * LANES:(n + 1) * LANES]],
                                     axis=0))
         for n in range(tb // LANES)], axis=0)

    u = _dot_nt(xb, wt_ref[EVEN_U:EVEN_U + BRANCH, :])
    q_dec = q * jnp.exp(b) * (DK ** -0.5)
    k_dec = k * jnp.exp(-b)
    masks = [_head_mask(h, True).astype(BF16) for h in range(HEADS)]
    q_dec_b = q_dec.astype(BF16)
    qms = [jnp.concatenate([q_dec_b[sl] * masks[h] for h in range(HEADS)], axis=0)
           for sl in chunks]
    sv = _dot_nt(xb, wt_ref[EVEN_SV:EVEN_SV + BRANCH, :])

    u = _gelu_tanh(u)
    kts, upds, dcols = [], [], []
    for c, sl in enumerate(chunks):
        kt = k_dec[sl].T.astype(BF16)
        kts.append(kt)
        upds.append(jnp.concatenate([_dot(kt[h * DK:(h + 1) * DK], v[sl, h * DV:(h + 1) * DV])
                                     for h in range(HEADS)], axis=0))
        b_last = b[(c + 1) * c_len - 1:(c + 1) * c_len, :]
        dcols.append(jnp.exp(jnp.broadcast_to(b_last, (LANES, QK)).T))
    gate_a = _silu(_dot_nt(xb, wt_ref[EVEN_GA:EVEN_GA + BRANCH, :])).astype(BF16)
    sv = _gelu_tanh(sv)
    slg = slg_ref[...]
    slb = slb_ref[...]
    svn = jnp.concatenate(
        [_layer_norm_rows(sv[:, g * LANES:(g + 1) * LANES], slg[:, g * LANES:(g + 1) * LANES],
                          slb[:, g * LANES:(g + 1) * LANES]) for g in range(BRANCH // LANES)],
        axis=1).astype(BF16)
    gate_b = _silu(_dot_nt(xb, wt_ref[EVEN_GB:EVEN_GB + BRANCH, :])).astype(BF16)

    states = []
    for upd, dcol in zip(upds, dcols):
        states.append(state.astype(BF16))
        state = dcol * (state + upd)

    trow = lax.broadcasted_iota(jnp.int32, (t_len, t_len), 0)
    tcol = lax.broadcasted_iota(jnp.int32, (t_len, t_len), 1)
    s_groups = []
    for g in range(BRANCH // LANES):
        w = jnp.where(trow >= tcol, ws_ref[g], 0.0).astype(BF16)
        cols = slice(g * LANES, (g + 1) * LANES)
        mixed = []
        for n in range(0, tb // t_len, 2):
            pair = _dot(w, jnp.concatenate([svn[n * t_len:(n + 1) * t_len, cols],
                                            svn[(n + 1) * t_len:(n + 2) * t_len, cols]], axis=1))
            mixed += [pair[:, :LANES], pair[:, LANES:]]
        s_groups.append(jnp.concatenate(mixed, axis=0))
    s = jnp.concatenate(s_groups, axis=1) + jnp.concatenate([bs_ref[...]] * (tb // t_len), axis=0)

    gng = gng_ref[...]
    crow = lax.broadcasted_iota(jnp.int32, (HEADS * c_len, c_len), 0) % c_len
    ccol = lax.broadcasted_iota(jnp.int32, (HEADS * c_len, c_len), 1)
    causal = crow >= ccol
    fill = jnp.zeros((QK, MXU_TILE - DV - c_len), BF16)
    fused = [_dot(qm, jnp.concatenate([s_b, kt, fill], axis=1)) for qm, s_b, kt in zip(qms, states, kts)]
    o_b = u * s * gate_b
    o_chunks = []
    for sl, f in zip(chunks, fused):
        o_inter = f[:, 0:DV]
        sc = jnp.where(causal, f[:, DV:DV + c_len], 0.0).astype(BF16)
        o_heads = []
        for h in range(HEADS):
            rows = slice(h * c_len, (h + 1) * c_len)
            o = _dot(sc[rows], v[sl, h * DV:(h + 1) * DV]) + o_inter[rows]
            o = o * lax.rsqrt(jnp.mean(o * o, axis=-1, keepdims=True) + LN_EPS)
            o_heads.append(o * gng[:, h * DV:(h + 1) * DV])
        o_chunks.append(jnp.concatenate(o_heads, axis=1))
    o_a = jnp.concatenate(o_chunks, axis=0) * gate_a
    return _project_and_norm(x_ref, o_a, o_b, wout_ref, lng_ref, lnb_ref), state


def _odd_kernel(run_ref, x_ref, pos_hbm, w_hbm, perm_ref, inv_ref, ctab_ref, stab_ref, dmat_ref, wint_ref,
                wst_ref, dec_ref, rng_ref, wpool_ref, psc_ref, wout_hbm, lng_ref, lnb_ref, o_ref, st_s, pad_s,
                tab_s, rot_s, pos_s, wqk_s, wp_ref, w_ref, wout_ref, wstage_s, ostage_s, sem):
    tb = x_ref.shape[1]
    seq = pl.program_id(0)
    i = pl.program_id(1)

    @pl.when((seq == 0) & (i == 0))
    def _():
        _load_as_bf16(w_hbm, w_ref, wstage_s, sem)
        _load_as_bf16(wout_hbm, wout_ref, ostage_s, sem, 1.0 / DN_ALPHA)
        wqk_s[...] = _dot(w_ref[:, 0:2 * QK], perm_ref[...]).astype(BF16)
        tab_s[0, 0] = ctab_ref[...]
        tab_s[0, 1] = stab_ref[...]
        tab_s[1] = jnp.zeros(tab_s.shape[1:], F32)
        zero = jnp.zeros((LANES, LANES), F32)
        for n in range(len(POOL_WINDOWS) // 2):
            wp_ref[n] = jnp.concatenate(
                [jnp.concatenate([wpool_ref[2 * n], zero], axis=1),
                 jnp.concatenate([zero, wpool_ref[2 * n + 1]], axis=1)], axis=0).astype(BF16)

    @pl.when(i == 0)
    def _():
        st_s[...] = jnp.zeros_like(st_s)
        pad_s[0:POOL_PAD, :] = jnp.zeros((POOL_PAD, BRANCH), F32)

    c_len = RET_CHUNK
    half = DK // 2
    chunks = [slice(c * c_len, (c + 1) * c_len) for c in range(tb // c_len)]
    c_q, c_k, c_v, c_gc, c_p, c_gd = [(o, o + n) for o, n in zip(
        (0, QK, 2 * QK, 2 * QK + BRANCH, 2 * QK + 2 * BRANCH, 2 * QK + 3 * BRANCH),
        (QK, QK, BRANCH, BRANCH, BRANCH, BRANCH))]

    consecutive = run_ref[1, seq, i] == 1

    @pl.when(consecutive)
    def _():
        a0 = run_ref[0, seq, i].astype(F32) * inv_ref[...]
        rot_s[0:1, :] = jnp.cos(a0)
        rot_s[1:2, :] = jnp.sin(a0)

    @pl.when(jnp.logical_not(consecutive))
    def _():
        copy = pltpu.make_async_copy(pos_hbm.at[seq, :, pl.ds(pl.multiple_of(i * tb, tb), tb)], pos_s, sem.at[0])
        copy.start()
        copy.wait()
        pos_rows = jnp.broadcast_to(pos_s[...].astype(F32), (LANES, tb)).T
        ang = pos_rows * inv_ref[...]
        tab_s[1, 0] = jnp.cos(ang)
        tab_s[1, 1] = jnp.sin(ang)
        rot_s[0:1, :] = jnp.ones((1, LANES), F32)
        rot_s[1:2, :] = jnp.zeros((1, LANES), F32)

    xb = x_ref[0].astype(BF16)

    q = _dot(xb, wqk_s[:, c_q[0]:c_q[1]]) * (DK ** -0.5)
    k = _dot(xb, wqk_s[:, c_k[0]:c_k[1]])
    slot = jnp.where(consecutive, 0, 1)
    c_tab, s_tab = tab_s[slot, 0], tab_s[slot, 1]
    c_rot, s_rot = rot_s[0:1, :], rot_s[1:2, :]
    cos = c_rot * c_tab - s_rot * s_tab
    sin = s_rot * c_tab + c_rot * s_tab
    v = _dot(xb, w_ref[:, c_v[0]:c_v[1]]).astype(BF16)
    p = _dot(xb, w_ref[:, c_p[0]:c_p[1]])
    q1, q2 = q[:, :LANES], q[:, LANES:]
    k1, k2 = k[:, :LANES], k[:, LANES:]
    qr = jnp.concatenate([q1 * cos - q2 * sin, q1 * sin + q2 * cos], axis=1)
    kr = jnp.concatenate([k1 * cos - k2 * sin, k1 * sin + k2 * cos], axis=1)
    masks = [_head_mask(h, False).astype(BF16) for h in range(HEADS)]
    qr_b = qr.astype(BF16)
    qms = [jnp.concatenate([qr_b[sl] * masks[h] for h in range(HEADS)], axis=0)
           for sl in chunks]
    gate_c = _silu(_dot(xb, w_ref[:, c_gc[0]:c_gc[1]])).astype(BF16)

    pad_s[POOL_PAD:POOL_PAD + tb, :] = p
    t_top = i * tb + lax.broadcasted_iota(jnp.int32, (POOL_PAD, LANES), 0)
    pooled = []
    for g, win in enumerate(POOL_WINDOWS):
        cols = slice(g * LANES, (g + 1) * LANES)
        acc = pad_s[:, cols]
        shift = 1
        while shift < win:
            acc = acc + pltpu.roll(acc, shift, axis=0)
            shift *= 2
        cnt_top = jnp.minimum(t_top + 1, win).astype(F32)
        mean = jnp.concatenate([acc[POOL_PAD:2 * POOL_PAD] / cnt_top, acc[2 * POOL_PAD:] * (1.0 / win)],
                               axis=0)
        pooled.append((mean - p[:, cols]).astype(BF16))
    tail = pad_s[tb:tb + POOL_PAD, :]
    pad_s[0:POOL_PAD, :] = tail
    kts, upds = [], []
    for sl in chunks:
        kt = (kr[sl] * wst_ref[...]).T.astype(BF16)
        kts.append(kt)
        upd = [_dot(jnp.concatenate([kt[h * half:(h + 1) * half],
                                     kt[LANES + h * half:LANES + (h + 1) * half]], axis=0),
                    v[sl, h * DV:(h + 1) * DV]) for h in range(HEADS)]
        upds.append(jnp.concatenate([u[:half] for u in upd] + [u[half:] for u in upd], axis=0))
    gate_d = _silu(_dot(xb, w_ref[:, c_gd[0]:c_gd[1]])).astype(BF16)

    state = st_s[...]
    states = []
    for upd in upds:
        states.append(state.astype(BF16))
        state = dec_ref[...] * state + upd
    st_s[...] = state
    yd = jnp.concatenate([_dot(jnp.concatenate([pooled[2 * n], pooled[2 * n + 1]], axis=1), wp_ref[n])
                          for n in range(len(POOL_WINDOWS) // 2)], axis=1)
    fused = [_dot(qm, jnp.concatenate([s_b, kt], axis=1)) for qm, s_b, kt in zip(qms, states, kts)]
    o_d = yd * psc_ref[...] * gate_d
    rng = rng_ref[...]
    o_chunks = []
    for sl, f in zip(chunks, fused):
        o_inter = f[:, 0:DV] * wint_ref[...]
        sc = (f[:, DV:DV + c_len] * dmat_ref[...]).astype(BF16)
        o_heads = []
        for h in range(HEADS):
            rows = slice(h * c_len, (h + 1) * c_len)
            o = _dot(sc[rows], v[sl, h * DV:(h + 1) * DV]) + o_inter[rows]
            mu = jnp.mean(o, axis=-1, keepdims=True)
            oc = o - mu
            var = jnp.mean(oc * oc, axis=-1, keepdims=True)
            o_heads.append(oc * lax.rsqrt(var + LN_EPS) * rng[:, h * DV:(h + 1) * DV])
        o_chunks.append(jnp.concatenate(o_heads, axis=1))
    o_c = jnp.concatenate(o_chunks, axis=0) * gate_c
    o_ref[0] = _project_and_norm(x_ref, o_c, o_d, wout_ref, lng_ref, lnb_ref)


def _full_spec(a):
    nd = a.ndim
    return pl.BlockSpec(a.shape, lambda *_, _nd=nd: (0,) * _nd)


def _compiler_params():
    return pltpu.CompilerParams(dimension_semantics=("arbitrary", "arbitrary"),
                                vmem_limit_bytes=VMEM_LIMIT_BYTES)


def _row(a):
    return a.reshape(1, -1).astype(F32)


def _hbm_spec():
    return pl.BlockSpec(memory_space=pl.ANY)


def _stage(w):
    rows = min(w.shape[0], STAGE_BYTES // (4 * w.shape[1]) // SUBLANES * SUBLANES)
    return pltpu.VMEM((STAGE_SLOTS, rows, w.shape[1]), F32)


def _even_layer(x, w_in, w_a2, b_a, gla_norm_g, sgu_ln_g, sgu_ln_b, w_s, b_s, w_out, ln_g, ln_b):
    B, S, D = x.shape
    tb = TOKEN_BLOCK
    w_in_t = w_in.T
    params = (w_in_t, w_a2.astype(F32), _row(b_a), _row(gla_norm_g), _row(sgu_ln_g), _row(sgu_ln_b),
              w_s.astype(F32), b_s.astype(F32), w_out, _row(ln_g), _row(ln_b))
    in_hbm = (w_in_t, w_out)
    tok_spec = pl.BlockSpec((1, tb, D), lambda b, i: (b, i, 0))
    return pl.pallas_call(
        _even_kernel,
        out_shape=jax.ShapeDtypeStruct((B, S, D), F32),
        grid=(B, S // tb),
        in_specs=[tok_spec] + [_hbm_spec() if any(p is h for h in in_hbm) else _full_spec(p) for p in params],
        out_specs=tok_spec,
        scratch_shapes=[
            pltpu.VMEM((QK, DV), F32),
            pltpu.VMEM((QK, D), BF16),
            pltpu.VMEM((SGU_CHUNK, BRANCH), F32),
            pltpu.VMEM(w_in_t.shape, BF16),
            pltpu.VMEM(w_out.shape, BF16),
            _stage(w_in_t),
            _stage(w_out),
            pltpu.SemaphoreType.DMA((STAGE_SLOTS,)),
        ],
        compiler_params=_compiler_params(),
        name="even_layer_gla_sgu",
    )(x, *params)


def _retention_tables():
    c = RET_CHUNK
    log_gamma = np.log(1.0 - 2.0 ** (-5.0 - np.arange(HEADS, dtype=np.float64)))
    idx = np.arange(c, dtype=np.float64)
    rel = idx[:, None] - idx[None, :]
    dmat = np.where(rel >= 0, np.exp((idx[:, None] - (c - 1.0))[None] * log_gamma[:, None, None]), 0.0)
    w_inter = np.exp((idx + 1.0)[None] * log_gamma[:, None])
    w_state = np.exp((c - 1.0 - idx)[None] * log_gamma[:, None])
    lane_head = (np.arange(QK) % LANES) // (DK // 2)
    dmat = dmat.reshape(HEADS * c, c)
    wint = np.broadcast_to(w_inter[:, :, None], (HEADS, c, DV)).reshape(HEADS * c, DV)
    wst = w_state[lane_head, :].T
    dec = np.broadcast_to(np.exp(c * log_gamma)[lane_head][:, None], (QK, DV))
    inv = ROPE_BASE ** (-np.arange(DK // 2, dtype=np.float64) / (DK // 2))
    inv = np.tile(inv, HEADS)[None, :].astype(np.float32)
    off = np.arange(TOKEN_BLOCK, dtype=np.float64)[:, None] * inv.astype(np.float64)
    f = lambda a: jnp.asarray(np.ascontiguousarray(a), F32)
    return f(inv), f(np.cos(off)), f(np.sin(off)), f(dmat), f(wint), f(wst), f(dec)


def _odd_layer(x, positions, w_in, ret_norm_g, w_pool, pool_scale, w_out, ln_g, ln_b):
    B, S, D = x.shape
    tb = TOKEN_BLOCK
    j = np.arange(2 * QK)
    src = (j // QK) * QK + ((j % LANES) // (DK // 2)) * DK + ((j % QK) // LANES) * (DK // 2) + j % (DK // 2)
    perm = jnp.asarray(src[None, :] == np.arange(2 * QK)[:, None], BF16)
    params = (w_in, perm, *_retention_tables(), _row(ret_norm_g), w_pool.astype(F32), _row(pool_scale), w_out,
              _row(ln_g), _row(ln_b))
    in_hbm = (w_in, w_out)
    pos_blocks = positions.reshape(B, S // tb, tb)
    first = pos_blocks[:, :, 0]
    runs = jnp.all(pos_blocks == first[:, :, None] + jnp.arange(tb, dtype=positions.dtype), axis=-1)
    block_info = jnp.stack([first.astype(jnp.int32), runs.astype(jnp.int32)])
    tok_spec = pl.BlockSpec((1, tb, D), lambda b, i: (b, i, 0))
    return pl.pallas_call(
        _odd_kernel,
        out_shape=jax.ShapeDtypeStruct((B, S, D), F32),
        grid=(B, S // tb),
        in_specs=[pl.BlockSpec(memory_space=pltpu.SMEM), tok_spec, _hbm_spec()]
                 + [_hbm_spec() if any(p is h for h in in_hbm) else _full_spec(p) for p in params],
        out_specs=tok_spec,
        scratch_shapes=[
            pltpu.VMEM((QK, DV), F32),
            pltpu.VMEM((POOL_PAD + tb, BRANCH), F32),
            pltpu.VMEM((2, 2, tb, LANES), F32),
            pltpu.VMEM((2, LANES), F32),
            pltpu.VMEM((1, tb), jnp.int32),
            pltpu.VMEM((D, 2 * QK), BF16),
            pltpu.VMEM((len(POOL_WINDOWS) // 2, 2 * LANES, 2 * LANES), BF16),
            pltpu.VMEM(w_in.shape, BF16),
            pltpu.VMEM(w_out.shape, BF16),
            _stage(w_in),
            _stage(w_out),
            pltpu.SemaphoreType.DMA((STAGE_SLOTS,)),
        ],
        compiler_params=_compiler_params(),
        name="odd_layer_retention_pool",
    )(block_info, x, positions.reshape(B, 1, S), *params)


def kernel(x, positions, l0_w_in, l0_w_a2, l0_b_a, l0_gla_norm_g, l0_sgu_ln_g, l0_sgu_ln_b, l0_w_s, l0_b_s, l0_w_out, l0_ln_g, l0_ln_b, l1_w_in, l1_ret_norm_g, l1_w_pool, l1_pool_scale, l1_w_out, l1_ln_g, l1_ln_b):
    x = _even_layer(x, l0_w_in, l0_w_a2, l0_b_a, l0_gla_norm_g, l0_sgu_ln_g, l0_sgu_ln_b, l0_w_s, l0_b_s,
                    l0_w_out, l0_ln_g, l0_ln_b)
    return _odd_layer(x, positions, l1_w_in, l1_ret_norm_g, l1_w_pool, l1_pool_scale, l1_w_out, l1_ln_g, l1_ln_b)
```
